```python
import math
import jax, jax.numpy as jnp
from jax import lax
import numpy as np

D_MODEL = 2048
BATCH = 2
SEQ = 16384
DEPTH = 1

MIX_WIDTH = D_MODEL
SCONV_WIDTH = MIX_WIDTH // 2
SCONV_GROUPS = 8
SCONV_K = 3
GDN_HEAD_DIM = 128
GDN_V_HEADS = (MIX_WIDTH - SCONV_WIDTH) // GDN_HEAD_DIM
GDN_K_HEADS = GDN_V_HEADS // 2
GDN_KEY_DIM = GDN_K_HEADS * GDN_HEAD_DIM
GDN_VALUE_DIM = GDN_V_HEADS * GDN_HEAD_DIM
GDN_CONV_K = 4
GDN_CHUNK = 64
GDN_CONV_DIM = 2 * GDN_KEY_DIM + GDN_VALUE_DIM
IN_PROJ_DIM = 3 * SCONV_WIDTH + GDN_CONV_DIM + GDN_VALUE_DIM + 2 * GDN_V_HEADS
N_MEM = 256
XA_HEADS = 4
XA_HEAD_DIM = D_MODEL // XA_HEADS
N_GROUPS = 8
EXPERTS_PER_GROUP = 8
N_EXPERTS = N_GROUPS * EXPERTS_PER_GROUP
EXPERT_TOP_K = 2
EXPERT_FF = 512
MOE_BLOCK = 128
DEEPNORM_ALPHA = (2 * DEPTH) ** 0.25
DEEPNORM_BETA = (8 * DEPTH) ** -0.25
LN_EPS = 1e-5
RMS_EPS = 1e-6
L2_EPS = 1e-6

kernel_name = 'hybrid_sconv_gdn_memxattn_hmoe_deepnorm'


def _layer_norm(x, g, b):
    xf = x.astype(jnp.float32)
    mu = jnp.mean(xf, axis=-1, keepdims=True)
    var = jnp.mean(jnp.square(xf - mu), axis=-1, keepdims=True)
    return ((xf - mu) * lax.rsqrt(var + LN_EPS) * g.astype(jnp.float32) + b.astype(jnp.float32)).astype(x.dtype)


def _l2norm(t):
    tf = t.astype(jnp.float32)
    return tf * lax.rsqrt(jnp.sum(tf * tf, axis=-1, keepdims=True) + L2_EPS)


def _causal_dwconv(x, w):
    K = w.shape[0]
    S = x.shape[1]
    xp = jnp.pad(x, ((0, 0), (K - 1, 0), (0, 0)))
    y = xp[:, 0:S] * w[0]
    for j in range(1, K):
        y = y + xp[:, j:j + S] * w[j]
    return y


def _chunk_gated_delta_rule(q, k, v, g, beta):
    Bn, S, H, Dk = q.shape
    Dv = v.shape[-1]
    C = GDN_CHUNK
    N = S // C

    def chunks(t):
        return jnp.swapaxes(t.reshape(Bn, N, C, H, -1), 2, 3)

    q = chunks(q) * (Dk ** -0.5)
    k = chunks(k)
    v = chunks(v)
    g = jnp.cumsum(jnp.swapaxes(g.reshape(Bn, N, C, H), 2, 3), axis=-1)
    beta = jnp.swapaxes(beta.reshape(Bn, N, C, H), 2, 3)
    causal = jnp.tril(jnp.ones((C, C), dtype=bool))
    strict = jnp.tril(jnp.ones((C, C), dtype=bool), k=-1)
    decay = jnp.exp(jnp.where(causal, g[..., :, None] - g[..., None, :], -jnp.inf))
    k_beta = k * beta[..., None]
    lower = jnp.where(strict, jnp.einsum('bnhid,bnhjd->bnhij', k_beta, k) * decay, 0.0)
    rhs = jnp.concatenate([v * beta[..., None], k_beta * jnp.exp(g)[..., None]], axis=-1)
    sol = lax.linalg.triangular_solve(lower, rhs, left_side=True, lower=True, unit_diagonal=True)
    u, w = sol[..., :Dv], sol[..., Dv:]
    attn = jnp.einsum('bnhid,bnhjd->bnhij', q, k) * decay
    q_decay = q * jnp.exp(g)[..., None]
    g_last = g[..., -1]
    k_tail = k * jnp.exp(g_last[..., None] - g)[..., None]

    def step(state, inp):
        u_c, w_c, qd_c, a_c, kt_c, gl_c = inp
        v_new = u_c - jnp.einsum('bhck,bhkv->bhcv', w_c, state)
        o_c = jnp.einsum('bhck,bhkv->bhcv', qd_c, state) + jnp.einsum('bhij,bhjv->bhiv', a_c, v_new)
        state = state * jnp.exp(gl_c)[..., None, None] + jnp.einsum('bhck,bhcv->bhkv', kt_c, v_new)
        return state, o_c

    xs = tuple(jnp.moveaxis(t, 1, 0) for t in (u, w, q_decay, attn, k_tail, g_last))
    state0 = jnp.zeros((Bn, H, Dk, Dv), jnp.float32)
    _, o = lax.scan(step, state0, xs)
    return jnp.transpose(o, (1, 0, 3, 2, 4)).reshape(Bn, S, H, Dv)


def _hybrid_mixer(h, w_in, sconv_w, gdn_conv_w, gdn_a_log, gdn_dt_bias, gdn_norm_w, w_out):
    Bn, S, _ = h.shape
    f32 = jnp.float32
    splits = np.cumsum([SCONV_WIDTH, SCONV_WIDTH, SCONV_WIDTH, GDN_CONV_DIM, GDN_VALUE_DIM, GDN_V_HEADS]).tolist()
    b_gate, c_gate, xa, qkv, z, a, b = jnp.split(h @ w_in, splits, axis=-1)
    y_a = b_gate * _causal_dwconv(c_gate * xa, sconv_w)
    qkv = jax.nn.silu(_causal_dwconv(qkv, gdn_conv_w))
    q, k, v = jnp.split(qkv, [GDN_KEY_DIM, 2 * GDN_KEY_DIM], axis=-1)
    rep = GDN_V_HEADS // GDN_K_HEADS
    q = jnp.repeat(_l2norm(q.reshape(Bn, S, GDN_K_HEADS, GDN_HEAD_DIM)), rep, axis=2)
    k = jnp.repeat(_l2norm(k.reshape(Bn, S, GDN_K_HEADS, GDN_HEAD_DIM)), rep, axis=2)
    v = v.reshape(Bn, S, GDN_V_HEADS, GDN_HEAD_DIM).astype(f32)
    beta = jax.nn.sigmoid(b.astype(f32))
    g = -jnp.exp(gdn_a_log.astype(f32)) * jax.nn.softplus(a.astype(f32) + gdn_dt_bias.astype(f32))
    o = _chunk_gated_delta_rule(q, k, v, g, beta)
    zf = z.reshape(Bn, S, GDN_V_HEADS, GDN_HEAD_DIM).astype(f32)
    o = o * lax.rsqrt(jnp.mean(o * o, axis=-1, keepdims=True) + RMS_EPS) * gdn_norm_w.astype(f32) * jax.nn.silu(zf)
    y_b = o.reshape(Bn, S, GDN_VALUE_DIM).astype(h.dtype)
    return jnp.concatenate([y_a, y_b], axis=-1) @ w_out


def _memory_cross_attention(h, mem, wq, wk, wv, wo):
    Bn, S, D = h.shape
    M = mem.shape[1]
    q = (h @ wq).reshape(Bn, S, XA_HEADS, XA_HEAD_DIM).astype(jnp.float32)
    k = (mem @ wk).reshape(Bn, M, XA_HEADS, XA_HEAD_DIM).astype(jnp.float32)
    v = (mem @ wv).reshape(Bn, M, XA_HEADS, XA_HEAD_DIM)
    s = jnp.einsum('bshd,bmhd->bhsm', q, k) * (XA_HEAD_DIM ** -0.5)
    p = jax.nn.softmax(s, axis=-1).astype(h.dtype)
    o = jnp.einsum('bhsm,bmhd->bshd', p, v).reshape(Bn, S, D)
    return o @ wo


def _hierarchical_moe(h, w_group, w_expert_router, w1, w3, w2):
    Bn, S, D = h.shape
    T = Bn * S
    K = EXPERT_TOP_K
    xf = h.reshape(T, D)
    g_logits = (xf @ w_group).astype(jnp.float32)
    g_idx = jnp.argmax(g_logits, axis=-1)
    g_prob = jnp.take_along_axis(jax.nn.softmax(g_logits, axis=-1), g_idx[:, None], axis=1)[:, 0]
    e_logits = (xf @ w_expert_router).astype(jnp.float32).reshape(T, N_GROUPS, EXPERTS_PER_GROUP)
    e_logits = jnp.take_along_axis(e_logits, g_idx[:, None, None], axis=1)[:, 0]
    top_p, top_i = lax.top_k(jax.nn.softmax(e_logits, axis=-1), K)
    gate = g_prob[:, None] * top_p / jnp.sum(top_p, axis=-1, keepdims=True)
    expert_id = g_idx[:, None] * EXPERTS_PER_GROUP + top_i
    TK = T * K
    flat_e = expert_id.reshape(-1)
    flat_tok = jnp.repeat(jnp.arange(T, dtype=jnp.int32), K)
    flat_gate = gate.reshape(-1)
    order = jnp.argsort(flat_e)
    sorted_e = flat_e[order]
    counts = jnp.bincount(flat_e, length=N_EXPERTS)
    start = jnp.cumsum(counts) - counts
    padded = (counts + MOE_BLOCK - 1) // MOE_BLOCK * MOE_BLOCK
    padded_end = jnp.cumsum(padded)
    padded_start = padded_end - padded
    dest = padded_start[sorted_e] + jnp.arange(TK, dtype=jnp.int32) - start[sorted_e]
    P = ((TK + MOE_BLOCK - 1) // MOE_BLOCK) * MOE_BLOCK + N_EXPERTS * MOE_BLOCK
    n_blocks = P // MOE_BLOCK
    slot_tok = jnp.full((P,), T, jnp.int32).at[dest].set(flat_tok[order])
    slot_gate = jnp.zeros((P,), jnp.float32).at[dest].set(flat_gate[order])
    block_start = jnp.arange(n_blocks, dtype=jnp.int32) * MOE_BLOCK
    block_expert = jnp.minimum(jnp.searchsorted(padded_end, block_start, side='right'), N_EXPERTS - 1)
    x_pad = jnp.concatenate([xf, jnp.zeros((1, D), xf.dtype)], axis=0)
    xb = x_pad[slot_tok].reshape(n_blocks, MOE_BLOCK, D)

    def expert_block(args):
        xblk, e = args
        hid = jax.nn.silu(xblk @ w1[e]) * (xblk @ w3[e])
        return hid @ w2[e]

    yb = lax.map(expert_block, (xb, block_expert))
    y = jnp.zeros((T + 1, D), h.dtype).at[slot_tok].add(yb.reshape(P, D) * slot_gate[:, None].astype(h.dtype))
    return y[:T].reshape(Bn, S, D)


def setup_inputs(seed: int = 0) -> dict:
    key = jax.random.key(seed)
    ks = jax.random.split(key, 32)
    f32 = jnp.float32
    L = DEPTH

    def nrm(k, shape, scale):
        return jax.random.normal(k, shape, f32) * scale

    x = nrm(ks[0], (BATCH, SEQ, D_MODEL), 1.0)
    mem = nrm(ks[1], (BATCH, N_MEM, D_MODEL), 1.0)
    ln0_g = 1.0 + nrm(ks[2], (D_MODEL,), 0.02)
    ln0_b = nrm(ks[3], (D_MODEL,), 0.02)
    w_in = nrm(ks[4], (L, D_MODEL, IN_PROJ_DIM), D_MODEL ** -0.5)
    sconv_w = nrm(ks[5], (L, SCONV_K, SCONV_WIDTH), SCONV_K ** -0.5)
    gdn_conv_w = nrm(ks[6], (L, GDN_CONV_K, GDN_CONV_DIM), GDN_CONV_K ** -0.5)
    gdn_a_log = jnp.log(jax.random.uniform(ks[7], (L, GDN_V_HEADS), f32, 1.0, 16.0))
    dt = jnp.exp(jax.random.uniform(ks[8], (L, GDN_V_HEADS), f32, math.log(1e-3), math.log(1e-1)))
    gdn_dt_bias = dt + jnp.log(-jnp.expm1(-dt))
    gdn_norm_w = 1.0 + nrm(ks[9], (L, GDN_HEAD_DIM), 0.02)
    w_mix_out = nrm(ks[10], (L, MIX_WIDTH, D_MODEL), MIX_WIDTH ** -0.5 * DEEPNORM_BETA)
    ln1_g = 1.0 + nrm(ks[11], (L, D_MODEL), 0.02)
    ln1_b = nrm(ks[12], (L, D_MODEL), 0.02)
    xa_wq = nrm(ks[13], (L, D_MODEL, D_MODEL), D_MODEL ** -0.5)
    xa_wk = nrm(ks[14], (L, D_MODEL, D_MODEL), D_MODEL ** -0.5)
    xa_wv = nrm(ks[15], (L, D_MODEL, D_MODEL), D_MODEL ** -0.5)
    xa_wo = nrm(ks[16], (L, D_MODEL, D_MODEL), D_MODEL ** -0.5 * DEEPNORM_BETA)
    ln2_g = 1.0 + nrm(ks[17], (L, D_MODEL), 0.02)
    ln2_b = nrm(ks[18], (L, D_MODEL), 0.02)
    w_group = nrm(ks[19], (L, D_MODEL, N_GROUPS), D_MODEL ** -0.5)
    w_expert_router = nrm(ks[20], (L, D_MODEL, N_EXPERTS), D_MODEL ** -0.5)
    w1 = nrm(ks[21], (L, N_EXPERTS, D_MODEL, EXPERT_FF), D_MODEL ** -0.5)
    w3 = nrm(ks[22], (L, N_EXPERTS, D_MODEL, EXPERT_FF), D_MODEL ** -0.5)
    w2 = nrm(ks[23], (L, N_EXPERTS, EXPERT_FF, D_MODEL), EXPERT_FF ** -0.5 * DEEPNORM_BETA)
    ln3_g = 1.0 + nrm(ks[24], (L, D_MODEL), 0.02)
    ln3_b = nrm(ks[25], (L, D_MODEL), 0.02)
    return {'x': x, 'mem': mem, 'ln0_g': ln0_g, 'ln0_b': ln0_b, 'w_in': w_in, 'sconv_w': sconv_w,
            'gdn_conv_w': gdn_conv_w, 'gdn_a_log': gdn_a_log, 'gdn_dt_bias': gdn_dt_bias,
            'gdn_norm_w': gdn_norm_w, 'w_mix_out': w_mix_out, 'ln1_g': ln1_g, 'ln1_b': ln1_b,
            'xa_wq': xa_wq, 'xa_wk': xa_wk, 'xa_wv': xa_wv, 'xa_wo': xa_wo, 'ln2_g': ln2_g,
            'ln2_b': ln2_b, 'w_group': w_group, 'w_expert_router': w_expert_router, 'w1': w1,
            'w3': w3, 'w2': w2, 'ln3_g': ln3_g, 'ln3_b': ln3_b}


def reference(x, mem, ln0_g, ln0_b, w_in, sconv_w, gdn_conv_w, gdn_a_log, gdn_dt_bias, gdn_norm_w,
              w_mix_out, ln1_g, ln1_b, xa_wq, xa_wk, xa_wv, xa_wo, ln2_g, ln2_b, w_group,
              w_expert_router, w1, w3, w2, ln3_g, ln3_b):
    h = _layer_norm(x, ln0_g, ln0_b)
    for l in range(DEPTH):
        mix = _hybrid_mixer(h, w_in[l], sconv_w[l], gdn_conv_w[l], gdn_a_log[l], gdn_dt_bias[l],
                            gdn_norm_w[l], w_mix_out[l])
        h = _layer_norm(DEEPNORM_ALPHA * h + mix, ln1_g[l], ln1_b[l])
        xat = _memory_cross_attention(h, mem, xa_wq[l], xa_wk[l], xa_wv[l], xa_wo[l])
        h = _layer_norm(DEEPNORM_ALPHA * h + xat, ln2_g[l], ln2_b[l])
        ffn = _hierarchical_moe(h, w_group[l], w_expert_router[l], w1[l], w3[l], w2[l])
        h = _layer_norm(DEEPNORM_ALPHA * h + ffn, ln3_g[l], ln3_b[l])
    return h
```

```python
import functools

import jax
import jax.numpy as jnp
from jax import lax
from jax.experimental import pallas as pl
from jax.experimental.pallas import tpu as pltpu

F32 = jnp.float32
BF16 = jnp.bfloat16

LN_EPS = 1e-5
RMS_EPS = 1e-6
L2_EPS = 1e-6

SCONV_WIDTH = 1024
SCONV_K = 3
GDN_HEAD_DIM = 128
GDN_V_HEADS = 8
GDN_K_HEADS = 4
GDN_KEY_DIM = GDN_K_HEADS * GDN_HEAD_DIM
GDN_VALUE_DIM = GDN_V_HEADS * GDN_HEAD_DIM
GDN_CONV_K = 4
GDN_CHUNK = 64
XA_HEADS = 4
N_GROUPS = 8
EXPERTS_PER_GROUP = 8
N_EXPERTS = N_GROUPS * EXPERTS_PER_GROUP
EXPERT_TOP_K = 2

LANES = 128
HALO = 8
VMEM_LIMIT = 56 * 1024 * 1024

COL_BG = 0
COL_CG = SCONV_WIDTH
COL_XA = 2 * SCONV_WIDTH
COL_Q = 3 * SCONV_WIDTH
COL_K = COL_Q + GDN_KEY_DIM
COL_V = COL_K + GDN_KEY_DIM
COL_Z = COL_V + GDN_VALUE_DIM
COL_AB = COL_Z + GDN_VALUE_DIM

NT_DIMS = (((1,), (1,)), ((), ()))
TN_DIMS = (((0,), (0,)), ((), ()))


def _dot(a, b):
    return jnp.dot(a, b, preferred_element_type=F32)


def _split_bf16(x):
    hi = x.astype(BF16)
    lo = (x - hi.astype(F32)).astype(BF16)
    return hi, lo


def _layer_norm(x, g, b):
    mu = jnp.mean(x, axis=-1, keepdims=True)
    xc = x - mu
    var = jnp.mean(xc * xc, axis=-1, keepdims=True)
    return xc * lax.rsqrt(var + LN_EPS) * g + b


def _sigmoid(x):
    return 1.0 / (1.0 + jnp.exp(-x))


def _silu(x):
    return x * _sigmoid(x)


def _params(sem):
    return pltpu.CompilerParams(dimension_semantics=sem, vmem_limit_bytes=VMEM_LIMIT)


def _ln_inproj_kernel(x_ref, g_ref, b_ref, w_ref, wab_hi_ref, wab_lo_ref, o_ref, ab_ref, hn_ref):
    @pl.when(pl.program_id(1) == 0)
    def _():
        h = _layer_norm(x_ref[...], g_ref[...], b_ref[...])
        hi, lo = _split_bf16(h)
        hn_ref[...] = hi
        ab_ref[...] = (_dot(hi, wab_hi_ref[...]) + _dot(hi, wab_lo_ref[...])
                       + _dot(lo, wab_hi_ref[...]))

    o_ref[...] = _dot(hn_ref[...], w_ref[...]).astype(o_ref.dtype)


def _ln_inproj(x2d, g, b, w_main, wab_hi, wab_lo, tm=1024, tn=1024):
    T, D = x2d.shape
    N = w_main.shape[1]
    return pl.pallas_call(
        _ln_inproj_kernel,
        grid=(T // tm, N // tn),
        in_specs=[
            pl.BlockSpec((tm, D), lambda i, j: (i, 0)),
            pl.BlockSpec((1, D), lambda i, j: (0, 0)),
            pl.BlockSpec((1, D), lambda i, j: (0, 0)),
            pl.BlockSpec((D, tn), lambda i, j: (0, j)),
            pl.BlockSpec((D, LANES), lambda i, j: (0, 0)),
            pl.BlockSpec((D, LANES), lambda i, j: (0, 0)),
        ],
        out_specs=[
            pl.BlockSpec((tm, tn), lambda i, j: (i, j)),
            pl.BlockSpec((tm, LANES), lambda i, j: (i, 0)),
        ],
        out_shape=[
            jax.ShapeDtypeStruct((T, N), BF16),
            jax.ShapeDtypeStruct((T, LANES), F32),
        ],
        scratch_shapes=[pltpu.VMEM((tm, D), BF16)],
        compiler_params=_params(("parallel", "arbitrary")),
        name="ln_inproj",
    )(x2d, g, b, w_main, wab_hi, wab_lo)


def _causal_conv(buf, w_ref, col0, width, ksize, ts):
    acc = None
    for j in range(ksize):
        term = w_ref[j:j + 1, col0:col0 + width] * buf[pl.ds(HALO - ksize + 1 + j, ts), :]
        acc = term if acc is None else acc + term
    return acc


def _mixer_kernel(bg_ref, cg_ref, xa_ref, q_ref, k_ref, v_ref, z_ref, ab_ref,
                  sw_ref, cw_ref, gp_ref, nw_ref,
                  y_ref,
                  sbuf, qbuf, kbuf, vbuf, qn, kn, vc, g_s, b_s, state):
    ts = q_ref.shape[0]
    C = GDN_CHUNK
    Dh = GDN_HEAD_DIM

    @pl.when(pl.program_id(1) == 0)
    def _():
        sbuf[0:HALO, :] = jnp.zeros((HALO, sbuf.shape[1]), F32)
        qbuf[0:HALO, :] = jnp.zeros((HALO, qbuf.shape[1]), F32)
        kbuf[0:HALO, :] = jnp.zeros((HALO, kbuf.shape[1]), F32)
        vbuf[0:HALO, :] = jnp.zeros((HALO, vbuf.shape[1]), F32)
        state[...] = jnp.zeros(state.shape, F32)

    sbuf[HALO:HALO + ts, :] = cg_ref[...].astype(F32) * xa_ref[...].astype(F32)
    conv_a = _causal_conv(sbuf, sw_ref, 0, SCONV_WIDTH, SCONV_K, ts)
    y_ref[:, 0:SCONV_WIDTH] = (bg_ref[...].astype(F32) * conv_a).astype(y_ref.dtype)
    sbuf[0:HALO, :] = sbuf[ts:ts + HALO, :]

    qbuf[HALO:HALO + ts, :] = q_ref[...].astype(F32)
    kbuf[HALO:HALO + ts, :] = k_ref[...].astype(F32)
    vbuf[HALO:HALO + ts, :] = v_ref[...].astype(F32)
    q_act = _silu(_causal_conv(qbuf, cw_ref, 0, GDN_KEY_DIM, GDN_CONV_K, ts))
    k_act = _silu(_causal_conv(kbuf, cw_ref, GDN_KEY_DIM, GDN_KEY_DIM, GDN_CONV_K, ts))
    vc[...] = _silu(_causal_conv(vbuf, cw_ref, 2 * GDN_KEY_DIM, GDN_VALUE_DIM, GDN_CONV_K, ts))
    qbuf[0:HALO, :] = qbuf[ts:ts + HALO, :]
    kbuf[0:HALO, :] = kbuf[ts:ts + HALO, :]
    vbuf[0:HALO, :] = vbuf[ts:ts + HALO, :]
    for kh in range(GDN_K_HEADS):
        sl = slice(kh * Dh, (kh + 1) * Dh)
        qh = q_act[:, sl]
        khd = k_act[:, sl]
        qn[:, sl] = qh * lax.rsqrt(jnp.sum(qh * qh, axis=-1, keepdims=True) + L2_EPS) * (Dh ** -0.5)
        kn[:, sl] = khd * lax.rsqrt(jnp.sum(khd * khd, axis=-1, keepdims=True) + L2_EPS)

    ab = ab_ref[...]
    xg = ab + gp_ref[1:2, :]
    softplus = jnp.maximum(xg, 0.0) + jnp.log1p(jnp.exp(-jnp.abs(xg)))
    g = -jnp.exp(gp_ref[0:1, :]) * softplus
    b_s[...] = _sigmoid(pltpu.roll(ab, LANES - GDN_V_HEADS, axis=1))
    row = lax.broadcasted_iota(jnp.int32, (ts, ts), 0)
    col = lax.broadcasted_iota(jnp.int32, (ts, ts), 1)
    tri = jnp.where((row // C == col // C) & (col <= row), 1.0, 0.0).astype(BF16)
    g1 = g.astype(BF16)
    r1 = g - g1.astype(F32)
    g2 = r1.astype(BF16)
    g3 = (r1 - g2.astype(F32)).astype(BF16)
    g_s[...] = _dot(tri, g1) + _dot(tri, g2) + _dot(tri, g3)

    ri = lax.broadcasted_iota(jnp.int32, (C, C), 0)
    ci = lax.broadcasted_iota(jnp.int32, (C, C), 1)
    causal = ci <= ri
    strict = ci < ri
    eye = jnp.where(ri == ci, 1.0, 0.0).astype(F32)
    nw = nw_ref[...]

    def chunk_body(c, carry):
        r0 = pl.multiple_of(c * C, C)
        rows = pl.ds(r0, C)
        gc = g_s[rows, :]
        gct = gc.T
        bc = b_s[rows, :]
        eg = jnp.exp(gc)
        g_last = g_s[pl.ds(r0 + C - 1, 1), :]
        e_tail = jnp.exp(g_last - gc)
        e_last = jnp.exp(g_last)
        for kh in range(GDN_K_HEADS):
            ksl = slice(kh * Dh, (kh + 1) * Dh)
            qc = qn[rows, ksl]
            kc = kn[rows, ksl]
            kcb = kc.astype(BF16)
            kk = lax.dot_general(kcb, kcb, NT_DIMS, preferred_element_type=F32)
            qk = lax.dot_general(qc.astype(BF16), kcb, NT_DIMS, preferred_element_type=F32)
            for h in range(kh * 2, kh * 2 + 2):
                vsl = slice(h * Dh, (h + 1) * Dh)
                gcol = gc[:, h:h + 1]
                decay = jnp.exp(jnp.where(causal, gcol - gct[h:h + 1, :], -1e30))
                bcol = bc[:, h:h + 1]
                low = jnp.where(strict, kk * bcol * decay, 0.0)
                lb = low.astype(BF16)
                inv = eye - low
                pw = _dot(lb, lb)
                for it in range(5):
                    pwb = pw.astype(BF16)
                    inv = inv + _dot(inv.astype(BF16), pwb)
                    if it < 4:
                        pw = _dot(pwb, pwb)
                egc = eg[:, h:h + 1]
                rhs = jnp.concatenate([vc[rows, vsl] * bcol, kc * (bcol * egc)], axis=1)
                sol = _dot(inv.astype(BF16), rhs.astype(BF16))
                u = sol[:, :Dh]
                w = sol[:, Dh:]
                attn = (qk * decay).astype(BF16)
                qd = (qc * egc).astype(BF16)
                kt = (kc * e_tail[:, h:h + 1]).astype(BF16)
                s_old = state[h]
                sb = s_old.astype(BF16)
                v_new = u - _dot(w.astype(BF16), sb)
                vnb = v_new.astype(BF16)
                o = _dot(qd, sb) + _dot(attn, vnb)
                state[h] = s_old * e_last[:, h:h + 1] + lax.dot_general(
                    kt, vnb, TN_DIMS, preferred_element_type=F32)
                zc = z_ref[rows, vsl].astype(F32)
                ms = jnp.mean(o * o, axis=-1, keepdims=True)
                yb = o * lax.rsqrt(ms + RMS_EPS) * nw * _silu(zc)
                y_ref[rows, SCONV_WIDTH + h * Dh:SCONV_WIDTH + (h + 1) * Dh] = yb.astype(y_ref.dtype)
        return carry

    lax.fori_loop(0, ts // C, chunk_body, 0)


def _mixer(proj, ab, sconv_w, conv_w, gate_p, norm_w, batch, seq, ts=256):
    T = proj.shape[0]
    spb = seq // ts

    def rows(cb, width):
        return pl.BlockSpec((ts, width), lambda b, s, cb=cb: (b * spb + s, cb))

    def whole(a):
        return pl.BlockSpec(a.shape, lambda b, s: (0,) * a.ndim)

    return pl.pallas_call(
        _mixer_kernel,
        grid=(batch, spb),
        in_specs=[
            rows(COL_BG // SCONV_WIDTH, SCONV_WIDTH),
            rows(COL_CG // SCONV_WIDTH, SCONV_WIDTH),
            rows(COL_XA // SCONV_WIDTH, SCONV_WIDTH),
            rows(COL_Q // GDN_KEY_DIM, GDN_KEY_DIM),
            rows(COL_K // GDN_KEY_DIM, GDN_KEY_DIM),
            rows(COL_V // GDN_VALUE_DIM, GDN_VALUE_DIM),
            rows(COL_Z // GDN_VALUE_DIM, GDN_VALUE_DIM),
            pl.BlockSpec((ts, LANES), lambda b, s: (b * spb + s, 0)),
            whole(sconv_w), whole(conv_w), whole(gate_p), whole(norm_w),
        ],
        out_specs=pl.BlockSpec((ts, SCONV_WIDTH + GDN_VALUE_DIM), lambda b, s: (b * spb + s, 0)),
        out_shape=jax.ShapeDtypeStruct((T, SCONV_WIDTH + GDN_VALUE_DIM), BF16),
        scratch_shapes=[
            pltpu.VMEM((ts + HALO, SCONV_WIDTH), F32),
            pltpu.VMEM((ts + HALO, GDN_KEY_DIM), F32),
            pltpu.VMEM((ts + HALO, GDN_KEY_DIM), F32),
            pltpu.VMEM((ts + HALO, GDN_VALUE_DIM), F32),
            pltpu.VMEM((ts, GDN_KEY_DIM), F32),
            pltpu.VMEM((ts, GDN_KEY_DIM), F32),
            pltpu.VMEM((ts, GDN_VALUE_DIM), F32),
            pltpu.VMEM((ts, LANES), F32),
            pltpu.VMEM((ts, LANES), F32),
            pltpu.VMEM((GDN_V_HEADS, GDN_HEAD_DIM, GDN_HEAD_DIM), F32),
        ],
        compiler_params=_params(("arbitrary", "arbitrary")),
        name="mixer",
    )(proj, proj, proj, proj, proj, proj, proj, ab, sconv_w, conv_w, gate_p, norm_w)


def _outproj_kernel(alpha, y_ref, x_ref, g0_ref, b0_ref, w_ref, g1_ref, b1_ref, o_ref):
    h0 = _layer_norm(x_ref[...], g0_ref[...], b0_ref[...])
    r = alpha * h0 + _dot(y_ref[...], w_ref[...])
    o_ref[...] = _layer_norm(r, g1_ref[...], b1_ref[...]).astype(o_ref.dtype)


def _outproj(y, x2d, g0, b0, w, g1, b1, alpha, tm=512):
    T, D = x2d.shape
    vec = pl.BlockSpec((1, D), lambda i: (0, 0))
    return pl.pallas_call(
        functools.partial(_outproj_kernel, alpha),
        grid=(T // tm,),
        in_specs=[
            pl.BlockSpec((tm, y.shape[1]), lambda i: (i, 0)),
            pl.BlockSpec((tm, D), lambda i: (i, 0)),
            vec, vec,
            pl.BlockSpec(w.shape, lambda i: (0, 0)),
            vec, vec,
        ],
        out_specs=pl.BlockSpec((tm, D), lambda i: (i, 0)),
        out_shape=jax.ShapeDtypeStruct((T, D), BF16),
        compiler_params=_params(("parallel",)),
        name="outproj_ln1",
    )(y, x2d, g0, b0, w, g1, b1)


def _matmul_kernel(x_ref, w_ref, o_ref):
    o_ref[...] = _dot(x_ref[...].astype(BF16), w_ref[...]).astype(o_ref.dtype)


def _matmul(x, w, tn=1024):
    M, K = x.shape
    N = w.shape[1]
    return pl.pallas_call(
        _matmul_kernel,
        grid=(N // tn,),
        in_specs=[pl.BlockSpec((M, K), lambda j: (0, 0)), pl.BlockSpec((K, tn), lambda j: (0, j))],
        out_specs=pl.BlockSpec((M, tn), lambda j: (0, j)),
        out_shape=jax.ShapeDtypeStruct((M, N), BF16),
        compiler_params=_params(("parallel",)),
        name="kvproj",
    )(x, w)


def _xattn_kernel(alpha, h_ref, wq_ref, k_ref, v_ref, wo_ref, g_ref, b_ref, wr_hi_ref, wr_lo_ref,
                  o_ref, lg_ref, acc_ref):
    D = h_ref.shape[1]
    hd = D // XA_HEADS
    h1 = h_ref[...]
    acc_ref[...] = alpha * h1.astype(F32)
    for hh in range(XA_HEADS):
        sl = slice(hh * hd, (hh + 1) * hd)
        qh = _dot(h1, wq_ref[:, sl]).astype(BF16)
        s = lax.dot_general(qh, k_ref[:, sl], NT_DIMS, preferred_element_type=F32) * (hd ** -0.5)
        p = jnp.exp(s - jnp.max(s, axis=-1, keepdims=True))
        p = p / jnp.sum(p, axis=-1, keepdims=True)
        oh = _dot(p.astype(BF16), v_ref[:, sl]).astype(BF16)
        acc_ref[...] += _dot(oh, wo_ref[sl, :])
    h2 = _layer_norm(acc_ref[...], g_ref[...], b_ref[...])
    o_ref[...] = h2
    hi, lo = _split_bf16(h2)
    lg_ref[...] = _dot(hi, wr_hi_ref[...]) + _dot(hi, wr_lo_ref[...]) + _dot(lo, wr_hi_ref[...])


def _xattn(h1, wq, kv, wo, g, b, wr_hi, wr_lo, alpha, seq, n_mem, tm=512):
    T, D = h1.shape
    spb = seq // tm
    vec = pl.BlockSpec((1, D), lambda i: (0, 0))
    full = pl.BlockSpec((D, D), lambda i: (0, 0))
    return pl.pallas_call(
        functools.partial(_xattn_kernel, alpha),
        grid=(T // tm,),
        in_specs=[
            pl.BlockSpec((tm, D), lambda i: (i, 0)),
            full,
            pl.BlockSpec((n_mem, D), lambda i: (i // spb, 0)),
            pl.BlockSpec((n_mem, D), lambda i: (i // spb, 1)),
            full,
            vec, vec,
            pl.BlockSpec((D, LANES), lambda i: (0, 0)),
            pl.BlockSpec((D, LANES), lambda i: (0, 0)),
        ],
        out_specs=[
            pl.BlockSpec((tm, D), lambda i: (i, 0)),
            pl.BlockSpec((tm, LANES), lambda i: (i, 0)),
        ],
        out_shape=[
            jax.ShapeDtypeStruct((T, D), F32),
            jax.ShapeDtypeStruct((T, LANES), F32),
        ],
        scratch_shapes=[pltpu.VMEM((tm, D), F32)],
        compiler_params=_params(("parallel",)),
        name="xattn_ln2",
    )(h1, wq, kv, kv, wo, g, b, wr_hi, wr_lo)


def _route_kernel(lg_ref, o_ref):
    lg = lg_ref[...]
    lane = lax.broadcasted_iota(jnp.int32, lg.shape, 1)
    neg = -jnp.inf
    big = jnp.int32(LANES)
    gl = jnp.where(lane < N_GROUPS, lg, neg)
    gmax = jnp.max(gl, axis=-1, keepdims=True)
    g_idx = jnp.min(jnp.where(gl == gmax, lane, big), axis=-1, keepdims=True)
    g_prob = 1.0 / jnp.sum(jnp.exp(gl - gmax), axis=-1, keepdims=True)
    e_lane = lane - N_GROUPS
    in_group = (e_lane >= g_idx * EXPERTS_PER_GROUP) & (e_lane < (g_idx + 1) * EXPERTS_PER_GROUP)
    el = jnp.where(in_group, lg, neg)
    m1 = jnp.max(el, axis=-1, keepdims=True)
    i1 = jnp.min(jnp.where(el == m1, lane, big), axis=-1, keepdims=True)
    denom = jnp.sum(jnp.exp(el - m1), axis=-1, keepdims=True)
    el2 = jnp.where(lane == i1, neg, el)
    m2 = jnp.max(el2, axis=-1, keepdims=True)
    i2 = jnp.min(jnp.where(el2 == m2, lane, big), axis=-1, keepdims=True)
    p1 = 1.0 / denom
    p2 = jnp.exp(m2 - m1) / denom
    psum = p1 + p2
    gate1 = g_prob * p1 / psum
    gate2 = g_prob * p2 / psum
    out = jnp.where(lane == 0, (i1 - N_GROUPS).astype(F32),
                    jnp.where(lane == 1, (i2 - N_GROUPS).astype(F32),
                              jnp.where(lane == 2, gate1, jnp.where(lane == 3, gate2, 0.0))))
    o_ref[...] = out


def _route(logits, tm=1024):
    T = logits.shape[0]
    spec = pl.BlockSpec((tm, LANES), lambda i: (i, 0))
    return pl.pallas_call(
        _route_kernel,
        grid=(T // tm,),
        in_specs=[spec],
        out_specs=spec,
        out_shape=jax.ShapeDtypeStruct((T, LANES), F32),
        compiler_params=_params(("parallel",)),
        name="route",
    )(logits)


def _expert_kernel(bexp_ref, nval_ref, tok_ref, dst_ref, gate_ref, h_hbm, w1_ref, w3_ref, w2_ref,
                   y_hbm, xbuf, ybuf, w1b, w3b, w2b, sems):
    i = pl.program_id(0)
    nv = nval_ref[i]

    @pl.when(i == 0)
    def _():
        xbuf[...] = jnp.zeros(xbuf.shape, xbuf.dtype)

    def gather_copy(r):
        return pltpu.make_async_copy(h_hbm.at[pl.ds(tok_ref[0, 0, r], 1), :],
                                     xbuf.at[pl.ds(r, 1), :], sems.at[0])

    def scatter_copy(r):
        return pltpu.make_async_copy(ybuf.at[pl.ds(r, 1), :],
                                     y_hbm.at[pl.ds(dst_ref[0, 0, r], 1), :], sems.at[1])

    @pl.when(nv > 0)
    def _():
        def g_start(r, c):
            gather_copy(r).start()
            return c
        lax.fori_loop(0, nv, g_start, 0)

        prev = bexp_ref[jnp.maximum(i - 1, 0)]

        @pl.when((i == 0) | (prev != bexp_ref[i]))
        def _():
            w1b[...] = w1_ref[0].astype(BF16)
            w3b[...] = w3_ref[0].astype(BF16)
            w2b[...] = w2_ref[0].astype(BF16)

        def g_wait(r, c):
            gather_copy(r).wait()
            return c
        lax.fori_loop(0, nv, g_wait, 0)

        xb = xbuf[...].astype(BF16)
        hid = _silu(_dot(xb, w1b[...])) * _dot(xb, w3b[...])
        out = _dot(hid.astype(BF16), w2b[...])
        ybuf[...] = out * gate_ref[0]

        def s_start(r, c):
            scatter_copy(r).start()
            return c
        lax.fori_loop(0, nv, s_start, 0)

        def s_wait(r, c):
            scatter_copy(r).wait()
            return c
        lax.fori_loop(0, nv, s_wait, 0)


def _experts(h2, w1, w3, w2, bexp, nval, slot_tok, slot_dst, slot_gate, bm):
    T, D = h2.shape
    nb = bexp.shape[0]
    ff = w1.shape[2]
    smem_rows = pl.BlockSpec((1, 1, bm), lambda i, be, nv: (i, 0, 0), memory_space=pltpu.SMEM)
    grid_spec = pltpu.PrefetchScalarGridSpec(
        num_scalar_prefetch=2,
        grid=(nb,),
        in_specs=[
            smem_rows, smem_rows,
            pl.BlockSpec((1, bm, 1), lambda i, be, nv: (i, 0, 0)),
            pl.BlockSpec(memory_space=pl.ANY),
            pl.BlockSpec((1, D, ff), lambda i, be, nv: (be[i], 0, 0)),
            pl.BlockSpec((1, D, ff), lambda i, be, nv: (be[i], 0, 0)),
            pl.BlockSpec((1, ff, D), lambda i, be, nv: (be[i], 0, 0)),
        ],
        out_specs=pl.BlockSpec(memory_space=pl.ANY),
        scratch_shapes=[
            pltpu.VMEM((bm, D), F32),
            pltpu.VMEM((bm, D), F32),
            pltpu.VMEM((D, ff), BF16),
            pltpu.VMEM((D, ff), BF16),
            pltpu.VMEM((ff, D), BF16),
            pltpu.SemaphoreType.DMA((2,)),
        ],
    )
    return pl.pallas_call(
        _expert_kernel,
        grid_spec=grid_spec,
        out_shape=jax.ShapeDtypeStruct((EXPERT_TOP_K * T, D), F32),
        compiler_params=_params(("arbitrary",)),
        name="experts",
    )(bexp, nval, slot_tok.reshape(nb, 1, bm), slot_dst.reshape(nb, 1, bm),
      slot_gate.reshape(nb, bm, 1), h2, w1, w3, w2)


def _combine_kernel(alpha, h_ref, ya_ref, yb_ref, g_ref, b_ref, o_ref):
    r = alpha * h_ref[...] + (ya_ref[0] + yb_ref[0])
    o_ref[...] = _layer_norm(r, g_ref[...], b_ref[...])


def _combine(h2, y2, g, b, alpha, tm=512):
    T, D = h2.shape
    vec = pl.BlockSpec((1, D), lambda i: (0, 0))
    return pl.pallas_call(
        functools.partial(_combine_kernel, alpha),
        grid=(T // tm,),
        in_specs=[
            pl.BlockSpec((tm, D), lambda i: (i, 0)),
            pl.BlockSpec((1, tm, D), lambda i: (0, i, 0)),
            pl.BlockSpec((1, tm, D), lambda i: (1, i, 0)),
            vec, vec,
        ],
        out_specs=pl.BlockSpec((tm, D), lambda i: (i, 0)),
        out_shape=jax.ShapeDtypeStruct((T, D), F32),
        compiler_params=_params(("parallel",)),
        name="combine_ln3",
    )(h2, y2, y2, g, b)


def _routing_tables(route, n_tok, bm):
    K = EXPERT_TOP_K
    TK = n_tok * K
    flat_e = route[:, 0:K].astype(jnp.int32).reshape(-1)
    flat_gate = route[:, K:2 * K].reshape(-1)
    order = jnp.argsort(flat_e).astype(jnp.int32)
    sorted_e = flat_e[order]
    counts = jnp.bincount(flat_e, length=N_EXPERTS).astype(jnp.int32)
    start = jnp.cumsum(counts) - counts
    padded = (counts + bm - 1) // bm * bm
    padded_end = jnp.cumsum(padded)
    padded_start = padded_end - padded
    dest = padded_start[sorted_e] + jnp.arange(TK, dtype=jnp.int32) - start[sorted_e]
    n_slots = (TK + bm - 1) // bm * bm + N_EXPERTS * bm
    nb = n_slots // bm
    slot_a = jnp.zeros((n_slots,), jnp.int32).at[dest].set(order)
    slot_gate = jnp.zeros((n_slots,), F32).at[dest].set(flat_gate[order])
    block_start = jnp.arange(nb, dtype=jnp.int32) * bm
    bexp = jnp.minimum(jnp.searchsorted(padded_end, block_start, side='right'),
                       N_EXPERTS - 1).astype(jnp.int32)
    nval = jnp.clip(counts[bexp] - (block_start - padded_start[bexp]), 0, bm).astype(jnp.int32)
    slot_tok = slot_a // K
    slot_dst = (slot_a % K) * n_tok + slot_tok
    return bexp, nval, slot_tok, slot_dst, slot_gate


def kernel(x, mem, ln0_g, ln0_b, w_in, sconv_w, gdn_conv_w, gdn_a_log, gdn_dt_bias, gdn_norm_w,
           w_mix_out, ln1_g, ln1_b, xa_wq, xa_wk, xa_wv, xa_wo, ln2_g, ln2_b, w_group,
           w_expert_router, w1, w3, w2, ln3_g, ln3_b):
    B, S, D = x.shape
    T = B * S
    depth = w_in.shape[0]
    assert depth == 1, "single-layer stack: the out-projection kernel recomputes LayerNorm0(x)"
    alpha = float((2 * depth) ** 0.25)
    n_mem = mem.shape[1]
    bm = 256

    def vec(a):
        return a.reshape(1, -1).astype(F32)

    def pad_lanes(a):
        return jnp.pad(a, ((0, 0), (0, LANES - a.shape[1])))

    x2d = x.reshape(T, D)
    g0, b0 = vec(ln0_g), vec(ln0_b)
    l = 0
    w_main = w_in[l][:, :COL_AB].astype(BF16)
    wab_hi, wab_lo = _split_bf16(pad_lanes(w_in[l][:, COL_AB:]))
    proj, ab = _ln_inproj(x2d, g0, b0, w_main, wab_hi, wab_lo)
    gate_p = jnp.pad(jnp.stack([gdn_a_log[l], gdn_dt_bias[l]]).astype(F32),
                     ((0, HALO - 2), (0, LANES - GDN_V_HEADS)))
    y = _mixer(proj, ab, sconv_w[l].astype(F32), gdn_conv_w[l].astype(F32), gate_p,
               vec(gdn_norm_w[l]), B, S)
    h1 = _outproj(y, x2d, g0, b0, w_mix_out[l].astype(BF16), vec(ln1_g[l]), vec(ln1_b[l]), alpha)
    kv = _matmul(mem.reshape(B * n_mem, D),
                 jnp.concatenate([xa_wk[l], xa_wv[l]], axis=1).astype(BF16))
    wr_hi, wr_lo = _split_bf16(pad_lanes(jnp.concatenate([w_group[l], w_expert_router[l]], axis=1)))
    h2, logits = _xattn(h1, xa_wq[l].astype(BF16), kv, xa_wo[l].astype(BF16), vec(ln2_g[l]),
                        vec(ln2_b[l]), wr_hi, wr_lo, alpha, S, n_mem)
    route = _route(logits)
    bexp, nval, slot_tok, slot_dst, slot_gate = _routing_tables(route, T, bm)
    y2 = _experts(h2, w1[l], w3[l], w2[l], bexp, nval, slot_tok, slot_dst, slot_gate, bm)
    out = _combine(h2, y2.reshape(EXPERT_TOP_K, T, D), vec(ln3_g[l]), vec(ln3_b[l]), alpha)
    return out.reshape(B, S, D)
```

```python
import functools

import jax
import jax.numpy as jnp
from jax import lax
from jax.experimental import pallas as pl
from jax.experimental.pallas import tpu as pltpu

F32 = jnp.float32
BF16 = jnp.bfloat16

LN_EPS = 1e-5
RMS_EPS = 1e-6
L2_EPS = 1e-6

SCONV_WIDTH = 1024
SCONV_K = 3
GDN_HEAD_DIM = 128
GDN_V_HEADS = 8
GDN_K_HEADS = 4
GDN_KEY_DIM = GDN_K_HEADS * GDN_HEAD_DIM
GDN_VALUE_DIM = GDN_V_HEADS * GDN_HEAD_DIM
GDN_CONV_K = 4
GDN_CHUNK = 64
XA_HEADS = 4
N_GROUPS = 8
EXPERTS_PER_GROUP = 8
N_EXPERTS = N_GROUPS * EXPERTS_PER_GROUP
EXPERT_TOP_K = 2

LANES = 128
HALO = 8
VMEM_LIMIT = 56 * 1024 * 1024

COL_BG = 0
COL_CG = SCONV_WIDTH
COL_XA = 2 * SCONV_WIDTH
COL_Q = 3 * SCONV_WIDTH
COL_K = COL_Q + GDN_KEY_DIM
COL_V = COL_K + GDN_KEY_DIM
COL_Z = COL_V + GDN_VALUE_DIM
COL_AB = COL_Z + GDN_VALUE_DIM

NT_DIMS = (((1,), (1,)), ((), ()))
TN_DIMS = (((0,), (0,)), ((), ()))


def _dot(a, b):
    return jnp.dot(a, b, preferred_element_type=F32)


def _split_bf16(x):
    hi = x.astype(BF16)
    lo = (x - hi.astype(F32)).astype(BF16)
    return hi, lo


def _layer_norm(x, g, b):
    mu = jnp.mean(x, axis=-1, keepdims=True)
    xc = x - mu
    var = jnp.mean(xc * xc, axis=-1, keepdims=True)
    return xc * lax.rsqrt(var + LN_EPS) * g + b


def _sigmoid(x):
    return 1.0 / (1.0 + jnp.exp(-x))


def _silu(x):
    return x * _sigmoid(x)


def _params(sem):
    return pltpu.CompilerParams(dimension_semantics=sem, vmem_limit_bytes=VMEM_LIMIT)


def _ln_inproj_kernel(x_ref, g_ref, b_ref, w_ref, wab_hi_ref, wab_lo_ref, o_ref, ab_ref, hn_ref):
    @pl.when(pl.program_id(1) == 0)
    def _():
        h = _layer_norm(x_ref[...], g_ref[...], b_ref[...])
        hi, lo = _split_bf16(h)
        hn_ref[...] = hi
        ab_ref[...] = (_dot(hi, wab_hi_ref[...]) + _dot(hi, wab_lo_ref[...])
                       + _dot(lo, wab_hi_ref[...]))

    o_ref[...] = _dot(hn_ref[...], w_ref[...]).astype(o_ref.dtype)


def _ln_inproj(x2d, g, b, w_main, wab_hi, wab_lo, tm=1024, tn=1024):
    T, D = x2d.shape
    N = w_main.shape[1]
    return pl.pallas_call(
        _ln_inproj_kernel,
        grid=(T // tm, N // tn),
        in_specs=[
            pl.BlockSpec((tm, D), lambda i, j: (i, 0)),
            pl.BlockSpec((1, D), lambda i, j: (0, 0)),
            pl.BlockSpec((1, D), lambda i, j: (0, 0)),
            pl.BlockSpec((D, tn), lambda i, j: (0, j)),
            pl.BlockSpec((D, LANES), lambda i, j: (0, 0)),
            pl.BlockSpec((D, LANES), lambda i, j: (0, 0)),
        ],
        out_specs=[
            pl.BlockSpec((tm, tn), lambda i, j: (i, j)),
            pl.BlockSpec((tm, LANES), lambda i, j: (i, 0)),
        ],
        out_shape=[
            jax.ShapeDtypeStruct((T, N), BF16),
            jax.ShapeDtypeStruct((T, LANES), F32),
        ],
        scratch_shapes=[pltpu.VMEM((tm, D), BF16)],
        compiler_params=_params(("parallel", "arbitrary")),
        name="ln_inproj",
    )(x2d, g, b, w_main, wab_hi, wab_lo)


def _causal_conv(buf, w_ref, col0, width, ksize, ts):
    acc = None
    for j in range(ksize):
        term = w_ref[j:j + 1, col0:col0 + width] * buf[pl.ds(HALO - ksize + 1 + j, ts), :]
        acc = term if acc is None else acc + term
    return acc


def _mixer_kernel(bg_ref, cg_ref, xa_ref, q_ref, k_ref, v_ref, z_ref, ab_ref,
                  sw_ref, cw_ref, gp_ref, nw_ref,
                  y_ref,
                  sbuf, qbuf, kbuf, vbuf, qn, kn, vc, g_s, b_s, state):
    ts = q_ref.shape[0]
    C = GDN_CHUNK
    Dh = GDN_HEAD_DIM

    @pl.when(pl.program_id(1) == 0)
    def _():
        sbuf[0:HALO, :] = jnp.zeros((HALO, sbuf.shape[1]), F32)
        qbuf[0:HALO, :] = jnp.zeros((HALO, qbuf.shape[1]), F32)
        kbuf[0:HALO, :] = jnp.zeros((HALO, kbuf.shape[1]), F32)
        vbuf[0:HALO, :] = jnp.zeros((HALO, vbuf.shape[1]), F32)
        state[...] = jnp.zeros(state.shape, F32)

    sbuf[HALO:HALO + ts, :] = cg_ref[...].astype(F32) * xa_ref[...].astype(F32)
    conv_a = _causal_conv(sbuf, sw_ref, 0, SCONV_WIDTH, SCONV_K, ts)
    y_ref[:, 0:SCONV_WIDTH] = (bg_ref[...].astype(F32) * conv_a).astype(y_ref.dtype)
    sbuf[0:HALO, :] = sbuf[ts:ts + HALO, :]

    qbuf[HALO:HALO + ts, :] = q_ref[...].astype(F32)
    kbuf[HALO:HALO + ts, :] = k_ref[...].astype(F32)
    vbuf[HALO:HALO + ts, :] = v_ref[...].astype(F32)
    q_act = _silu(_causal_conv(qbuf, cw_ref, 0, GDN_KEY_DIM, GDN_CONV_K, ts))
    k_act = _silu(_causal_conv(kbuf, cw_ref, GDN_KEY_DIM, GDN_KEY_DIM, GDN_CONV_K, ts))
    vc[...] = _silu(_causal_conv(vbuf, cw_ref, 2 * GDN_KEY_DIM, GDN_VALUE_DIM, GDN_CONV_K, ts))
    qbuf[0:HALO, :] = qbuf[ts:ts + HALO, :]
    kbuf[0:HALO, :] = kbuf[ts:ts + HALO, :]
    vbuf[0:HALO, :] = vbuf[ts:ts + HALO, :]
    for kh in range(GDN_K_HEADS):
        sl = slice(kh * Dh, (kh + 1) * Dh)
        qh = q_act[:, sl]
        khd = k_act[:, sl]
        qn[:, sl] = qh * lax.rsqrt(jnp.sum(qh * qh, axis=-1, keepdims=True) + L2_EPS) * (Dh ** -0.5)
        kn[:, sl] = khd * lax.rsqrt(jnp.sum(khd * khd, axis=-1, keepdims=True) + L2_EPS)

    ab = ab_ref[...]
    xg = ab + gp_ref[1:2, :]
    softplus = jnp.maximum(xg, 0.0) + jnp.log1p(jnp.exp(-jnp.abs(xg)))
    g = -jnp.exp(gp_ref[0:1, :]) * softplus
    b_s[...] = _sigmoid(pltpu.roll(ab, LANES - GDN_V_HEADS, axis=1))
    row = lax.broadcasted_iota(jnp.int32, (ts, ts), 0)
    col = lax.broadcasted_iota(jnp.int32, (ts, ts), 1)
    tri = jnp.where((row // C == col // C) & (col <= row), 1.0, 0.0).astype(BF16)
    g1 = g.astype(BF16)
    r1 = g - g1.astype(F32)
    g2 = r1.astype(BF16)
    g3 = (r1 - g2.astype(F32)).astype(BF16)
    g_s[...] = _dot(tri, g1) + _dot(tri, g2) + _dot(tri, g3)

    ri = lax.broadcasted_iota(jnp.int32, (C, C), 0)
    ci = lax.broadcasted_iota(jnp.int32, (C, C), 1)
    causal = ci <= ri
    strict = ci < ri
    eye = jnp.where(ri == ci, 1.0, 0.0).astype(F32)
    nw = nw_ref[...]

    def chunk_body(c, carry):
        r0 = pl.multiple_of(c * C, C)
        rows = pl.ds(r0, C)
        gc = g_s[rows, :]
        gct = gc.T
        bc = b_s[rows, :]
        eg = jnp.exp(gc)
        g_last = g_s[pl.ds(r0 + C - 1, 1), :]
        e_tail = jnp.exp(g_last - gc)
        e_last = jnp.exp(g_last)
        heads = range(GDN_V_HEADS)
        qc, kc, kcb, kk, qk = [], [], [], [], []
        for kh in range(GDN_K_HEADS):
            ksl = slice(kh * Dh, (kh + 1) * Dh)
            qc.append(qn[rows, ksl])
            kc.append(kn[rows, ksl])
            kcb.append(kc[kh].astype(BF16))
        for kh in range(GDN_K_HEADS):
            kk.append(lax.dot_general(kcb[kh], kcb[kh], NT_DIMS, preferred_element_type=F32))
            qk.append(lax.dot_general(qc[kh].astype(BF16), kcb[kh], NT_DIMS, preferred_element_type=F32))
        decay, bcol, egc, inv, pw = [], [], [], [], []
        for h in heads:
            decay.append(jnp.exp(jnp.where(causal, gc[:, h:h + 1] - gct[h:h + 1, :], -1e30)))
            bcol.append(bc[:, h:h + 1])
            egc.append(eg[:, h:h + 1])
        for h in heads:
            low = jnp.where(strict, kk[h // 2] * bcol[h] * decay[h], 0.0)
            lb = low.astype(BF16)
            inv.append(eye - low)
            pw.append(_dot(lb, lb))
        for it in range(5):
            for h in heads:
                pwb = pw[h].astype(BF16)
                both = jnp.concatenate([inv[h], pw[h]], axis=0) if it < 4 else inv[h]
                prod = _dot(both.astype(BF16), pwb)
                inv[h] = inv[h] + prod[:C]
                if it < 4:
                    pw[h] = prod[C:]
        u, w = [], []
        for h in heads:
            vsl = slice(h * Dh, (h + 1) * Dh)
            rhs = jnp.concatenate([vc[rows, vsl] * bcol[h], kc[h // 2] * (bcol[h] * egc[h])], axis=1)
            sol = _dot(inv[h].astype(BF16), rhs.astype(BF16))
            u.append(sol[:, :Dh])
            w.append(sol[:, Dh:].astype(BF16))
        s_old, sb, vnb = [], [], []
        for h in heads:
            s_old.append(state[h])
            sb.append(s_old[h].astype(BF16))
            vnb.append((u[h] - _dot(w[h], sb[h])).astype(BF16))
        for h in heads:
            vsl = slice(h * Dh, (h + 1) * Dh)
            attn = (qk[h // 2] * decay[h]).astype(BF16)
            qd = (qc[h // 2] * egc[h]).astype(BF16)
            kt = (kc[h // 2] * e_tail[:, h:h + 1]).astype(BF16)
            o = _dot(qd, sb[h]) + _dot(attn, vnb[h])
            state[h] = s_old[h] * e_last[:, h:h + 1] + lax.dot_general(
                kt, vnb[h], TN_DIMS, preferred_element_type=F32)
            zc = z_ref[rows, vsl].astype(F32)
            ms = jnp.mean(o * o, axis=-1, keepdims=True)
            yb = o * lax.rsqrt(ms + RMS_EPS) * nw * _silu(zc)
            y_ref[rows, SCONV_WIDTH + h * Dh:SCONV_WIDTH + (h + 1) * Dh] = yb.astype(y_ref.dtype)
        return carry

    lax.fori_loop(0, ts // C, chunk_body, 0)


def _mixer(proj, ab, sconv_w, conv_w, gate_p, norm_w, batch, seq, ts=256):
    T = proj.shape[0]
    spb = seq // ts

    def rows(cb, width):
        return pl.BlockSpec((ts, width), lambda b, s, cb=cb: (b * spb + s, cb))

    def whole(a):
        return pl.BlockSpec(a.shape, lambda b, s: (0,) * a.ndim)

    return pl.pallas_call(
        _mixer_kernel,
        grid=(batch, spb),
        in_specs=[
            rows(COL_BG // SCONV_WIDTH, SCONV_WIDTH),
            rows(COL_CG // SCONV_WIDTH, SCONV_WIDTH),
            rows(COL_XA // SCONV_WIDTH, SCONV_WIDTH),
            rows(COL_Q // GDN_KEY_DIM, GDN_KEY_DIM),
            rows(COL_K // GDN_KEY_DIM, GDN_KEY_DIM),
            rows(COL_V // GDN_VALUE_DIM, GDN_VALUE_DIM),
            rows(COL_Z // GDN_VALUE_DIM, GDN_VALUE_DIM),
            pl.BlockSpec((ts, LANES), lambda b, s: (b * spb + s, 0)),
            whole(sconv_w), whole(conv_w), whole(gate_p), whole(norm_w),
        ],
        out_specs=pl.BlockSpec((ts, SCONV_WIDTH + GDN_VALUE_DIM), lambda b, s: (b * spb + s, 0)),
        out_shape=jax.ShapeDtypeStruct((T, SCONV_WIDTH + GDN_VALUE_DIM), BF16),
        scratch_shapes=[
            pltpu.VMEM((ts + HALO, SCONV_WIDTH), F32),
            pltpu.VMEM((ts + HALO, GDN_KEY_DIM), F32),
            pltpu.VMEM((ts + HALO, GDN_KEY_DIM), F32),
            pltpu.VMEM((ts + HALO, GDN_VALUE_DIM), F32),
            pltpu.VMEM((ts, GDN_KEY_DIM), F32),
            pltpu.VMEM((ts, GDN_KEY_DIM), F32),
            pltpu.VMEM((ts, GDN_VALUE_DIM), F32),
            pltpu.VMEM((ts, LANES), F32),
            pltpu.VMEM((ts, LANES), F32),
            pltpu.VMEM((GDN_V_HEADS, GDN_HEAD_DIM, GDN_HEAD_DIM), F32),
        ],
        compiler_params=_params(("arbitrary", "arbitrary")),
        name="mixer",
    )(proj, proj, proj, proj, proj, proj, proj, ab, sconv_w, conv_w, gate_p, norm_w)


def _outproj_kernel(alpha, y_ref, x_ref, g0_ref, b0_ref, w_ref, g1_ref, b1_ref, o_ref):
    h0 = _layer_norm(x_ref[...], g0_ref[...], b0_ref[...])
    r = alpha * h0 + _dot(y_ref[...], w_ref[...])
    o_ref[...] = _layer_norm(r, g1_ref[...], b1_ref[...]).astype(o_ref.dtype)


def _outproj(y, x2d, g0, b0, w, g1, b1, alpha, tm=512):
    T, D = x2d.shape
    vec = pl.BlockSpec((1, D), lambda i: (0, 0))
    return pl.pallas_call(
        functools.partial(_outproj_kernel, alpha),
        grid=(T // tm,),
        in_specs=[
            pl.BlockSpec((tm, y.shape[1]), lambda i: (i, 0)),
            pl.BlockSpec((tm, D), lambda i: (i, 0)),
            vec, vec,
            pl.BlockSpec(w.shape, lambda i: (0, 0)),
            vec, vec,
        ],
        out_specs=pl.BlockSpec((tm, D), lambda i: (i, 0)),
        out_shape=jax.ShapeDtypeStruct((T, D), BF16),
        compiler_params=_params(("parallel",)),
        name="outproj_ln1",
    )(y, x2d, g0, b0, w, g1, b1)


def _matmul_kernel(x_ref, w_ref, o_ref):
    o_ref[...] = _dot(x_ref[...].astype(BF16), w_ref[...]).astype(o_ref.dtype)


def _matmul(x, w, tn=1024):
    M, K = x.shape
    N = w.shape[1]
    return pl.pallas_call(
        _matmul_kernel,
        grid=(N // tn,),
        in_specs=[pl.BlockSpec((M, K), lambda j: (0, 0)), pl.BlockSpec((K, tn), lambda j: (0, j))],
        out_specs=pl.BlockSpec((M, tn), lambda j: (0, j)),
        out_shape=jax.ShapeDtypeStruct((M, N), BF16),
        compiler_params=_params(("parallel",)),
        name="kvproj",
    )(x, w)


def _xattn_kernel(alpha, h_ref, wq_ref, k_ref, v_ref, wo_ref, g_ref, b_ref, wr_hi_ref, wr_lo_ref,
                  o_ref, lg_ref, acc_ref):
    D = h_ref.shape[1]
    hd = D // XA_HEADS
    h1 = h_ref[...]
    acc_ref[...] = alpha * h1.astype(F32)
    for hh in range(XA_HEADS):
        sl = slice(hh * hd, (hh + 1) * hd)
        qh = _dot(h1, wq_ref[:, sl]).astype(BF16)
        s = lax.dot_general(qh, k_ref[:, sl], NT_DIMS, preferred_element_type=F32) * (hd ** -0.5)
        p = jnp.exp(s - jnp.max(s, axis=-1, keepdims=True))
        p = p / jnp.sum(p, axis=-1, keepdims=True)
        oh = _dot(p.astype(BF16), v_ref[:, sl]).astype(BF16)
        acc_ref[...] += _dot(oh, wo_ref[sl, :])
    h2 = _layer_norm(acc_ref[...], g_ref[...], b_ref[...])
    o_ref[...] = h2
    hi, lo = _split_bf16(h2)
    lg_ref[...] = _dot(hi, wr_hi_ref[...]) + _dot(hi, wr_lo_ref[...]) + _dot(lo, wr_hi_ref[...])


def _xattn(h1, wq, kv, wo, g, b, wr_hi, wr_lo, alpha, seq, n_mem, tm=512):
    T, D = h1.shape
    spb = seq // tm
    vec = pl.BlockSpec((1, D), lambda i: (0, 0))
    full = pl.BlockSpec((D, D), lambda i: (0, 0))
    return pl.pallas_call(
        functools.partial(_xattn_kernel, alpha),
        grid=(T // tm,),
        in_specs=[
            pl.BlockSpec((tm, D), lambda i: (i, 0)),
            full,
            pl.BlockSpec((n_mem, D), lambda i: (i // spb, 0)),
            pl.BlockSpec((n_mem, D), lambda i: (i // spb, 1)),
            full,
            vec, vec,
            pl.BlockSpec((D, LANES), lambda i: (0, 0)),
            pl.BlockSpec((D, LANES), lambda i: (0, 0)),
        ],
        out_specs=[
            pl.BlockSpec((tm, D), lambda i: (i, 0)),
            pl.BlockSpec((tm, LANES), lambda i: (i, 0)),
        ],
        out_shape=[
            jax.ShapeDtypeStruct((T, D), F32),
            jax.ShapeDtypeStruct((T, LANES), F32),
        ],
        scratch_shapes=[pltpu.VMEM((tm, D), F32)],
        compiler_params=_params(("parallel",)),
        name="xattn_ln2",
    )(h1, wq, kv, kv, wo, g, b, wr_hi, wr_lo)


def _route_kernel(lg_ref, o_ref):
    lg = lg_ref[...]
    lane = lax.broadcasted_iota(jnp.int32, lg.shape, 1)
    neg = -jnp.inf
    big = jnp.int32(LANES)
    gl = jnp.where(lane < N_GROUPS, lg, neg)
    gmax = jnp.max(gl, axis=-1, keepdims=True)
    g_idx = jnp.min(jnp.where(gl == gmax, lane, big), axis=-1, keepdims=True)
    g_prob = 1.0 / jnp.sum(jnp.exp(gl - gmax), axis=-1, keepdims=True)
    e_lane = lane - N_GROUPS
    in_group = (e_lane >= g_idx * EXPERTS_PER_GROUP) & (e_lane < (g_idx + 1) * EXPERTS_PER_GROUP)
    el = jnp.where(in_group, lg, neg)
    m1 = jnp.max(el, axis=-1, keepdims=True)
    i1 = jnp.min(jnp.where(el == m1, lane, big), axis=-1, keepdims=True)
    denom = jnp.sum(jnp.exp(el - m1), axis=-1, keepdims=True)
    el2 = jnp.where(lane == i1, neg, el)
    m2 = jnp.max(el2, axis=-1, keepdims=True)
    i2 = jnp.min(jnp.where(el2 == m2, lane, big), axis=-1, keepdims=True)
    p1 = 1.0 / denom
    p2 = jnp.exp(m2 - m1) / denom
    psum = p1 + p2
    gate1 = g_prob * p1 / psum
    gate2 = g_prob * p2 / psum
    out = jnp.where(lane == 0, (i1 - N_GROUPS).astype(F32),
                    jnp.where(lane == 1, (i2 - N_GROUPS).astype(F32),
                              jnp.where(lane == 2, gate1, jnp.where(lane == 3, gate2, 0.0))))
    o_ref[...] = out


def _route(logits, tm=1024):
    T = logits.shape[0]
    spec = pl.BlockSpec((tm, LANES), lambda i: (i, 0))
    return pl.pallas_call(
        _route_kernel,
        grid=(T // tm,),
        in_specs=[spec],
        out_specs=spec,
        out_shape=jax.ShapeDtypeStruct((T, LANES), F32),
        compiler_params=_params(("parallel",)),
        name="route",
    )(logits)


def _expert_kernel(bexp_ref, nval_ref, off_ref, order_ref, h_hbm, w1_ref, w3_ref, w2_ref,
                   y_hbm, xbuf, ybuf, w1b, w3b, w2b, gsem, ssem):
    i = pl.program_id(0)
    bm = xbuf.shape[1]
    n_tok = h_hbm.shape[0]
    spare_row0 = EXPERT_TOP_K * n_tok
    slot = lax.rem(i, 2)
    other = 1 - slot
    nv = nval_ref[i]
    nv_prev = jnp.where(i > 0, nval_ref[jnp.maximum(i - 1, 0)], 0)

    def gather_row(base, r, dst_slot):
        tok = lax.shift_right_logical(order_ref[base + r], 1)
        pltpu.make_async_copy(h_hbm.at[pl.ds(tok, 1), :], xbuf.at[dst_slot, pl.ds(r, 1), :], gsem).start()

    def scatter_row(base, r, nvalid, src_slot):
        a = order_ref[base + r]
        real = (a & 1) * n_tok + lax.shift_right_logical(a, 1)
        dst = jnp.where(r < nvalid, real, spare_row0 + r)
        pltpu.make_async_copy(ybuf.at[src_slot, pl.ds(r, 1), :], y_hbm.at[pl.ds(dst, 1), :], ssem).start()

    def wait_block(buf, s, sem):
        pltpu.make_async_copy(buf.at[s], buf.at[s], sem).wait()

    @pl.when(i == 0)
    def _():
        ybuf[...] = jnp.zeros(ybuf.shape, ybuf.dtype)

        def body(r, c):
            gather_row(off_ref[0], r, 0)
            return c
        lax.fori_loop(0, bm, body, 0)

    @pl.when(nv > 0)
    def _():
        wait_block(xbuf, slot, gsem)

        @pl.when(i > 0)
        def _():
            wait_block(ybuf, slot, ssem)

        @pl.when((i == 0) | (bexp_ref[jnp.maximum(i - 1, 0)] != bexp_ref[i]))
        def _():
            w1b[...] = w1_ref[0].astype(BF16)
            w3b[...] = w3_ref[0].astype(BF16)
            w2b[...] = w2_ref[0].astype(BF16)

        next_base = off_ref[i + 1]
        prev_base = off_ref[jnp.maximum(i - 1, 0)]

        def compute(cur, oth):
            for r in range(bm):
                gather_row(next_base, r, oth)
                scatter_row(prev_base, r, nv_prev, oth)
            xb = xbuf[cur].astype(BF16)
            hid = _silu(_dot(xb, w1b[...])) * _dot(xb, w3b[...])
            ybuf[cur] = _dot(hid.astype(BF16), w2b[...])

        @pl.when(slot == 0)
        def _():
            compute(0, 1)

        @pl.when(slot == 1)
        def _():
            compute(1, 0)

    @pl.when((nv == 0) & (nv_prev > 0))
    def _():
        wait_block(xbuf, slot, gsem)
        wait_block(ybuf, slot, ssem)
        prev_base = off_ref[i - 1]

        def body(r, c):
            scatter_row(prev_base, r, nv_prev, other)
            return c
        lax.fori_loop(0, bm, body, 0)
        wait_block(ybuf, other, ssem)


def _experts(h2, w1, w3, w2, bexp, nval, off, order, bm):
    T, D = h2.shape
    nb = bexp.shape[0]
    ff = w1.shape[2]
    grid_spec = pltpu.PrefetchScalarGridSpec(
        num_scalar_prefetch=4,
        grid=(nb,),
        in_specs=[
            pl.BlockSpec(memory_space=pl.ANY),
            pl.BlockSpec((1, D, ff), lambda i, be, nv, of, od: (be[i], 0, 0)),
            pl.BlockSpec((1, D, ff), lambda i, be, nv, of, od: (be[i], 0, 0)),
            pl.BlockSpec((1, ff, D), lambda i, be, nv, of, od: (be[i], 0, 0)),
        ],
        out_specs=pl.BlockSpec(memory_space=pl.ANY),
        scratch_shapes=[
            pltpu.VMEM((2, bm, D), F32),
            pltpu.VMEM((2, bm, D), F32),
            pltpu.VMEM((D, ff), BF16),
            pltpu.VMEM((D, ff), BF16),
            pltpu.VMEM((ff, D), BF16),
            pltpu.SemaphoreType.DMA(()),
            pltpu.SemaphoreType.DMA(()),
        ],
    )
    return pl.pallas_call(
        _expert_kernel,
        grid_spec=grid_spec,
        out_shape=jax.ShapeDtypeStruct((EXPERT_TOP_K * T + bm, D), F32),
        compiler_params=_params(("arbitrary",)),
        name="experts",
    )(bexp, nval, off, order, h2, w1, w3, w2)


def _combine_kernel(alpha, h_ref, ya_ref, yb_ref, rt_ref, g_ref, b_ref, o_ref):
    rt = rt_ref[...]
    ffn = ya_ref[...] * rt[:, 2:3] + yb_ref[...] * rt[:, 3:4]
    o_ref[...] = _layer_norm(alpha * h_ref[...] + ffn, g_ref[...], b_ref[...])


def _combine(h2, y2, route, g, b, alpha, tm=512):
    T, D = h2.shape
    nt = T // tm
    vec = pl.BlockSpec((1, D), lambda i: (0, 0))
    return pl.pallas_call(
        functools.partial(_combine_kernel, alpha),
        grid=(nt,),
        in_specs=[
            pl.BlockSpec((tm, D), lambda i: (i, 0)),
            pl.BlockSpec((tm, D), lambda i: (i, 0)),
            pl.BlockSpec((tm, D), lambda i: (nt + i, 0)),
            pl.BlockSpec((tm, LANES), lambda i: (i, 0)),
            vec, vec,
        ],
        out_specs=pl.BlockSpec((tm, D), lambda i: (i, 0)),
        out_shape=jax.ShapeDtypeStruct((T, D), F32),
        compiler_params=_params(("parallel",)),
        name="combine_ln3",
    )(h2, y2, y2, route, g, b)


def _routing_tables(route, n_tok, bm):
    K = EXPERT_TOP_K
    TK = n_tok * K
    flat_e = route[:, 0:K].astype(jnp.int32).reshape(-1)
    order = jnp.argsort(flat_e).astype(jnp.int32)
    experts = jnp.arange(N_EXPERTS, dtype=jnp.int32)
    counts = jnp.sum((flat_e[:, None] == experts[None, :]).astype(jnp.int32), axis=0)
    start = jnp.cumsum(counts) - counts
    padded = (counts + bm - 1) // bm * bm
    padded_end = jnp.cumsum(padded)
    padded_start = padded_end - padded
    nb = TK // bm + N_EXPERTS
    block_start = jnp.arange(nb, dtype=jnp.int32) * bm
    bexp = jnp.minimum(jnp.sum((block_start[:, None] >= padded_end[None, :]).astype(jnp.int32), axis=1),
                       N_EXPERTS - 1)
    rank0 = block_start - padded_start[bexp]
    nval = jnp.clip(counts[bexp] - rank0, 0, bm).astype(jnp.int32)
    off = jnp.clip(start[bexp] + rank0, 0, TK).astype(jnp.int32)
    order = jnp.concatenate([order, jnp.zeros((bm,), jnp.int32)])
    return bexp.astype(jnp.int32), nval, off, order


def kernel(x, mem, ln0_g, ln0_b, w_in, sconv_w, gdn_conv_w, gdn_a_log, gdn_dt_bias, gdn_norm_w,
           w_mix_out, ln1_g, ln1_b, xa_wq, xa_wk, xa_wv, xa_wo, ln2_g, ln2_b, w_group,
           w_expert_router, w1, w3, w2, ln3_g, ln3_b):
    B, S, D = x.shape
    T = B * S
    depth = w_in.shape[0]
    assert depth == 1, "single-layer stack: the out-projection kernel recomputes LayerNorm0(x)"
    alpha = float((2 * depth) ** 0.25)
    n_mem = mem.shape[1]
    bm = 256

    def vec(a):
        return a.reshape(1, -1).astype(F32)

    def pad_lanes(a):
        return jnp.pad(a, ((0, 0), (0, LANES - a.shape[1])))

    x2d = x.reshape(T, D)
    g0, b0 = vec(ln0_g), vec(ln0_b)
    l = 0
    w_main = w_in[l][:, :COL_AB].astype(BF16)
    wab_hi, wab_lo = _split_bf16(pad_lanes(w_in[l][:, COL_AB:]))
    proj, ab = _ln_inproj(x2d, g0, b0, w_main, wab_hi, wab_lo)
    gate_p = jnp.pad(jnp.stack([gdn_a_log[l], gdn_dt_bias[l]]).astype(F32),
                     ((0, HALO - 2), (0, LANES - GDN_V_HEADS)))
    y = _mixer(proj, ab, sconv_w[l].astype(F32), gdn_conv_w[l].astype(F32), gate_p,
               vec(gdn_norm_w[l]), B, S)
    h1 = _outproj(y, x2d, g0, b0, w_mix_out[l].astype(BF16), vec(ln1_g[l]), vec(ln1_b[l]), alpha)
    kv = _matmul(mem.reshape(B * n_mem, D),
                 jnp.concatenate([xa_wk[l], xa_wv[l]], axis=1).astype(BF16))
    wr_hi, wr_lo = _split_bf16(pad_lanes(jnp.concatenate([w_group[l], w_expert_router[l]], axis=1)))
    h2, logits = _xattn(h1, xa_wq[l].astype(BF16), kv, xa_wo[l].astype(BF16), vec(ln2_g[l]),
                        vec(ln2_b[l]), wr_hi, wr_lo, alpha, S, n_mem)
    route = _route(logits)
    bexp, nval, off, order = _routing_tables(route, T, bm)
    y2 = _experts(h2, w1[l], w3[l], w2[l], bexp, nval, off, order, bm)
    out = _combine(h2, y2, route, vec(ln3_g[l]), vec(ln3_b[l]), alpha)
    return out.reshape(B, S, D)
```

```python
import functools

import jax
import jax.numpy as jnp
from jax import lax
from jax.experimental import pallas as pl
from jax.experimental.pallas import tpu as pltpu

F32 = jnp.float32
BF16 = jnp.bfloat16

LN_EPS = 1e-5
RMS_EPS = 1e-6
L2_EPS = 1e-6

SCONV_WIDTH = 1024
SCONV_K = 3
GDN_HEAD_DIM = 128
GDN_V_HEADS = 8
GDN_K_HEADS = 4
GDN_KEY_DIM = GDN_K_HEADS * GDN_HEAD_DIM
GDN_VALUE_DIM = GDN_V_HEADS * GDN_HEAD_DIM
GDN_CONV_K = 4
GDN_CHUNK = 64
XA_HEADS = 4
N_GROUPS = 8
EXPERTS_PER_GROUP = 8
N_EXPERTS = N_GROUPS * EXPERTS_PER_GROUP
EXPERT_TOP_K = 2

LANES = 128
HALO = 8
VMEM_LIMIT = 56 * 1024 * 1024

COL_BG = 0
COL_CG = SCONV_WIDTH
COL_XA = 2 * SCONV_WIDTH
COL_Q = 3 * SCONV_WIDTH
COL_K = COL_Q + GDN_KEY_DIM
COL_V = COL_K + GDN_KEY_DIM
COL_Z = COL_V + GDN_VALUE_DIM
COL_AB = COL_Z + GDN_VALUE_DIM

NT_DIMS = (((1,), (1,)), ((), ()))
TN_DIMS = (((0,), (0,)), ((), ()))


def _dot(a, b):
    return jnp.dot(a, b, preferred_element_type=F32)


def _split_bf16(x):
    hi = x.astype(BF16)
    lo = (x - hi.astype(F32)).astype(BF16)
    return hi, lo


def _layer_norm(x, g, b):
    mu = jnp.mean(x, axis=-1, keepdims=True)
    xc = x - mu
    var = jnp.mean(xc * xc, axis=-1, keepdims=True)
    return xc * lax.rsqrt(var + LN_EPS) * g + b


def _sigmoid(x):
    return 1.0 / (1.0 + jnp.exp(-x))


def _silu(x):
    return x * _sigmoid(x)


def _params(sem):
    return pltpu.CompilerParams(dimension_semantics=sem, vmem_limit_bytes=VMEM_LIMIT)


def _ln_inproj_kernel(x_ref, g_ref, b_ref, w_ref, wab_hi_ref, wab_lo_ref, o_ref, ab_ref, hn_ref):
    @pl.when(pl.program_id(1) == 0)
    def _():
        h = _layer_norm(x_ref[...], g_ref[...], b_ref[...])
        hi, lo = _split_bf16(h)
        hn_ref[...] = hi
        ab_ref[...] = (_dot(hi, wab_hi_ref[...]) + _dot(hi, wab_lo_ref[...])
                       + _dot(lo, wab_hi_ref[...]))

    o_ref[...] = _dot(hn_ref[...], w_ref[...]).astype(o_ref.dtype)


def _ln_inproj(x2d, g, b, w_main, wab_hi, wab_lo, tm=1024, tn=1024):
    T, D = x2d.shape
    N = w_main.shape[1]
    return pl.pallas_call(
        _ln_inproj_kernel,
        grid=(T // tm, N // tn),
        in_specs=[
            pl.BlockSpec((tm, D), lambda i, j: (i, 0)),
            pl.BlockSpec((1, D), lambda i, j: (0, 0)),
            pl.BlockSpec((1, D), lambda i, j: (0, 0)),
            pl.BlockSpec((D, tn), lambda i, j: (0, j)),
            pl.BlockSpec((D, LANES), lambda i, j: (0, 0)),
            pl.BlockSpec((D, LANES), lambda i, j: (0, 0)),
        ],
        out_specs=[
            pl.BlockSpec((tm, tn), lambda i, j: (i, j)),
            pl.BlockSpec((tm, LANES), lambda i, j: (i, 0)),
        ],
        out_shape=[
            jax.ShapeDtypeStruct((T, N), BF16),
            jax.ShapeDtypeStruct((T, LANES), F32),
        ],
        scratch_shapes=[pltpu.VMEM((tm, D), BF16)],
        compiler_params=_params(("parallel", "arbitrary")),
        name="ln_inproj",
    )(x2d, g, b, w_main, wab_hi, wab_lo)


CONV_ROWS = 64


def _conv_taps(buf, b, lanes, w_ref, w_col0, ksize, r0):
    w_lanes = slice(w_col0 + lanes.start, w_col0 + lanes.stop)
    acc = None
    for j in range(ksize):
        start = HALO - ksize + 1 + j + r0
        term = w_ref[j:j + 1, w_lanes] * buf[b, start:start + CONV_ROWS, lanes]
        acc = term if acc is None else acc + term
    return acc


def _mixer_kernel(bg_ref, cg_ref, xa_ref, q_ref, k_ref, v_ref, z_ref, ab_ref,
                  sw_ref, cw_ref, gp_ref, nw_ref,
                  y_ref,
                  sbuf, qbuf, kbuf, vbuf, qn, kn, vc, g_s, b_s, state):
    nbatch, ts = q_ref.shape[0], q_ref.shape[1]
    C = GDN_CHUNK
    Dh = GDN_HEAD_DIM
    stage_bufs = (sbuf, qbuf, kbuf, vbuf)

    @pl.when(pl.program_id(0) == 0)
    def _():
        for buf in stage_bufs:
            buf[:, 0:HALO, :] = jnp.zeros((nbatch, HALO, buf.shape[2]), F32)
        state[...] = jnp.zeros(state.shape, F32)

    row = lax.broadcasted_iota(jnp.int32, (ts, ts), 0)
    col = lax.broadcasted_iota(jnp.int32, (ts, ts), 1)
    tri = jnp.where((row // C == col // C) & (col <= row), 1.0, 0.0).astype(BF16)

    for b in range(nbatch):
        for r0 in range(0, ts, CONV_ROWS):
            rows = slice(r0, r0 + CONV_ROWS)
            srows = slice(HALO + r0, HALO + r0 + CONV_ROWS)
            for cb in range(SCONV_WIDTH // LANES):
                ls = slice(cb * LANES, (cb + 1) * LANES)
                sbuf[b, srows, ls] = cg_ref[b, rows, ls].astype(F32) * xa_ref[b, rows, ls].astype(F32)
                conv = _conv_taps(sbuf, b, ls, sw_ref, 0, SCONV_K, r0)
                y_ref[b, rows, ls] = (bg_ref[b, rows, ls].astype(F32) * conv).astype(y_ref.dtype)
            for kh in range(GDN_K_HEADS):
                ls = slice(kh * Dh, (kh + 1) * Dh)
                qbuf[b, srows, ls] = q_ref[b, rows, ls].astype(F32)
                kbuf[b, srows, ls] = k_ref[b, rows, ls].astype(F32)
                qa = _silu(_conv_taps(qbuf, b, ls, cw_ref, 0, GDN_CONV_K, r0))
                ka = _silu(_conv_taps(kbuf, b, ls, cw_ref, GDN_KEY_DIM, GDN_CONV_K, r0))
                qn[b, rows, ls] = qa * (lax.rsqrt(jnp.sum(qa * qa, axis=-1, keepdims=True) + L2_EPS) * (Dh ** -0.5))
                kn[b, rows, ls] = ka * lax.rsqrt(jnp.sum(ka * ka, axis=-1, keepdims=True) + L2_EPS)
            for h in range(GDN_V_HEADS):
                ls = slice(h * Dh, (h + 1) * Dh)
                vbuf[b, srows, ls] = v_ref[b, rows, ls].astype(F32)
                vc[b, rows, ls] = _silu(_conv_taps(vbuf, b, ls, cw_ref, 2 * GDN_KEY_DIM, GDN_CONV_K, r0))
        for buf in stage_bufs:
            buf[b, 0:HALO, :] = buf[b, ts:ts + HALO, :]

        ab = ab_ref[b]
        xg = ab + gp_ref[1:2, :]
        softplus = jnp.maximum(xg, 0.0) + jnp.log1p(jnp.exp(-jnp.abs(xg)))
        g = -jnp.exp(gp_ref[0:1, :]) * softplus
        b_s[b] = _sigmoid(pltpu.roll(ab, LANES - GDN_V_HEADS, axis=1))
        g1 = g.astype(BF16)
        r1 = g - g1.astype(F32)
        g2 = r1.astype(BF16)
        g3 = (r1 - g2.astype(F32)).astype(BF16)
        g_s[b] = _dot(tri, g1) + _dot(tri, g2) + _dot(tri, g3)

    ri = lax.broadcasted_iota(jnp.int32, (C, C), 0)
    ci = lax.broadcasted_iota(jnp.int32, (C, C), 1)
    causal = ci <= ri
    strict = ci < ri
    eye = jnp.where(ri == ci, 1.0, 0.0).astype(F32)
    nw = nw_ref[...]

    def chunk_body(c, carry):
        r0 = pl.multiple_of(c * C, C)
        rows = pl.ds(r0, C)
        batches = range(nbatch)
        chains = [(b, h) for b in batches for h in range(GDN_V_HEADS)]
        kchains = [(b, kh) for b in batches for kh in range(GDN_K_HEADS)]

        def kidx(b, h):
            return b * GDN_K_HEADS + h // 2

        gc, gct, bc, eg, e_tail, e_last = [], [], [], [], [], []
        for b in batches:
            gc.append(g_s[b, rows, :])
            gct.append(gc[b].T)
            bc.append(b_s[b, rows, :])
            eg.append(jnp.exp(gc[b]))
            g_last = g_s[b, pl.ds(r0 + C - 1, 1), :]
            e_tail.append(jnp.exp(g_last - gc[b]))
            e_last.append(jnp.exp(g_last))
        qc, kc, kcb, kk, qk = [], [], [], [], []
        for b, kh in kchains:
            ksl = slice(kh * Dh, (kh + 1) * Dh)
            qc.append(qn[b, rows, ksl])
            kc.append(kn[b, rows, ksl])
            kcb.append(kc[-1].astype(BF16))
        for n in range(len(kchains)):
            kk.append(lax.dot_general(kcb[n], kcb[n], NT_DIMS, preferred_element_type=F32))
            qk.append(lax.dot_general(qc[n].astype(BF16), kcb[n], NT_DIMS, preferred_element_type=F32))
        decay, bcol, egc, inv, pw = [], [], [], [], []
        for b, h in chains:
            decay.append(jnp.exp(jnp.where(causal, gc[b][:, h:h + 1] - gct[b][h:h + 1, :], -1e30)))
            bcol.append(bc[b][:, h:h + 1])
            egc.append(eg[b][:, h:h + 1])
        for n, (b, h) in enumerate(chains):
            low = jnp.where(strict, kk[kidx(b, h)] * bcol[n] * decay[n], 0.0)
            lb = low.astype(BF16)
            inv.append(eye - low)
            pw.append(_dot(lb, lb))
        for it in range(5):
            for n in range(len(chains)):
                pwb = pw[n].astype(BF16)
                both = jnp.concatenate([inv[n], pw[n]], axis=0) if it < 4 else inv[n]
                prod = _dot(both.astype(BF16), pwb)
                inv[n] = inv[n] + prod[:C]
                if it < 4:
                    pw[n] = prod[C:]
        u, w = [], []
        for n, (b, h) in enumerate(chains):
            vsl = slice(h * Dh, (h + 1) * Dh)
            rhs = jnp.concatenate([vc[b, rows, vsl] * bcol[n], kc[kidx(b, h)] * (bcol[n] * egc[n])], axis=1)
            sol = _dot(inv[n].astype(BF16), rhs.astype(BF16))
            u.append(sol[:, :Dh])
            w.append(sol[:, Dh:].astype(BF16))
        s_old, sb, vnb = [], [], []
        for n, (b, h) in enumerate(chains):
            s_old.append(state[b, h])
            sb.append(s_old[n].astype(BF16))
            vnb.append((u[n] - _dot(w[n], sb[n])).astype(BF16))
        for n, (b, h) in enumerate(chains):
            vsl = slice(h * Dh, (h + 1) * Dh)
            attn = (qk[kidx(b, h)] * decay[n]).astype(BF16)
            qd = (qc[kidx(b, h)] * egc[n]).astype(BF16)
            kt = (kc[kidx(b, h)] * e_tail[b][:, h:h + 1]).astype(BF16)
            o = _dot(qd, sb[n]) + _dot(attn, vnb[n])
            state[b, h] = s_old[n] * e_last[b][:, h:h + 1] + lax.dot_general(
                kt, vnb[n], TN_DIMS, preferred_element_type=F32)
            zc = z_ref[b, rows, vsl].astype(F32)
            ms = jnp.mean(o * o, axis=-1, keepdims=True)
            yb = o * lax.rsqrt(ms + RMS_EPS) * nw * _silu(zc)
            y_ref[b, rows, SCONV_WIDTH + h * Dh:SCONV_WIDTH + (h + 1) * Dh] = yb.astype(y_ref.dtype)
        return carry

    lax.fori_loop(0, ts // C, chunk_body, 0)


def _mixer(proj, ab, sconv_w, conv_w, gate_p, norm_w, batch, seq, ts=256):
    def rows(cb, width):
        return pl.BlockSpec((batch, ts, width), lambda s, cb=cb: (0, s, cb))

    def whole(a):
        return pl.BlockSpec(a.shape, lambda s: (0,) * a.ndim)

    width_out = SCONV_WIDTH + GDN_VALUE_DIM
    return pl.pallas_call(
        _mixer_kernel,
        grid=(seq // ts,),
        in_specs=[
            rows(COL_BG // SCONV_WIDTH, SCONV_WIDTH),
            rows(COL_CG // SCONV_WIDTH, SCONV_WIDTH),
            rows(COL_XA // SCONV_WIDTH, SCONV_WIDTH),
            rows(COL_Q // GDN_KEY_DIM, GDN_KEY_DIM),
            rows(COL_K // GDN_KEY_DIM, GDN_KEY_DIM),
            rows(COL_V // GDN_VALUE_DIM, GDN_VALUE_DIM),
            rows(COL_Z // GDN_VALUE_DIM, GDN_VALUE_DIM),
            rows(0, LANES),
            whole(sconv_w), whole(conv_w), whole(gate_p), whole(norm_w),
        ],
        out_specs=pl.BlockSpec((batch, ts, width_out), lambda s: (0, s, 0)),
        out_shape=jax.ShapeDtypeStruct((batch, seq, width_out), BF16),
        scratch_shapes=[
            pltpu.VMEM((batch, ts + HALO, SCONV_WIDTH), F32),
            pltpu.VMEM((batch, ts + HALO, GDN_KEY_DIM), F32),
            pltpu.VMEM((batch, ts + HALO, GDN_KEY_DIM), F32),
            pltpu.VMEM((batch, ts + HALO, GDN_VALUE_DIM), F32),
            pltpu.VMEM((batch, ts, GDN_KEY_DIM), F32),
            pltpu.VMEM((batch, ts, GDN_KEY_DIM), F32),
            pltpu.VMEM((batch, ts, GDN_VALUE_DIM), F32),
            pltpu.VMEM((batch, ts, LANES), F32),
            pltpu.VMEM((batch, ts, LANES), F32),
            pltpu.VMEM((batch, GDN_V_HEADS, GDN_HEAD_DIM, GDN_HEAD_DIM), F32),
        ],
        compiler_params=_params(("arbitrary",)),
        name="mixer",
    )(proj, proj, proj, proj, proj, proj, proj, ab, sconv_w, conv_w, gate_p, norm_w)


def _outproj_kernel(alpha, y_ref, x_ref, g0_ref, b0_ref, w_ref, g1_ref, b1_ref, o_ref):
    h0 = _layer_norm(x_ref[...], g0_ref[...], b0_ref[...])
    r = alpha * h0 + _dot(y_ref[...], w_ref[...])
    o_ref[...] = _layer_norm(r, g1_ref[...], b1_ref[...]).astype(o_ref.dtype)


def _outproj(y, x2d, g0, b0, w, g1, b1, alpha, tm=512):
    T, D = x2d.shape
    vec = pl.BlockSpec((1, D), lambda i: (0, 0))
    return pl.pallas_call(
        functools.partial(_outproj_kernel, alpha),
        grid=(T // tm,),
        in_specs=[
            pl.BlockSpec((tm, y.shape[1]), lambda i: (i, 0)),
            pl.BlockSpec((tm, D), lambda i: (i, 0)),
            vec, vec,
            pl.BlockSpec(w.shape, lambda i: (0, 0)),
            vec, vec,
        ],
        out_specs=pl.BlockSpec((tm, D), lambda i: (i, 0)),
        out_shape=jax.ShapeDtypeStruct((T, D), BF16),
        compiler_params=_params(("parallel",)),
        name="outproj_ln1",
    )(y, x2d, g0, b0, w, g1, b1)


def _matmul_kernel(x_ref, w_ref, o_ref):
    o_ref[...] = _dot(x_ref[...].astype(BF16), w_ref[...]).astype(o_ref.dtype)


def _matmul(x, w, tn=1024):
    M, K = x.shape
    N = w.shape[1]
    return pl.pallas_call(
        _matmul_kernel,
        grid=(N // tn,),
        in_specs=[pl.BlockSpec((M, K), lambda j: (0, 0)), pl.BlockSpec((K, tn), lambda j: (0, j))],
        out_specs=pl.BlockSpec((M, tn), lambda j: (0, j)),
        out_shape=jax.ShapeDtypeStruct((M, N), BF16),
        compiler_params=_params(("parallel",)),
        name="kvproj",
    )(x, w)


def _xattn_kernel(alpha, h_ref, wq_ref, k_ref, v_ref, wo_ref, g_ref, b_ref, wr_hi_ref, wr_lo_ref,
                  o_ref, lg_ref, acc_ref):
    D = h_ref.shape[1]
    hd = D // XA_HEADS
    h1 = h_ref[...]
    acc_ref[...] = alpha * h1.astype(F32)
    for hh in range(XA_HEADS):
        sl = slice(hh * hd, (hh + 1) * hd)
        qh = _dot(h1, wq_ref[:, sl]).astype(BF16)
        s = lax.dot_general(qh, k_ref[:, sl], NT_DIMS, preferred_element_type=F32) * (hd ** -0.5)
        p = jnp.exp(s - jnp.max(s, axis=-1, keepdims=True))
        p = p / jnp.sum(p, axis=-1, keepdims=True)
        oh = _dot(p.astype(BF16), v_ref[:, sl]).astype(BF16)
        acc_ref[...] += _dot(oh, wo_ref[sl, :])
    h2 = _layer_norm(acc_ref[...], g_ref[...], b_ref[...])
    o_ref[...] = h2
    hi, lo = _split_bf16(h2)
    lg_ref[...] = _dot(hi, wr_hi_ref[...]) + _dot(hi, wr_lo_ref[...]) + _dot(lo, wr_hi_ref[...])


def _xattn(h1, wq, kv, wo, g, b, wr_hi, wr_lo, alpha, seq, n_mem, tm=512):
    T, D = h1.shape
    spb = seq // tm
    vec = pl.BlockSpec((1, D), lambda i: (0, 0))
    full = pl.BlockSpec((D, D), lambda i: (0, 0))
    return pl.pallas_call(
        functools.partial(_xattn_kernel, alpha),
        grid=(T // tm,),
        in_specs=[
            pl.BlockSpec((tm, D), lambda i: (i, 0)),
            full,
            pl.BlockSpec((n_mem, D), lambda i: (i // spb, 0)),
            pl.BlockSpec((n_mem, D), lambda i: (i // spb, 1)),
            full,
            vec, vec,
            pl.BlockSpec((D, LANES), lambda i: (0, 0)),
            pl.BlockSpec((D, LANES), lambda i: (0, 0)),
        ],
        out_specs=[
            pl.BlockSpec((tm, D), lambda i: (i, 0)),
            pl.BlockSpec((tm, LANES), lambda i: (i, 0)),
        ],
        out_shape=[
            jax.ShapeDtypeStruct((T, D), F32),
            jax.ShapeDtypeStruct((T, LANES), F32),
        ],
        scratch_shapes=[pltpu.VMEM((tm, D), F32)],
        compiler_params=_params(("parallel",)),
        name="xattn_ln2",
    )(h1, wq, kv, kv, wo, g, b, wr_hi, wr_lo)


def _route_kernel(lg_ref, o_ref, cnt_ref, carry):
    @pl.when(pl.program_id(0) == 0)
    def _():
        carry[...] = jnp.zeros(carry.shape, F32)

    lg = lg_ref[...]
    lane = lax.broadcasted_iota(jnp.int32, lg.shape, 1)
    neg = -jnp.inf
    big = jnp.int32(LANES)
    gl = jnp.where(lane < N_GROUPS, lg, neg)
    gmax = jnp.max(gl, axis=-1, keepdims=True)
    g_idx = jnp.min(jnp.where(gl == gmax, lane, big), axis=-1, keepdims=True)
    g_prob = 1.0 / jnp.sum(jnp.exp(gl - gmax), axis=-1, keepdims=True)
    e_lane = lane - N_GROUPS
    in_group = (e_lane >= g_idx * EXPERTS_PER_GROUP) & (e_lane < (g_idx + 1) * EXPERTS_PER_GROUP)
    el = jnp.where(in_group, lg, neg)
    m1 = jnp.max(el, axis=-1, keepdims=True)
    i1 = jnp.min(jnp.where(el == m1, lane, big), axis=-1, keepdims=True)
    denom = jnp.sum(jnp.exp(el - m1), axis=-1, keepdims=True)
    el2 = jnp.where(lane == i1, neg, el)
    m2 = jnp.max(el2, axis=-1, keepdims=True)
    i2 = jnp.min(jnp.where(el2 == m2, lane, big), axis=-1, keepdims=True)
    p1 = 1.0 / denom
    p2 = jnp.exp(m2 - m1) / denom
    psum = p1 + p2
    gate1 = g_prob * p1 / psum
    gate2 = g_prob * p2 / psum
    tm = lg.shape[0]
    onehot = jnp.where((lane == i1) | (lane == i2), 1.0, 0.0)
    ri = lax.broadcasted_iota(jnp.int32, (tm, tm), 0)
    ci = lax.broadcasted_iota(jnp.int32, (tm, tm), 1)
    earlier = jnp.where(ci < ri, 1.0, 0.0).astype(BF16)
    before = _dot(earlier, onehot.astype(BF16)) + carry[...]
    rank1 = jnp.sum(jnp.where(lane == i1, before, 0.0), axis=-1, keepdims=True)
    rank2 = jnp.sum(jnp.where(lane == i2, before, 0.0), axis=-1, keepdims=True)
    carry[...] += jnp.sum(onehot, axis=0, keepdims=True)
    cnt_ref[...] = carry[...]
    cols = ((i1 - N_GROUPS).astype(F32), (i2 - N_GROUPS).astype(F32), gate1, gate2, rank1, rank2)
    out = jnp.zeros(lg.shape, F32)
    for n, val in enumerate(cols):
        out = jnp.where(lane == n, val, out)
    o_ref[...] = out


def _route(logits, tm=1024):
    T = logits.shape[0]
    spec = pl.BlockSpec((tm, LANES), lambda i: (i, 0))
    return pl.pallas_call(
        _route_kernel,
        grid=(T // tm,),
        in_specs=[spec],
        out_specs=[spec, pl.BlockSpec((1, LANES), lambda i: (0, 0))],
        out_shape=[jax.ShapeDtypeStruct((T, LANES), F32), jax.ShapeDtypeStruct((1, LANES), F32)],
        scratch_shapes=[pltpu.VMEM((1, LANES), F32)],
        compiler_params=_params(("arbitrary",)),
        name="route",
    )(logits)


def _wait_rows(buf, sem):
    pltpu.make_async_copy(buf, buf, sem).wait()


def _expert_kernel(bexp_ref, nval_ref, off_ref, order_ref, h_hbm, w1_ref, w3_ref, w2_ref,
                   y_ref, xbuf0, xbuf1, w1b, w3b, w2b, gsem):
    i = pl.program_id(0)
    xbuf = (xbuf0, xbuf1)
    bm = xbuf0.shape[0]
    slot = lax.rem(i, 2)
    nv = nval_ref[i]
    nv_prev = jnp.where(i > 0, nval_ref[jnp.maximum(i - 1, 0)], 0)

    def gather_row(base, r, dst):
        tok = lax.shift_right_logical(order_ref[base + r], 1)
        pltpu.make_async_copy(h_hbm.at[pl.ds(tok, 1), :], dst.at[pl.ds(r, 1), :], gsem).start()

    @pl.when(i == 0)
    def _():
        def body(r, c):
            gather_row(off_ref[0], r, xbuf0)
            return c
        lax.fori_loop(0, bm, body, 0)

    @pl.when(nv > 0)
    def _():
        _wait_rows(xbuf0, gsem)

        @pl.when((i == 0) | (bexp_ref[jnp.maximum(i - 1, 0)] != bexp_ref[i]))
        def _():
            w1b[...] = w1_ref[0].astype(BF16)
            w3b[...] = w3_ref[0].astype(BF16)
            w2b[...] = w2_ref[0].astype(BF16)

        next_base = off_ref[i + 1]

        def compute(cur, oth):
            for r in range(bm):
                gather_row(next_base, r, xbuf[oth])
            xb = xbuf[cur][...].astype(BF16)
            hid = _silu(_dot(xb, w1b[...])) * _dot(xb, w3b[...])
            y_ref[...] = _dot(hid.astype(BF16), w2b[...])

        @pl.when(slot == 0)
        def _():
            compute(0, 1)

        @pl.when(slot == 1)
        def _():
            compute(1, 0)

    @pl.when(nv == 0)
    def _():
        y_ref[...] = jnp.zeros(y_ref.shape, y_ref.dtype)

        @pl.when(nv_prev > 0)
        def _():
            _wait_rows(xbuf0, gsem)


def _experts(h2, w1, w3, w2, bexp, nval, off, order, bm):
    T, D = h2.shape
    nb = bexp.shape[0]
    ff = w1.shape[2]
    grid_spec = pltpu.PrefetchScalarGridSpec(
        num_scalar_prefetch=4,
        grid=(nb,),
        in_specs=[
            pl.BlockSpec(memory_space=pl.ANY),
            pl.BlockSpec((1, D, ff), lambda i, be, nv, of, od: (be[i], 0, 0)),
            pl.BlockSpec((1, D, ff), lambda i, be, nv, of, od: (be[i], 0, 0)),
            pl.BlockSpec((1, ff, D), lambda i, be, nv, of, od: (be[i], 0, 0)),
        ],
        out_specs=pl.BlockSpec((bm, D), lambda i, be, nv, of, od: (i, 0)),
        scratch_shapes=[
            pltpu.VMEM((bm, D), F32),
            pltpu.VMEM((bm, D), F32),
            pltpu.VMEM((D, ff), BF16),
            pltpu.VMEM((D, ff), BF16),
            pltpu.VMEM((ff, D), BF16),
            pltpu.SemaphoreType.DMA(()),
        ],
    )
    return pl.pallas_call(
        _expert_kernel,
        grid_spec=grid_spec,
        out_shape=jax.ShapeDtypeStruct((nb * bm, D), F32),
        compiler_params=_params(("arbitrary",)),
        name="experts",
    )(bexp, nval, off, order, h2, w1, w3, w2)


def _combine_kernel(alpha, pos_ref, h_ref, rt_ref, g_ref, b_ref, y_hbm, o_ref, ybuf0, ybuf1, gsem):
    i = pl.program_id(0)
    nt = pl.num_programs(0)
    K = EXPERT_TOP_K
    tm = h_ref.shape[0]
    ybuf = (ybuf0, ybuf1)
    slot = lax.rem(i, 2)

    def gather_row(tile, r, k, dst):
        src = pos_ref[(tile * tm + r) * K + k]
        pltpu.make_async_copy(y_hbm.at[pl.ds(src, 1), :], dst.at[k, pl.ds(r, 1), :], gsem).start()

    @pl.when(i == 0)
    def _():
        def body(r, c):
            for k in range(K):
                gather_row(0, r, k, ybuf0)
            return c
        lax.fori_loop(0, tm, body, 0)

    _wait_rows(ybuf0, gsem)
    next_tile = jnp.minimum(i + 1, nt - 1)
    rt = rt_ref[...]

    def compute(cur, oth):
        for r in range(tm):
            for k in range(K):
                gather_row(next_tile, r, k, ybuf[oth])
        ffn = ybuf[cur][0] * rt[:, 2:3] + ybuf[cur][1] * rt[:, 3:4]
        o_ref[...] = _layer_norm(alpha * h_ref[...] + ffn, g_ref[...], b_ref[...])

    @pl.when(slot == 0)
    def _():
        compute(0, 1)

    @pl.when(slot == 1)
    def _():
        compute(1, 0)

    @pl.when(i == nt - 1)
    def _():
        _wait_rows(ybuf0, gsem)


def _combine(h2, y_sorted, pos, route, g, b, alpha, tm=256):
    T, D = h2.shape
    vec = pl.BlockSpec((1, D), lambda i, ps: (0, 0))
    grid_spec = pltpu.PrefetchScalarGridSpec(
        num_scalar_prefetch=1,
        grid=(T // tm,),
        in_specs=[
            pl.BlockSpec((tm, D), lambda i, ps: (i, 0)),
            pl.BlockSpec((tm, LANES), lambda i, ps: (i, 0)),
            vec, vec,
            pl.BlockSpec(memory_space=pl.ANY),
        ],
        out_specs=pl.BlockSpec((tm, D), lambda i, ps: (i, 0)),
        scratch_shapes=[
            pltpu.VMEM((EXPERT_TOP_K, tm, D), F32),
            pltpu.VMEM((EXPERT_TOP_K, tm, D), F32),
            pltpu.SemaphoreType.DMA(()),
        ],
    )
    return pl.pallas_call(
        functools.partial(_combine_kernel, alpha),
        grid_spec=grid_spec,
        out_shape=jax.ShapeDtypeStruct((T, D), F32),
        compiler_params=_params(("arbitrary",)),
        name="combine_ln3",
    )(pos, h2, route, g, b, y_sorted)


def _routing_tables(route, lane_counts, n_tok, bm):
    K = EXPERT_TOP_K
    TK = n_tok * K
    flat_e = route[:, 0:K].astype(jnp.int32).reshape(-1)
    flat_rank = route[:, 2 * K:3 * K].astype(jnp.int32).reshape(-1)
    order = jnp.argsort(flat_e).astype(jnp.int32)
    experts = jnp.arange(N_EXPERTS, dtype=jnp.int32)
    counts = lane_counts[0, N_GROUPS:N_GROUPS + N_EXPERTS].astype(jnp.int32)
    start = jnp.cumsum(counts) - counts
    padded = (counts + bm - 1) // bm * bm
    padded_end = jnp.cumsum(padded)
    padded_start = padded_end - padded
    nb = TK // bm + N_EXPERTS
    block_start = jnp.arange(nb, dtype=jnp.int32) * bm
    bexp = jnp.minimum(jnp.sum((block_start[:, None] >= padded_end[None, :]).astype(jnp.int32), axis=1),
                       N_EXPERTS - 1)
    of_block = bexp[:, None] == experts[None, :]

    def pick(per_expert):
        return jnp.sum(jnp.where(of_block, per_expert[None, :], 0), axis=1)

    rank0 = block_start - pick(padded_start)
    nval = jnp.clip(pick(counts) - rank0, 0, bm).astype(jnp.int32)
    off = jnp.clip(pick(start) + rank0, 0, TK).astype(jnp.int32)
    order = jnp.concatenate([order, jnp.zeros((bm,), jnp.int32)])
    pos = flat_rank + jnp.sum(jnp.where(flat_e[:, None] == experts[None, :], padded_start[None, :], 0), axis=1)
    return bexp.astype(jnp.int32), nval, off, order, pos.astype(jnp.int32)


def kernel(x, mem, ln0_g, ln0_b, w_in, sconv_w, gdn_conv_w, gdn_a_log, gdn_dt_bias, gdn_norm_w,
           w_mix_out, ln1_g, ln1_b, xa_wq, xa_wk, xa_wv, xa_wo, ln2_g, ln2_b, w_group,
           w_expert_router, w1, w3, w2, ln3_g, ln3_b):
    B, S, D = x.shape
    T = B * S
    depth = w_in.shape[0]
    assert depth == 1, "single-layer stack: the out-projection kernel recomputes LayerNorm0(x)"
    alpha = float((2 * depth) ** 0.25)
    n_mem = mem.shape[1]
    bm = 256

    def vec(a):
        return a.reshape(1, -1).astype(F32)

    def pad_lanes(a):
        return jnp.pad(a, ((0, 0), (0, LANES - a.shape[1])))

    x2d = x.reshape(T, D)
    g0, b0 = vec(ln0_g), vec(ln0_b)
    l = 0
    w_main = w_in[l][:, :COL_AB].astype(BF16)
    wab_hi, wab_lo = _split_bf16(pad_lanes(w_in[l][:, COL_AB:]))
    proj, ab = _ln_inproj(x2d, g0, b0, w_main, wab_hi, wab_lo)
    gate_p = jnp.pad(jnp.stack([gdn_a_log[l], gdn_dt_bias[l]]).astype(F32),
                     ((0, HALO - 2), (0, LANES - GDN_V_HEADS)))
    y = _mixer(proj.reshape(B, S, -1), ab.reshape(B, S, LANES), sconv_w[l].astype(F32),
               gdn_conv_w[l].astype(F32), gate_p, vec(gdn_norm_w[l]), B, S)
    h1 = _outproj(y.reshape(T, -1), x2d, g0, b0, w_mix_out[l].astype(BF16), vec(ln1_g[l]), vec(ln1_b[l]), alpha)
    kv = _matmul(mem.reshape(B * n_mem, D),
                 jnp.concatenate([xa_wk[l], xa_wv[l]], axis=1).astype(BF16))
    wr_hi, wr_lo = _split_bf16(pad_lanes(jnp.concatenate([w_group[l], w_expert_router[l]], axis=1)))
    h2, logits = _xattn(h1, xa_wq[l].astype(BF16), kv, xa_wo[l].astype(BF16), vec(ln2_g[l]),
                        vec(ln2_b[l]), wr_hi, wr_lo, alpha, S, n_mem)
    route, lane_counts = _route(logits)
    bexp, nval, off, order, pos = _routing_tables(route, lane_counts, T, bm)
    y_sorted = _experts(h2, w1[l], w3[l], w2[l], bexp, nval, off, order, bm)
    out = _combine(h2, y_sorted, pos, route, vec(ln3_g[l]), vec(ln3_b[l]), alpha)
    return out.reshape(B, S, D)
```

```python
import functools

import jax
import jax.numpy as jnp
from jax import lax
from jax.experimental import pallas as pl
from jax.experimental.pallas import tpu as pltpu

F32 = jnp.float32
BF16 = jnp.bfloat16

LN_EPS = 1e-5
RMS_EPS = 1e-6
L2_EPS = 1e-6

SCONV_WIDTH = 1024
SCONV_K = 3
GDN_HEAD_DIM = 128
GDN_V_HEADS = 8
GDN_K_HEADS = 4
GDN_KEY_DIM = GDN_K_HEADS * GDN_HEAD_DIM
GDN_VALUE_DIM = GDN_V_HEADS * GDN_HEAD_DIM
GDN_CONV_K = 4
GDN_CHUNK = 64
XA_HEADS = 4
N_GROUPS = 8
EXPERTS_PER_GROUP = 8
N_EXPERTS = N_GROUPS * EXPERTS_PER_GROUP
EXPERT_TOP_K = 2

LANES = 128
HALO = 8
VMEM_LIMIT = 56 * 1024 * 1024

COL_BG = 0
COL_CG = SCONV_WIDTH
COL_XA = 2 * SCONV_WIDTH
COL_Q = 3 * SCONV_WIDTH
COL_K = COL_Q + GDN_KEY_DIM
COL_V = COL_K + GDN_KEY_DIM
COL_Z = COL_V + GDN_VALUE_DIM
COL_AB = COL_Z + GDN_VALUE_DIM

NT_DIMS = (((1,), (1,)), ((), ()))
TN_DIMS = (((0,), (0,)), ((), ()))


def _dot(a, b):
    return jnp.dot(a, b, preferred_element_type=F32)


def _split_bf16(x):
    hi = x.astype(BF16)
    lo = (x - hi.astype(F32)).astype(BF16)
    return hi, lo


def _dot_split(x_hi, x_lo, w_hi, w_lo):
    n = w_hi.shape[1]
    both = _dot(x_hi, jnp.concatenate([w_hi, w_lo], axis=1))
    return both[:, :n] + both[:, n:] + _dot(x_lo, w_hi)


def _layer_norm(x, g, b):
    mu = jnp.mean(x, axis=-1, keepdims=True)
    xc = x - mu
    var = jnp.mean(xc * xc, axis=-1, keepdims=True)
    return xc * lax.rsqrt(var + LN_EPS) * g + b


def _sigmoid(x):
    return 1.0 / (1.0 + jnp.exp(-x))


def _silu(x):
    return x * _sigmoid(x)


def _params(sem):
    return pltpu.CompilerParams(dimension_semantics=sem, vmem_limit_bytes=VMEM_LIMIT)


def _ln_inproj_kernel(x_ref, g_ref, b_ref, w_ref, wab_hi_ref, wab_lo_ref, o_ref, ab_ref, hn_ref):
    @pl.when(pl.program_id(1) == 0)
    def _():
        h = _layer_norm(x_ref[...], g_ref[...], b_ref[...])
        hi, lo = _split_bf16(h)
        hn_ref[...] = hi
        ab_ref[...] = _dot_split(hi, lo, wab_hi_ref[...], wab_lo_ref[...])

    o_ref[...] = _dot(hn_ref[...], w_ref[...]).astype(o_ref.dtype)


def _ln_inproj(x2d, g, b, w_main, wab_hi, wab_lo, tm=1024, tn=1024):
    T, D = x2d.shape
    N = w_main.shape[1]
    return pl.pallas_call(
        _ln_inproj_kernel,
        grid=(T // tm, N // tn),
        in_specs=[
            pl.BlockSpec((tm, D), lambda i, j: (i, 0)),
            pl.BlockSpec((1, D), lambda i, j: (0, 0)),
            pl.BlockSpec((1, D), lambda i, j: (0, 0)),
            pl.BlockSpec((D, tn), lambda i, j: (0, j)),
            pl.BlockSpec((D, LANES), lambda i, j: (0, 0)),
            pl.BlockSpec((D, LANES), lambda i, j: (0, 0)),
        ],
        out_specs=[
            pl.BlockSpec((tm, tn), lambda i, j: (i, j)),
            pl.BlockSpec((tm, LANES), lambda i, j: (i, 0)),
        ],
        out_shape=[
            jax.ShapeDtypeStruct((T, N), BF16),
            jax.ShapeDtypeStruct((T, LANES), F32),
        ],
        scratch_shapes=[pltpu.VMEM((tm, D), BF16)],
        compiler_params=_params(("parallel", "arbitrary")),
        name="ln_inproj",
    )(x2d, g, b, w_main, wab_hi, wab_lo)


CONV_ROWS = 64


def _conv_taps(buf, b, lanes, w_ref, w_col0, ksize, r0):
    w_lanes = slice(w_col0 + lanes.start, w_col0 + lanes.stop)
    acc = None
    for j in range(ksize):
        start = HALO - ksize + 1 + j + r0
        term = w_ref[j:j + 1, w_lanes] * buf[b, start:start + CONV_ROWS, lanes]
        acc = term if acc is None else acc + term
    return acc


def _mixer_kernel(bg_ref, cg_ref, xa_ref, q_ref, k_ref, v_ref, z_ref, ab_ref,
                  sw_ref, cw_ref, gp_ref, nw_ref,
                  y_ref,
                  sbuf, qbuf, kbuf, vbuf, qn, kn, vc, g_s, b_s, state):
    nbatch, ts = q_ref.shape[0], q_ref.shape[1]
    C = GDN_CHUNK
    Dh = GDN_HEAD_DIM
    stage_bufs = (sbuf, qbuf, kbuf, vbuf)

    @pl.when(pl.program_id(0) == 0)
    def _():
        for buf in stage_bufs:
            buf[:, 0:HALO, :] = jnp.zeros((nbatch, HALO, buf.shape[2]), F32)
        state[...] = jnp.zeros(state.shape, F32)

    row = lax.broadcasted_iota(jnp.int32, (ts, ts), 0)
    col = lax.broadcasted_iota(jnp.int32, (ts, ts), 1)
    tri = jnp.where((row // C == col // C) & (col <= row), 1.0, 0.0).astype(BF16)

    for b in range(nbatch):
        ab = ab_ref[b]
        xg = ab + gp_ref[1:2, :]
        softplus = jnp.maximum(xg, 0.0) + jnp.log1p(jnp.exp(-jnp.abs(xg)))
        g = -jnp.exp(gp_ref[0:1, :]) * softplus
        b_s[b] = _sigmoid(pltpu.roll(ab, LANES - GDN_V_HEADS, axis=1))
        g1 = g.astype(BF16)
        r1 = g - g1.astype(F32)
        g2 = r1.astype(BF16)
        g3 = (r1 - g2.astype(F32)).astype(BF16)
        g_s[b] = _dot(tri, g1) + _dot(tri, g2) + _dot(tri, g3)

    def conv_block(r0):
        for b in range(nbatch):
            rows = slice(r0, r0 + CONV_ROWS)
            srows = slice(HALO + r0, HALO + r0 + CONV_ROWS)
            for cb in range(SCONV_WIDTH // LANES):
                ls = slice(cb * LANES, (cb + 1) * LANES)
                sbuf[b, srows, ls] = cg_ref[b, rows, ls].astype(F32) * xa_ref[b, rows, ls].astype(F32)
                conv = _conv_taps(sbuf, b, ls, sw_ref, 0, SCONV_K, r0)
                y_ref[b, rows, ls] = (bg_ref[b, rows, ls].astype(F32) * conv).astype(y_ref.dtype)
            for kh in range(GDN_K_HEADS):
                ls = slice(kh * Dh, (kh + 1) * Dh)
                qbuf[b, srows, ls] = q_ref[b, rows, ls].astype(F32)
                kbuf[b, srows, ls] = k_ref[b, rows, ls].astype(F32)
                qa = _silu(_conv_taps(qbuf, b, ls, cw_ref, 0, GDN_CONV_K, r0))
                ka = _silu(_conv_taps(kbuf, b, ls, cw_ref, GDN_KEY_DIM, GDN_CONV_K, r0))
                qn[b, rows, ls] = qa * (lax.rsqrt(jnp.sum(qa * qa, axis=-1, keepdims=True) + L2_EPS) * (Dh ** -0.5))
                kn[b, rows, ls] = ka * lax.rsqrt(jnp.sum(ka * ka, axis=-1, keepdims=True) + L2_EPS)
            for h in range(GDN_V_HEADS):
                ls = slice(h * Dh, (h + 1) * Dh)
                vbuf[b, srows, ls] = v_ref[b, rows, ls].astype(F32)
                vc[b, rows, ls] = _silu(_conv_taps(vbuf, b, ls, cw_ref, 2 * GDN_KEY_DIM, GDN_CONV_K, r0))
    ri = lax.broadcasted_iota(jnp.int32, (C, C), 0)
    ci = lax.broadcasted_iota(jnp.int32, (C, C), 1)
    causal = ci <= ri
    strict = ci < ri
    eye = jnp.where(ri == ci, 1.0, 0.0).astype(F32)
    nw = nw_ref[...]

    def chunk_body(c, carry):
        r0 = pl.multiple_of(c * C, C)
        rows = pl.ds(r0, C)
        batches = range(nbatch)
        chains = [(b, h) for b in batches for h in range(GDN_V_HEADS)]
        kchains = [(b, kh) for b in batches for kh in range(GDN_K_HEADS)]

        def kidx(b, h):
            return b * GDN_K_HEADS + h // 2

        gc, gct, bc, eg, e_tail, e_last = [], [], [], [], [], []
        for b in batches:
            gc.append(g_s[b, rows, :])
            gct.append(gc[b].T)
            bc.append(b_s[b, rows, :])
            eg.append(jnp.exp(gc[b]))
            g_last = g_s[b, pl.ds(r0 + C - 1, 1), :]
            e_tail.append(jnp.exp(g_last - gc[b]))
            e_last.append(jnp.exp(g_last))
        qc, kc, kcb, kk, qk = [], [], [], [], []
        for b, kh in kchains:
            ksl = slice(kh * Dh, (kh + 1) * Dh)
            qc.append(qn[b, rows, ksl])
            kc.append(kn[b, rows, ksl])
            kcb.append(kc[-1].astype(BF16))
        for n in range(len(kchains)):
            kk.append(lax.dot_general(kcb[n], kcb[n], NT_DIMS, preferred_element_type=F32))
            qk.append(lax.dot_general(qc[n].astype(BF16), kcb[n], NT_DIMS, preferred_element_type=F32))
        decay, bcol, egc, inv, pw = [], [], [], [], []
        for b, h in chains:
            decay.append(jnp.exp(jnp.where(causal, gc[b][:, h:h + 1] - gct[b][h:h + 1, :], -1e30)))
            bcol.append(bc[b][:, h:h + 1])
            egc.append(eg[b][:, h:h + 1])
        for n, (b, h) in enumerate(chains):
            low = jnp.where(strict, kk[kidx(b, h)] * bcol[n] * decay[n], 0.0)
            lb = low.astype(BF16)
            inv.append(eye - low)
            pw.append(_dot(lb, lb))
        for it in range(5):
            for n in range(len(chains)):
                pwb = pw[n].astype(BF16)
                both = jnp.concatenate([inv[n], pw[n]], axis=0) if it < 4 else inv[n]
                prod = _dot(both.astype(BF16), pwb)
                inv[n] = inv[n] + prod[:C]
                if it < 4:
                    pw[n] = prod[C:]
        u, w = [], []
        for n, (b, h) in enumerate(chains):
            vsl = slice(h * Dh, (h + 1) * Dh)
            rhs = jnp.concatenate([vc[b, rows, vsl] * bcol[n], kc[kidx(b, h)] * (bcol[n] * egc[n])], axis=1)
            sol = _dot(inv[n].astype(BF16), rhs.astype(BF16))
            u.append(sol[:, :Dh])
            w.append(sol[:, Dh:].astype(BF16))
        s_old, sb, vnb = [], [], []
        for n, (b, h) in enumerate(chains):
            s_old.append(state[b, h])
            sb.append(s_old[n].astype(BF16))
            vnb.append((u[n] - _dot(w[n], sb[n])).astype(BF16))
        for n, (b, h) in enumerate(chains):
            vsl = slice(h * Dh, (h + 1) * Dh)
            attn = (qk[kidx(b, h)] * decay[n]).astype(BF16)
            qd = (qc[kidx(b, h)] * egc[n]).astype(BF16)
            kt = (kc[kidx(b, h)] * e_tail[b][:, h:h + 1]).astype(BF16)
            o = _dot(qd, sb[n]) + _dot(attn, vnb[n])
            state[b, h] = s_old[n] * e_last[b][:, h:h + 1] + lax.dot_general(
                kt, vnb[n], TN_DIMS, preferred_element_type=F32)
            zc = z_ref[b, rows, vsl].astype(F32)
            ms = jnp.mean(o * o, axis=-1, keepdims=True)
            yb = o * lax.rsqrt(ms + RMS_EPS) * nw * _silu(zc)
            y_ref[b, rows, SCONV_WIDTH + h * Dh:SCONV_WIDTH + (h + 1) * Dh] = yb.astype(y_ref.dtype)

        return carry

    for r0 in range(0, ts, CONV_ROWS):
        conv_block(r0)
    for buf in stage_bufs:
        buf[:, 0:HALO, :] = buf[:, ts:ts + HALO, :]
    lax.fori_loop(0, ts // C, chunk_body, 0)


def _mixer(proj, ab, sconv_w, conv_w, gate_p, norm_w, batch, seq, ts=256):
    def rows(cb, width):
        return pl.BlockSpec((batch, ts, width), lambda s, cb=cb: (0, s, cb))

    def whole(a):
        return pl.BlockSpec(a.shape, lambda s: (0,) * a.ndim)

    width_out = SCONV_WIDTH + GDN_VALUE_DIM
    return pl.pallas_call(
        _mixer_kernel,
        grid=(seq // ts,),
        in_specs=[
            rows(COL_BG // SCONV_WIDTH, SCONV_WIDTH),
            rows(COL_CG // SCONV_WIDTH, SCONV_WIDTH),
            rows(COL_XA // SCONV_WIDTH, SCONV_WIDTH),
            rows(COL_Q // GDN_KEY_DIM, GDN_KEY_DIM),
            rows(COL_K // GDN_KEY_DIM, GDN_KEY_DIM),
            rows(COL_V // GDN_VALUE_DIM, GDN_VALUE_DIM),
            rows(COL_Z // GDN_VALUE_DIM, GDN_VALUE_DIM),
            rows(0, LANES),
            whole(sconv_w), whole(conv_w), whole(gate_p), whole(norm_w),
        ],
        out_specs=pl.BlockSpec((batch, ts, width_out), lambda s: (0, s, 0)),
        out_shape=jax.ShapeDtypeStruct((batch, seq, width_out), BF16),
        scratch_shapes=[
            pltpu.VMEM((batch, ts + HALO, SCONV_WIDTH), F32),
            pltpu.VMEM((batch, ts + HALO, GDN_KEY_DIM), F32),
            pltpu.VMEM((batch, ts + HALO, GDN_KEY_DIM), F32),
            pltpu.VMEM((batch, ts + HALO, GDN_VALUE_DIM), F32),
            pltpu.VMEM((batch, ts, GDN_KEY_DIM), F32),
            pltpu.VMEM((batch, ts, GDN_KEY_DIM), F32),
            pltpu.VMEM((batch, ts, GDN_VALUE_DIM), F32),
            pltpu.VMEM((batch, ts, LANES), F32),
            pltpu.VMEM((batch, ts, LANES), F32),
            pltpu.VMEM((batch, GDN_V_HEADS, GDN_HEAD_DIM, GDN_HEAD_DIM), F32),
        ],
        compiler_params=_params(("arbitrary",)),
        name="mixer",
    )(proj, proj, proj, proj, proj, proj, proj, ab, sconv_w, conv_w, gate_p, norm_w)


def _outproj_kernel(alpha, y_ref, x_ref, g0_ref, b0_ref, w_ref, g1_ref, b1_ref, o_ref):
    h0 = _layer_norm(x_ref[...], g0_ref[...], b0_ref[...])
    r = alpha * h0 + _dot(y_ref[...], w_ref[...])
    o_ref[...] = _layer_norm(r, g1_ref[...], b1_ref[...]).astype(o_ref.dtype)


def _outproj(y, x2d, g0, b0, w, g1, b1, alpha, tm=512):
    T, D = x2d.shape
    vec = pl.BlockSpec((1, D), lambda i: (0, 0))
    return pl.pallas_call(
        functools.partial(_outproj_kernel, alpha),
        grid=(T // tm,),
        in_specs=[
            pl.BlockSpec((tm, y.shape[1]), lambda i: (i, 0)),
            pl.BlockSpec((tm, D), lambda i: (i, 0)),
            vec, vec,
            pl.BlockSpec(w.shape, lambda i: (0, 0)),
            vec, vec,
        ],
        out_specs=pl.BlockSpec((tm, D), lambda i: (i, 0)),
        out_shape=jax.ShapeDtypeStruct((T, D), BF16),
        compiler_params=_params(("parallel",)),
        name="outproj_ln1",
    )(y, x2d, g0, b0, w, g1, b1)


def _matmul_kernel(x_ref, w_ref, o_ref):
    o_ref[...] = _dot(x_ref[...].astype(BF16), w_ref[...]).astype(o_ref.dtype)


def _matmul(x, w, tn=1024):
    M, K = x.shape
    N = w.shape[1]
    return pl.pallas_call(
        _matmul_kernel,
        grid=(N // tn,),
        in_specs=[pl.BlockSpec((M, K), lambda j: (0, 0)), pl.BlockSpec((K, tn), lambda j: (0, j))],
        out_specs=pl.BlockSpec((M, tn), lambda j: (0, j)),
        out_shape=jax.ShapeDtypeStruct((M, N), BF16),
        compiler_params=_params(("parallel",)),
        name="kvproj",
    )(x, w)


def _xattn_kernel(alpha, h_ref, wq_ref, k_ref, v_ref, wo_ref, g_ref, b_ref, wr_hi_ref, wr_lo_ref,
                  o_ref, lg_ref, acc_ref):
    D = h_ref.shape[1]
    hd = D // XA_HEADS
    h1 = h_ref[...]
    acc_ref[...] = alpha * h1.astype(F32)
    for hh in range(XA_HEADS):
        sl = slice(hh * hd, (hh + 1) * hd)
        qh = _dot(h1, wq_ref[:, sl]).astype(BF16)
        s = lax.dot_general(qh, k_ref[:, sl], NT_DIMS, preferred_element_type=F32) * (hd ** -0.5)
        p = jnp.exp(s - jnp.max(s, axis=-1, keepdims=True))
        p = p / jnp.sum(p, axis=-1, keepdims=True)
        oh = _dot(p.astype(BF16), v_ref[:, sl]).astype(BF16)
        acc_ref[...] += _dot(oh, wo_ref[sl, :])
    h2 = _layer_norm(acc_ref[...], g_ref[...], b_ref[...])
    o_ref[...] = h2
    hi, lo = _split_bf16(h2)
    lg_ref[...] = _dot_split(hi, lo, wr_hi_ref[...], wr_lo_ref[...])


def _xattn(h1, wq, kv, wo, g, b, wr_hi, wr_lo, alpha, seq, n_mem, tm=512):
    T, D = h1.shape
    spb = seq // tm
    vec = pl.BlockSpec((1, D), lambda i: (0, 0))
    full = pl.BlockSpec((D, D), lambda i: (0, 0))
    return pl.pallas_call(
        functools.partial(_xattn_kernel, alpha),
        grid=(T // tm,),
        in_specs=[
            pl.BlockSpec((tm, D), lambda i: (i, 0)),
            full,
            pl.BlockSpec((n_mem, D), lambda i: (i // spb, 0)),
            pl.BlockSpec((n_mem, D), lambda i: (i // spb, 1)),
            full,
            vec, vec,
            pl.BlockSpec((D, LANES), lambda i: (0, 0)),
            pl.BlockSpec((D, LANES), lambda i: (0, 0)),
        ],
        out_specs=[
            pl.BlockSpec((tm, D), lambda i: (i, 0)),
            pl.BlockSpec((tm, LANES), lambda i: (i, 0)),
        ],
        out_shape=[
            jax.ShapeDtypeStruct((T, D), F32),
            jax.ShapeDtypeStruct((T, LANES), F32),
        ],
        scratch_shapes=[pltpu.VMEM((tm, D), F32)],
        compiler_params=_params(("parallel",)),
        name="xattn_ln2",
    )(h1, wq, kv, kv, wo, g, b, wr_hi, wr_lo)


def _route_kernel(lg_ref, o_ref, cnt_ref, carry):
    @pl.when(pl.program_id(0) == 0)
    def _():
        carry[...] = jnp.zeros(carry.shape, F32)

    lg = lg_ref[...]
    lane = lax.broadcasted_iota(jnp.int32, lg.shape, 1)
    neg = -jnp.inf
    big = jnp.int32(LANES)
    gl = jnp.where(lane < N_GROUPS, lg, neg)
    gmax = jnp.max(gl, axis=-1, keepdims=True)
    g_idx = jnp.min(jnp.where(gl == gmax, lane, big), axis=-1, keepdims=True)
    g_prob = 1.0 / jnp.sum(jnp.exp(gl - gmax), axis=-1, keepdims=True)
    e_lane = lane - N_GROUPS
    in_group = (e_lane >= g_idx * EXPERTS_PER_GROUP) & (e_lane < (g_idx + 1) * EXPERTS_PER_GROUP)
    el = jnp.where(in_group, lg, neg)
    m1 = jnp.max(el, axis=-1, keepdims=True)
    i1 = jnp.min(jnp.where(el == m1, lane, big), axis=-1, keepdims=True)
    denom = jnp.sum(jnp.exp(el - m1), axis=-1, keepdims=True)
    el2 = jnp.where(lane == i1, neg, el)
    m2 = jnp.max(el2, axis=-1, keepdims=True)
    i2 = jnp.min(jnp.where(el2 == m2, lane, big), axis=-1, keepdims=True)
    p1 = 1.0 / denom
    p2 = jnp.exp(m2 - m1) / denom
    psum = p1 + p2
    gate1 = g_prob * p1 / psum
    gate2 = g_prob * p2 / psum
    tm = lg.shape[0]
    onehot = jnp.where((lane == i1) | (lane == i2), 1.0, 0.0)
    ri = lax.broadcasted_iota(jnp.int32, (tm, tm), 0)
    ci = lax.broadcasted_iota(jnp.int32, (tm, tm), 1)
    earlier = jnp.where(ci < ri, 1.0, 0.0).astype(BF16)
    before = _dot(earlier, onehot.astype(BF16)) + carry[...]
    rank1 = jnp.sum(jnp.where(lane == i1, before, 0.0), axis=-1, keepdims=True)
    rank2 = jnp.sum(jnp.where(lane == i2, before, 0.0), axis=-1, keepdims=True)
    carry[...] += jnp.sum(onehot, axis=0, keepdims=True)
    cnt_ref[...] = carry[...]
    cols = ((i1 - N_GROUPS).astype(F32), (i2 - N_GROUPS).astype(F32), gate1, gate2, rank1, rank2)
    out = jnp.zeros(lg.shape, F32)
    for n, val in enumerate(cols):
        out = jnp.where(lane == n, val, out)
    o_ref[...] = out


def _route(logits, tm=512):
    T = logits.shape[0]
    spec = pl.BlockSpec((tm, LANES), lambda i: (i, 0))
    return pl.pallas_call(
        _route_kernel,
        grid=(T // tm,),
        in_specs=[spec],
        out_specs=[spec, pl.BlockSpec((1, LANES), lambda i: (0, 0))],
        out_shape=[jax.ShapeDtypeStruct((T, LANES), F32), jax.ShapeDtypeStruct((1, LANES), F32)],
        scratch_shapes=[pltpu.VMEM((1, LANES), F32)],
        compiler_params=_params(("arbitrary",)),
        name="route",
    )(logits)


def _wait_rows(buf, sem):
    pltpu.make_async_copy(buf, buf, sem).wait()


def _expert_kernel(bexp_ref, nval_ref, off_ref, order_ref, h_hbm, w1_ref, w3_ref, w2_ref,
                   y_ref, xbuf0, xbuf1, w1b, w3b, w2b, gsem):
    i = pl.program_id(0)
    xbuf = (xbuf0, xbuf1)
    bm = xbuf0.shape[0]
    slot = lax.rem(i, 2)
    nv = nval_ref[i]
    nv_prev = jnp.where(i > 0, nval_ref[jnp.maximum(i - 1, 0)], 0)

    def gather_row(base, r, dst, priority=0):
        tok = lax.shift_right_logical(order_ref[base + r], 1)
        pltpu.make_async_copy(h_hbm.at[pl.ds(tok, 1), :], dst.at[pl.ds(r, 1), :], gsem).start(priority)

    @pl.when(i == 0)
    def _():
        def body(r, c):
            gather_row(off_ref[0], r, xbuf0)
            return c
        lax.fori_loop(0, bm, body, 0)

    @pl.when(nv > 0)
    def _():
        _wait_rows(xbuf0, gsem)

        @pl.when((i == 0) | (bexp_ref[jnp.maximum(i - 1, 0)] != bexp_ref[i]))
        def _():
            w1b[...] = w1_ref[0].astype(BF16)
            w3b[...] = w3_ref[0].astype(BF16)
            w2b[...] = w2_ref[0].astype(BF16)

        next_base = off_ref[i + 1]

        def compute(cur, oth):
            for r in range(bm):
                gather_row(next_base, r, xbuf[oth], r % 2)
            xb = xbuf[cur][...].astype(BF16)
            hid = _silu(_dot(xb, w1b[...])) * _dot(xb, w3b[...])
            y_ref[...] = _dot(hid.astype(BF16), w2b[...])

        @pl.when(slot == 0)
        def _():
            compute(0, 1)

        @pl.when(slot == 1)
        def _():
            compute(1, 0)

    @pl.when(nv == 0)
    def _():
        y_ref[...] = jnp.zeros(y_ref.shape, y_ref.dtype)

        @pl.when(nv_prev > 0)
        def _():
            _wait_rows(xbuf0, gsem)


def _experts(h2, w1, w3, w2, bexp, nval, off, order, bm):
    D = h2.shape[1]
    nb = bexp.shape[0]
    ff = w1.shape[2]
    grid_spec = pltpu.PrefetchScalarGridSpec(
        num_scalar_prefetch=4,
        grid=(nb,),
        in_specs=[
            pl.BlockSpec(memory_space=pl.ANY),
            pl.BlockSpec((1, D, ff), lambda i, be, nv, of, od: (be[i], 0, 0)),
            pl.BlockSpec((1, D, ff), lambda i, be, nv, of, od: (be[i], 0, 0)),
            pl.BlockSpec((1, ff, D), lambda i, be, nv, of, od: (be[i], 0, 0)),
        ],
        out_specs=pl.BlockSpec((bm, D), lambda i, be, nv, of, od: (i, 0)),
        scratch_shapes=[
            pltpu.VMEM((bm, D), F32),
            pltpu.VMEM((bm, D), F32),
            pltpu.VMEM((D, ff), BF16),
            pltpu.VMEM((D, ff), BF16),
            pltpu.VMEM((ff, D), BF16),
            pltpu.SemaphoreType.DMA(()),
        ],
    )
    return pl.pallas_call(
        _expert_kernel,
        grid_spec=grid_spec,
        out_shape=jax.ShapeDtypeStruct((nb * bm, D), F32),
        compiler_params=_params(("arbitrary",)),
        name="experts",
    )(bexp, nval, off, order, h2, w1, w3, w2)


def _combine_kernel(alpha, pos_ref, h_ref, rt_ref, g_ref, b_ref, y_hbm, o_ref, ybuf0, ybuf1, gsem):
    i = pl.program_id(0)
    nt = pl.num_programs(0)
    K = EXPERT_TOP_K
    tm = h_ref.shape[0]
    ybuf = (ybuf0, ybuf1)
    slot = lax.rem(i, 2)

    def gather_row(tile, r, k, dst, priority=0):
        src = pos_ref[(tile * tm + r) * K + k]
        pltpu.make_async_copy(y_hbm.at[pl.ds(src, 1), :], dst.at[k, pl.ds(r, 1), :], gsem).start(priority)

    @pl.when(i == 0)
    def _():
        def body(r, c):
            for k in range(K):
                gather_row(0, r, k, ybuf0)
            return c
        lax.fori_loop(0, tm, body, 0)

    _wait_rows(ybuf0, gsem)
    next_tile = jnp.minimum(i + 1, nt - 1)
    rt = rt_ref[...]

    def compute(cur, oth):
        for r in range(tm):
            for k in range(K):
                gather_row(next_tile, r, k, ybuf[oth], k % 2)
        ffn = ybuf[cur][0] * rt[:, 2:3] + ybuf[cur][1] * rt[:, 3:4]
        o_ref[...] = _layer_norm(alpha * h_ref[...] + ffn, g_ref[...], b_ref[...])

    @pl.when(slot == 0)
    def _():
        compute(0, 1)

    @pl.when(slot == 1)
    def _():
        compute(1, 0)

    @pl.when(i == nt - 1)
    def _():
        _wait_rows(ybuf0, gsem)


def _combine(h2, y_sorted, pos, route, g, b, alpha, tm=256):
    T, D = h2.shape
    vec = pl.BlockSpec((1, D), lambda i, ps: (0, 0))
    grid_spec = pltpu.PrefetchScalarGridSpec(
        num_scalar_prefetch=1,
        grid=(T // tm,),
        in_specs=[
            pl.BlockSpec((tm, D), lambda i, ps: (i, 0)),
            pl.BlockSpec((tm, LANES), lambda i, ps: (i, 0)),
            vec, vec,
            pl.BlockSpec(memory_space=pl.ANY),
        ],
        out_specs=pl.BlockSpec((tm, D), lambda i, ps: (i, 0)),
        scratch_shapes=[
            pltpu.VMEM((EXPERT_TOP_K, tm, D), F32),
            pltpu.VMEM((EXPERT_TOP_K, tm, D), F32),
            pltpu.SemaphoreType.DMA(()),
        ],
    )
    return pl.pallas_call(
        functools.partial(_combine_kernel, alpha),
        grid_spec=grid_spec,
        out_shape=jax.ShapeDtypeStruct((T, D), F32),
        compiler_params=_params(("arbitrary",)),
        name="combine_ln3",
    )(pos, h2, route, g, b, y_sorted)


def _routing_tables(route, lane_counts, n_tok, bm):
    K = EXPERT_TOP_K
    TK = n_tok * K
    flat_e = route[:, 0:K].astype(jnp.int32).reshape(-1)
    flat_rank = route[:, 2 * K:3 * K].astype(jnp.int32).reshape(-1)
    order = jnp.argsort(flat_e).astype(jnp.int32)
    experts = jnp.arange(N_EXPERTS, dtype=jnp.int32)
    counts = lane_counts[0, N_GROUPS:N_GROUPS + N_EXPERTS].astype(jnp.int32)
    start = jnp.cumsum(counts) - counts
    padded = (counts + bm - 1) // bm * bm
    padded_end = jnp.cumsum(padded)
    padded_start = padded_end - padded
    nb = TK // bm + N_EXPERTS
    block_start = jnp.arange(nb, dtype=jnp.int32) * bm
    bexp = jnp.minimum(jnp.sum((block_start[:, None] >= padded_end[None, :]).astype(jnp.int32), axis=1),
                       N_EXPERTS - 1)
    of_block = bexp[:, None] == experts[None, :]

    def pick(per_expert):
        return jnp.sum(jnp.where(of_block, per_expert[None, :], 0), axis=1)

    rank0 = block_start - pick(padded_start)
    nval = jnp.clip(pick(counts) - rank0, 0, bm).astype(jnp.int32)
    off = jnp.clip(pick(start) + rank0, 0, TK).astype(jnp.int32)
    order = jnp.concatenate([order, jnp.zeros((bm,), jnp.int32)])
    pos = flat_rank + jnp.sum(jnp.where(flat_e[:, None] == experts[None, :], padded_start[None, :], 0), axis=1)
    return bexp.astype(jnp.int32), nval, off, order, pos.astype(jnp.int32)


def kernel(x, mem, ln0_g, ln0_b, w_in, sconv_w, gdn_conv_w, gdn_a_log, gdn_dt_bias, gdn_norm_w,
           w_mix_out, ln1_g, ln1_b, xa_wq, xa_wk, xa_wv, xa_wo, ln2_g, ln2_b, w_group,
           w_expert_router, w1, w3, w2, ln3_g, ln3_b):
    B, S, D = x.shape
    T = B * S
    depth = w_in.shape[0]
    assert depth == 1, "single-layer stack: the out-projection kernel recomputes LayerNorm0(x)"
    alpha = float((2 * depth) ** 0.25)
    n_mem = mem.shape[1]
    bm = 256

    def vec(a):
        return a.reshape(1, -1).astype(F32)

    def pad_lanes(a):
        return jnp.pad(a, ((0, 0), (0, LANES - a.shape[1])))

    x2d = x.reshape(T, D)
    g0, b0 = vec(ln0_g), vec(ln0_b)
    l = 0
    w_main = w_in[l][:, :COL_AB].astype(BF16)
    wab_hi, wab_lo = _split_bf16(pad_lanes(w_in[l][:, COL_AB:]))
    proj, ab = _ln_inproj(x2d, g0, b0, w_main, wab_hi, wab_lo)
    gate_p = jnp.pad(jnp.stack([gdn_a_log[l], gdn_dt_bias[l]]).astype(F32),
                     ((0, HALO - 2), (0, LANES - GDN_V_HEADS)))
    y = _mixer(proj.reshape(B, S, -1), ab.reshape(B, S, LANES), sconv_w[l].astype(F32),
               gdn_conv_w[l].astype(F32), gate_p, vec(gdn_norm_w[l]), B, S)
    h1 = _outproj(y.reshape(T, -1), x2d, g0, b0, w_mix_out[l].astype(BF16), vec(ln1_g[l]), vec(ln1_b[l]), alpha)
    kv = _matmul(mem.reshape(B * n_mem, D),
                 jnp.concatenate([xa_wk[l], xa_wv[l]], axis=1).astype(BF16))
    wr_hi, wr_lo = _split_bf16(pad_lanes(jnp.concatenate([w_group[l], w_expert_router[l]], axis=1)))
    h2, logits = _xattn(h1, xa_wq[l].astype(BF16), kv, xa_wo[l].astype(BF16), vec(ln2_g[l]),
                        vec(ln2_b[l]), wr_hi, wr_lo, alpha, S, n_mem)
    route, lane_counts = _route(logits)
    bexp, nval, off, order, pos = _routing_tables(route, lane_counts, T, bm)
    y_sorted = _experts(h2, w1[l], w3[l], w2[l], bexp, nval, off, order, bm)
    out = _combine(h2, y_sorted, pos, route, vec(ln3_g[l]), vec(ln3_b[l]), alpha)
    return out.reshape(B, S, D)
```

```python
import functools

import jax
import jax.numpy as jnp
from jax import lax
from jax.experimental import pallas as pl
from jax.experimental.pallas import tpu as pltpu

F32 = jnp.float32
BF16 = jnp.bfloat16

LN_EPS = 1e-5
RMS_EPS = 1e-6
L2_EPS = 1e-6

SCONV_WIDTH = 1024
SCONV_K = 3
GDN_HEAD_DIM = 128
GDN_V_HEADS = 8
GDN_K_HEADS = 4
GDN_KEY_DIM = GDN_K_HEADS * GDN_HEAD_DIM
GDN_VALUE_DIM = GDN_V_HEADS * GDN_HEAD_DIM
GDN_CONV_K = 4
GDN_CHUNK = 64
XA_HEADS = 4
N_GROUPS = 8
EXPERTS_PER_GROUP = 8
N_EXPERTS = N_GROUPS * EXPERTS_PER_GROUP
EXPERT_TOP_K = 2

LANES = 128
HALO = 8
VMEM_LIMIT = 56 * 1024 * 1024

COL_BG = 0
COL_CG = SCONV_WIDTH
COL_XA = 2 * SCONV_WIDTH
COL_Q = 3 * SCONV_WIDTH
COL_K = COL_Q + GDN_KEY_DIM
COL_V = COL_K + GDN_KEY_DIM
COL_Z = COL_V + GDN_VALUE_DIM
COL_AB = COL_Z + GDN_VALUE_DIM

NT_DIMS = (((1,), (1,)), ((), ()))
TN_DIMS = (((0,), (0,)), ((), ()))


def _dot(a, b):
    return jnp.dot(a, b, preferred_element_type=F32)


def _split_bf16(x):
    hi = x.astype(BF16)
    lo = (x - hi.astype(F32)).astype(BF16)
    return hi, lo


def _dot_split(x_hi, x_lo, w_hi, w_lo):
    n = w_hi.shape[1]
    both = _dot(x_hi, jnp.concatenate([w_hi, w_lo], axis=1))
    return both[:, :n] + both[:, n:] + _dot(x_lo, w_hi)


def _layer_norm(x, g, b):
    mu = jnp.mean(x, axis=-1, keepdims=True)
    xc = x - mu
    var = jnp.mean(xc * xc, axis=-1, keepdims=True)
    return xc * lax.rsqrt(var + LN_EPS) * g + b


def _sigmoid(x):
    return 1.0 / (1.0 + jnp.exp(-x))


def _silu(x):
    return x * _sigmoid(x)


def _params(sem):
    return pltpu.CompilerParams(dimension_semantics=sem, vmem_limit_bytes=VMEM_LIMIT)


def _ln_inproj_kernel(x_ref, g_ref, b_ref, w_ref, wab_hi_ref, wab_lo_ref, o_ref, ab_ref, hn_ref):
    @pl.when(pl.program_id(1) == 0)
    def _():
        h = _layer_norm(x_ref[...], g_ref[...], b_ref[...])
        hi, lo = _split_bf16(h)
        hn_ref[...] = hi
        ab_ref[...] = _dot_split(hi, lo, wab_hi_ref[...], wab_lo_ref[...])

    o_ref[...] = _dot(hn_ref[...], w_ref[...]).astype(o_ref.dtype)


def _ln_inproj(x2d, g, b, w_main, wab_hi, wab_lo, tm=1024, tn=1024):
    T, D = x2d.shape
    N = COL_AB
    return pl.pallas_call(
        _ln_inproj_kernel,
        grid=(T // tm, N // tn),
        in_specs=[
            pl.BlockSpec((tm, D), lambda i, j: (i, 0)),
            pl.BlockSpec((1, D), lambda i, j: (0, 0)),
            pl.BlockSpec((1, D), lambda i, j: (0, 0)),
            pl.BlockSpec((D, tn), lambda i, j: (0, j)),
            pl.BlockSpec((D, LANES), lambda i, j: (0, 0)),
            pl.BlockSpec((D, LANES), lambda i, j: (0, 0)),
        ],
        out_specs=[
            pl.BlockSpec((tm, tn), lambda i, j: (i, j)),
            pl.BlockSpec((tm, LANES), lambda i, j: (i, 0)),
        ],
        out_shape=[
            jax.ShapeDtypeStruct((T, N), BF16),
            jax.ShapeDtypeStruct((T, LANES), F32),
        ],
        scratch_shapes=[pltpu.VMEM((tm, D), BF16)],
        compiler_params=_params(("parallel", "arbitrary")),
        name="ln_inproj",
    )(x2d, g, b, w_main, wab_hi, wab_lo)


CONV_ROWS = 64


def _conv_taps(buf, b, lanes, w_ref, w_col0, ksize, r0):
    w_lanes = slice(w_col0 + lanes.start, w_col0 + lanes.stop)
    acc = None
    for j in range(ksize):
        start = HALO - ksize + 1 + j + r0
        term = w_ref[j:j + 1, w_lanes] * buf[b, start:start + CONV_ROWS, lanes]
        acc = term if acc is None else acc + term
    return acc


def _mixer_kernel(bg_ref, cg_ref, xa_ref, q_ref, k_ref, v_ref, z_ref, ab_ref,
                  sw_ref, cw_ref, gp_ref, nw_ref,
                  y_ref,
                  sbuf, qbuf, kbuf, vbuf, qn, kn, vc, g_s, b_s, state):
    nbatch, ts = q_ref.shape[0], q_ref.shape[1]
    C = GDN_CHUNK
    Dh = GDN_HEAD_DIM
    stage_bufs = (sbuf, qbuf, kbuf, vbuf)

    @pl.when(pl.program_id(0) == 0)
    def _():
        for buf in stage_bufs:
            buf[:, 0:HALO, :] = jnp.zeros((nbatch, HALO, buf.shape[2]), F32)
        state[...] = jnp.zeros(state.shape, F32)

    row = lax.broadcasted_iota(jnp.int32, (ts, ts), 0)
    col = lax.broadcasted_iota(jnp.int32, (ts, ts), 1)
    tri = jnp.where((row // C == col // C) & (col <= row), 1.0, 0.0).astype(BF16)

    for b in range(nbatch):
        ab = ab_ref[b]
        xg = ab + gp_ref[1:2, :]
        softplus = jnp.maximum(xg, 0.0) + jnp.log1p(jnp.exp(-jnp.abs(xg)))
        g = -jnp.exp(gp_ref[0:1, :]) * softplus
        b_s[b] = _sigmoid(pltpu.roll(ab, LANES - GDN_V_HEADS, axis=1))
        g1 = g.astype(BF16)
        r1 = g - g1.astype(F32)
        g2 = r1.astype(BF16)
        g3 = (r1 - g2.astype(F32)).astype(BF16)
        g_s[b] = _dot(tri, g1) + _dot(tri, g2) + _dot(tri, g3)

    def conv_block(r0):
        for b in range(nbatch):
            rows = slice(r0, r0 + CONV_ROWS)
            srows = slice(HALO + r0, HALO + r0 + CONV_ROWS)
            for cb in range(SCONV_WIDTH // LANES):
                ls = slice(cb * LANES, (cb + 1) * LANES)
                sbuf[b, srows, ls] = cg_ref[b, rows, ls].astype(F32) * xa_ref[b, rows, ls].astype(F32)
                conv = _conv_taps(sbuf, b, ls, sw_ref, 0, SCONV_K, r0)
                y_ref[b, rows, ls] = (bg_ref[b, rows, ls].astype(F32) * conv).astype(y_ref.dtype)
            for kh in range(GDN_K_HEADS):
                ls = slice(kh * Dh, (kh + 1) * Dh)
                qbuf[b, srows, ls] = q_ref[b, rows, ls].astype(F32)
                kbuf[b, srows, ls] = k_ref[b, rows, ls].astype(F32)
                qa = _silu(_conv_taps(qbuf, b, ls, cw_ref, 0, GDN_CONV_K, r0))
                ka = _silu(_conv_taps(kbuf, b, ls, cw_ref, GDN_KEY_DIM, GDN_CONV_K, r0))
                qn[b, rows, ls] = qa * (lax.rsqrt(jnp.sum(qa * qa, axis=-1, keepdims=True) + L2_EPS) * (Dh ** -0.5))
                kn[b, rows, ls] = ka * lax.rsqrt(jnp.sum(ka * ka, axis=-1, keepdims=True) + L2_EPS)
            for h in range(GDN_V_HEADS):
                ls = slice(h * Dh, (h + 1) * Dh)
                vbuf[b, srows, ls] = v_ref[b, rows, ls].astype(F32)
                vc[b, rows, ls] = _silu(_conv_taps(vbuf, b, ls, cw_ref, 2 * GDN_KEY_DIM, GDN_CONV_K, r0))
    ri = lax.broadcasted_iota(jnp.int32, (C, C), 0)
    ci = lax.broadcasted_iota(jnp.int32, (C, C), 1)
    causal = ci <= ri
    strict = ci < ri
    eye = jnp.where(ri == ci, 1.0, 0.0).astype(F32)
    nw = nw_ref[...]

    def chunk_body(c, carry):
        r0 = pl.multiple_of(c * C, C)
        rows = pl.ds(r0, C)
        batches = range(nbatch)
        chains = [(b, h) for b in batches for h in range(GDN_V_HEADS)]
        kchains = [(b, kh) for b in batches for kh in range(GDN_K_HEADS)]

        def kidx(b, h):
            return b * GDN_K_HEADS + h // 2

        gc, gct, bc, eg, e_tail, e_last = [], [], [], [], [], []
        for b in batches:
            gc.append(g_s[b, rows, :])
            gct.append(gc[b].T)
            bc.append(b_s[b, rows, :])
            eg.append(jnp.exp(gc[b]))
            g_last = g_s[b, pl.ds(r0 + C - 1, 1), :]
            e_tail.append(jnp.exp(g_last - gc[b]))
            e_last.append(jnp.exp(g_last))
        qc, kc, kcb, kk, qk = [], [], [], [], []
        for b, kh in kchains:
            ksl = slice(kh * Dh, (kh + 1) * Dh)
            qc.append(qn[b, rows, ksl])
            kc.append(kn[b, rows, ksl])
            kcb.append(kc[-1].astype(BF16))
        for n in range(len(kchains)):
            kk.append(lax.dot_general(kcb[n], kcb[n], NT_DIMS, preferred_element_type=F32))
            qk.append(lax.dot_general(qc[n].astype(BF16), kcb[n], NT_DIMS, preferred_element_type=F32))
        decay, bcol, egc, inv, pw = [], [], [], [], []
        for b, h in chains:
            decay.append(jnp.exp(jnp.where(causal, gc[b][:, h:h + 1] - gct[b][h:h + 1, :], -1e30)))
            bcol.append(bc[b][:, h:h + 1])
            egc.append(eg[b][:, h:h + 1])
        for n, (b, h) in enumerate(chains):
            low = jnp.where(strict, kk[kidx(b, h)] * bcol[n] * decay[n], 0.0)
            lb = low.astype(BF16)
            inv.append(eye - low)
            pw.append(_dot(lb, lb))
        for it in range(5):
            for n in range(len(chains)):
                pwb = pw[n].astype(BF16)
                both = jnp.concatenate([inv[n], pw[n]], axis=0) if it < 4 else inv[n]
                prod = _dot(both.astype(BF16), pwb)
                inv[n] = inv[n] + prod[:C]
                if it < 4:
                    pw[n] = prod[C:]
        u, w = [], []
        for n, (b, h) in enumerate(chains):
            vsl = slice(h * Dh, (h + 1) * Dh)
            rhs = jnp.concatenate([vc[b, rows, vsl] * bcol[n], kc[kidx(b, h)] * (bcol[n] * egc[n])], axis=1)
            sol = _dot(inv[n].astype(BF16), rhs.astype(BF16))
            u.append(sol[:, :Dh])
            w.append(sol[:, Dh:].astype(BF16))
        s_old, sb, vnb = [], [], []
        for n, (b, h) in enumerate(chains):
            s_old.append(state[b, h])
            sb.append(s_old[n].astype(BF16))
            vnb.append((u[n] - _dot(w[n], sb[n])).astype(BF16))
        for n, (b, h) in enumerate(chains):
            vsl = slice(h * Dh, (h + 1) * Dh)
            attn = (qk[kidx(b, h)] * decay[n]).astype(BF16)
            qd = (qc[kidx(b, h)] * egc[n]).astype(BF16)
            kt = (kc[kidx(b, h)] * e_tail[b][:, h:h + 1]).astype(BF16)
            o = _dot(qd, sb[n]) + _dot(attn, vnb[n])
            state[b, h] = s_old[n] * e_last[b][:, h:h + 1] + lax.dot_general(
                kt, vnb[n], TN_DIMS, preferred_element_type=F32)
            zc = z_ref[b, rows, vsl].astype(F32)
            ms = jnp.mean(o * o, axis=-1, keepdims=True)
            yb = o * lax.rsqrt(ms + RMS_EPS) * nw * _silu(zc)
            y_ref[b, rows, SCONV_WIDTH + h * Dh:SCONV_WIDTH + (h + 1) * Dh] = yb.astype(y_ref.dtype)

        return carry

    for r0 in range(0, ts, CONV_ROWS):
        conv_block(r0)
    for buf in stage_bufs:
        buf[:, 0:HALO, :] = buf[:, ts:ts + HALO, :]
    lax.fori_loop(0, ts // C, chunk_body, 0)


def _mixer(proj, ab, sconv_w, conv_w, gate_p, norm_w, batch, seq, ts=256):
    def rows(cb, width):
        return pl.BlockSpec((batch, ts, width), lambda s, cb=cb: (0, s, cb))

    def whole(a):
        return pl.BlockSpec(a.shape, lambda s: (0,) * a.ndim)

    width_out = SCONV_WIDTH + GDN_VALUE_DIM
    return pl.pallas_call(
        _mixer_kernel,
        grid=(seq // ts,),
        in_specs=[
            rows(COL_BG // SCONV_WIDTH, SCONV_WIDTH),
            rows(COL_CG // SCONV_WIDTH, SCONV_WIDTH),
            rows(COL_XA // SCONV_WIDTH, SCONV_WIDTH),
            rows(COL_Q // GDN_KEY_DIM, GDN_KEY_DIM),
            rows(COL_K // GDN_KEY_DIM, GDN_KEY_DIM),
            rows(COL_V // GDN_VALUE_DIM, GDN_VALUE_DIM),
            rows(COL_Z // GDN_VALUE_DIM, GDN_VALUE_DIM),
            rows(0, LANES),
            whole(sconv_w), whole(conv_w), whole(gate_p), whole(norm_w),
        ],
        out_specs=pl.BlockSpec((batch, ts, width_out), lambda s: (0, s, 0)),
        out_shape=jax.ShapeDtypeStruct((batch, seq, width_out), BF16),
        scratch_shapes=[
            pltpu.VMEM((batch, ts + HALO, SCONV_WIDTH), F32),
            pltpu.VMEM((batch, ts + HALO, GDN_KEY_DIM), F32),
            pltpu.VMEM((batch, ts + HALO, GDN_KEY_DIM), F32),
            pltpu.VMEM((batch, ts + HALO, GDN_VALUE_DIM), F32),
            pltpu.VMEM((batch, ts, GDN_KEY_DIM), F32),
            pltpu.VMEM((batch, ts, GDN_KEY_DIM), F32),
            pltpu.VMEM((batch, ts, GDN_VALUE_DIM), F32),
            pltpu.VMEM((batch, ts, LANES), F32),
            pltpu.VMEM((batch, ts, LANES), F32),
            pltpu.VMEM((batch, GDN_V_HEADS, GDN_HEAD_DIM, GDN_HEAD_DIM), F32),
        ],
        compiler_params=_params(("arbitrary",)),
        name="mixer",
    )(proj, proj, proj, proj, proj, proj, proj, ab, sconv_w, conv_w, gate_p, norm_w)


def _outproj_kernel(alpha, y_ref, x_ref, g0_ref, b0_ref, w_ref, g1_ref, b1_ref, o_ref):
    h0 = _layer_norm(x_ref[...], g0_ref[...], b0_ref[...])
    r = alpha * h0 + _dot(y_ref[...], w_ref[...])
    o_ref[...] = _layer_norm(r, g1_ref[...], b1_ref[...]).astype(o_ref.dtype)


def _outproj(y, x2d, g0, b0, w, g1, b1, alpha, tm=512):
    T, D = x2d.shape
    vec = pl.BlockSpec((1, D), lambda i: (0, 0))
    return pl.pallas_call(
        functools.partial(_outproj_kernel, alpha),
        grid=(T // tm,),
        in_specs=[
            pl.BlockSpec((tm, y.shape[1]), lambda i: (i, 0)),
            pl.BlockSpec((tm, D), lambda i: (i, 0)),
            vec, vec,
            pl.BlockSpec(w.shape, lambda i: (0, 0)),
            vec, vec,
        ],
        out_specs=pl.BlockSpec((tm, D), lambda i: (i, 0)),
        out_shape=jax.ShapeDtypeStruct((T, D), BF16),
        compiler_params=_params(("parallel",)),
        name="outproj_ln1",
    )(y, x2d, g0, b0, w, g1, b1)


def _matmul_kernel(x_ref, w_ref, o_ref):
    o_ref[...] = _dot(x_ref[...].astype(BF16), w_ref[...]).astype(o_ref.dtype)


def _matmul(x, w, tn=1024):
    M, K = x.shape
    N = w.shape[1]
    return pl.pallas_call(
        _matmul_kernel,
        grid=(N // tn,),
        in_specs=[pl.BlockSpec((M, K), lambda j: (0, 0)), pl.BlockSpec((K, tn), lambda j: (0, j))],
        out_specs=pl.BlockSpec((M, tn), lambda j: (0, j)),
        out_shape=jax.ShapeDtypeStruct((M, N), BF16),
        compiler_params=_params(("parallel",)),
        name="kvproj",
    )(x, w)


def _xattn_kernel(alpha, h_ref, wq_ref, k_ref, v_ref, wo_ref, g_ref, b_ref, wr_hi_ref, wr_lo_ref,
                  o_ref, lg_ref, acc_ref):
    D = h_ref.shape[1]
    hd = D // XA_HEADS
    h1 = h_ref[...]
    acc_ref[...] = alpha * h1.astype(F32)
    for hh in range(XA_HEADS):
        sl = slice(hh * hd, (hh + 1) * hd)
        qh = _dot(h1, wq_ref[:, sl]).astype(BF16)
        s = lax.dot_general(qh, k_ref[:, sl], NT_DIMS, preferred_element_type=F32) * (hd ** -0.5)
        p = jnp.exp(s - jnp.max(s, axis=-1, keepdims=True))
        p = p / jnp.sum(p, axis=-1, keepdims=True)
        oh = _dot(p.astype(BF16), v_ref[:, sl]).astype(BF16)
        acc_ref[...] += _dot(oh, wo_ref[sl, :])
    h2 = _layer_norm(acc_ref[...], g_ref[...], b_ref[...])
    o_ref[...] = h2
    hi, lo = _split_bf16(h2)
    lg_ref[...] = _dot_split(hi, lo, wr_hi_ref[...], wr_lo_ref[...])


def _xattn(h1, wq, kv, wo, g, b, wr_hi, wr_lo, alpha, seq, n_mem, tm=512):
    T, D = h1.shape
    spb = seq // tm
    vec = pl.BlockSpec((1, D), lambda i: (0, 0))
    full = pl.BlockSpec((D, D), lambda i: (0, 0))
    return pl.pallas_call(
        functools.partial(_xattn_kernel, alpha),
        grid=(T // tm,),
        in_specs=[
            pl.BlockSpec((tm, D), lambda i: (i, 0)),
            full,
            pl.BlockSpec((n_mem, D), lambda i: (i // spb, 0)),
            pl.BlockSpec((n_mem, D), lambda i: (i // spb, 1)),
            full,
            vec, vec,
            pl.BlockSpec((D, LANES), lambda i: (0, 0)),
            pl.BlockSpec((D, LANES), lambda i: (0, 0)),
        ],
        out_specs=[
            pl.BlockSpec((tm, D), lambda i: (i, 0)),
            pl.BlockSpec((tm, LANES), lambda i: (i, 0)),
        ],
        out_shape=[
            jax.ShapeDtypeStruct((T, D), F32),
            jax.ShapeDtypeStruct((T, LANES), F32),
        ],
        scratch_shapes=[pltpu.VMEM((tm, D), F32)],
        compiler_params=_params(("parallel",)),
        name="xattn_ln2",
    )(h1, wq, kv, kv, wo, g, b, wr_hi, wr_lo)


def _route_kernel(lg_ref, o_ref, cnt_ref, carry):
    @pl.when(pl.program_id(0) == 0)
    def _():
        carry[...] = jnp.zeros(carry.shape, F32)

    lg = lg_ref[...]
    lane = lax.broadcasted_iota(jnp.int32, lg.shape, 1)
    neg = -jnp.inf
    big = jnp.int32(LANES)
    gl = jnp.where(lane < N_GROUPS, lg, neg)
    gmax = jnp.max(gl, axis=-1, keepdims=True)
    g_idx = jnp.min(jnp.where(gl == gmax, lane, big), axis=-1, keepdims=True)
    g_prob = 1.0 / jnp.sum(jnp.exp(gl - gmax), axis=-1, keepdims=True)
    e_lane = lane - N_GROUPS
    in_group = (e_lane >= g_idx * EXPERTS_PER_GROUP) & (e_lane < (g_idx + 1) * EXPERTS_PER_GROUP)
    el = jnp.where(in_group, lg, neg)
    m1 = jnp.max(el, axis=-1, keepdims=True)
    i1 = jnp.min(jnp.where(el == m1, lane, big), axis=-1, keepdims=True)
    denom = jnp.sum(jnp.exp(el - m1), axis=-1, keepdims=True)
    el2 = jnp.where(lane == i1, neg, el)
    m2 = jnp.max(el2, axis=-1, keepdims=True)
    i2 = jnp.min(jnp.where(el2 == m2, lane, big), axis=-1, keepdims=True)
    p1 = 1.0 / denom
    p2 = jnp.exp(m2 - m1) / denom
    psum = p1 + p2
    gate1 = g_prob * p1 / psum
    gate2 = g_prob * p2 / psum
    tm = lg.shape[0]
    onehot = jnp.where((lane == i1) | (lane == i2), 1.0, 0.0)
    ri = lax.broadcasted_iota(jnp.int32, (tm, tm), 0)
    ci = lax.broadcasted_iota(jnp.int32, (tm, tm), 1)
    earlier = jnp.where(ci < ri, 1.0, 0.0).astype(BF16)
    before = _dot(earlier, onehot.astype(BF16)) + carry[...]
    rank1 = jnp.sum(jnp.where(lane == i1, before, 0.0), axis=-1, keepdims=True)
    rank2 = jnp.sum(jnp.where(lane == i2, before, 0.0), axis=-1, keepdims=True)
    carry[...] += jnp.sum(onehot, axis=0, keepdims=True)
    cnt_ref[...] = carry[...]
    cols = ((i1 - N_GROUPS).astype(F32), (i2 - N_GROUPS).astype(F32), gate1, gate2, rank1, rank2)
    out = jnp.zeros(lg.shape, F32)
    for n, val in enumerate(cols):
        out = jnp.where(lane == n, val, out)
    o_ref[...] = out


def _route(logits, tm=512):
    T = logits.shape[0]
    spec = pl.BlockSpec((tm, LANES), lambda i: (i, 0))
    return pl.pallas_call(
        _route_kernel,
        grid=(T // tm,),
        in_specs=[spec],
        out_specs=[spec, pl.BlockSpec((1, LANES), lambda i: (0, 0))],
        out_shape=[jax.ShapeDtypeStruct((T, LANES), F32), jax.ShapeDtypeStruct((1, LANES), F32)],
        scratch_shapes=[pltpu.VMEM((1, LANES), F32)],
        compiler_params=_params(("arbitrary",)),
        name="route",
    )(logits)


def _wait_rows(buf, sem):
    pltpu.make_async_copy(buf, buf, sem).wait()


def _expert_kernel(bexp_ref, nval_ref, off_ref, order_ref, h_hbm, w1_ref, w3_ref, w2_ref,
                   y_ref, xbuf0, xbuf1, xbuf2, w1b, w3b, w2b, gsem):
    i = pl.program_id(0)
    last = pl.num_programs(0) - 1
    xbuf = (xbuf0, xbuf1, xbuf2)
    nbuf = len(xbuf)
    bm = xbuf0.shape[0]
    slot = lax.rem(i, nbuf)
    nv = nval_ref[i]
    nv_prev = jnp.where(i > 0, nval_ref[jnp.maximum(i - 1, 0)], 0)

    def gather_row(base, r, k, priority=0):
        tok = lax.shift_right_logical(order_ref[base + r], 1)
        pltpu.make_async_copy(h_hbm.at[pl.ds(tok, 1), :], xbuf[k].at[pl.ds(r, 1), :],
                              gsem.at[k]).start(priority)

    def for_slot(fn):
        for k in range(nbuf):
            pl.when(slot == k)(functools.partial(fn, k))

    @pl.when(i == 0)
    def _():
        def body(r, c):
            gather_row(off_ref[0], r, 0)
            gather_row(off_ref[1], r, 1)
            return c
        lax.fori_loop(0, bm, body, 0)

    @pl.when(nv > 0)
    def _():
        @pl.when((i == 0) | (bexp_ref[jnp.maximum(i - 1, 0)] != bexp_ref[i]))
        def _():
            w1b[...] = w1_ref[0].astype(BF16)
            w3b[...] = w3_ref[0].astype(BF16)
            w2b[...] = w2_ref[0].astype(BF16)

        ahead_base = off_ref[jnp.minimum(i + 2, last)]

        def compute(cur):
            _wait_rows(xbuf[cur], gsem.at[cur])
            for r in range(bm):
                gather_row(ahead_base, r, (cur + 2) % nbuf, r % 2)
            xb = xbuf[cur][...].astype(BF16)
            hid = _silu(_dot(xb, w1b[...])) * _dot(xb, w3b[...])
            y_ref[...] = _dot(hid.astype(BF16), w2b[...])

        for_slot(compute)

    @pl.when(nv == 0)
    def _():
        y_ref[...] = jnp.zeros(y_ref.shape, y_ref.dtype)

        @pl.when(nv_prev > 0)
        def _():
            def drain(cur):
                _wait_rows(xbuf[cur], gsem.at[cur])
                _wait_rows(xbuf[(cur + 1) % nbuf], gsem.at[(cur + 1) % nbuf])
            for_slot(drain)


def _experts(h2, w1, w3, w2, bexp, nval, off, order, bm):
    D = h2.shape[1]
    nb = bexp.shape[0]
    ff = w1.shape[2]
    grid_spec = pltpu.PrefetchScalarGridSpec(
        num_scalar_prefetch=4,
        grid=(nb,),
        in_specs=[
            pl.BlockSpec(memory_space=pl.ANY),
            pl.BlockSpec((1, D, ff), lambda i, be, nv, of, od: (be[i], 0, 0)),
            pl.BlockSpec((1, D, ff), lambda i, be, nv, of, od: (be[i], 0, 0)),
            pl.BlockSpec((1, ff, D), lambda i, be, nv, of, od: (be[i], 0, 0)),
        ],
        out_specs=pl.BlockSpec((bm, D), lambda i, be, nv, of, od: (i, 0)),
        scratch_shapes=[
            pltpu.VMEM((bm, D), F32),
            pltpu.VMEM((bm, D), F32),
            pltpu.VMEM((bm, D), F32),
            pltpu.VMEM((D, ff), BF16),
            pltpu.VMEM((D, ff), BF16),
            pltpu.VMEM((ff, D), BF16),
            pltpu.SemaphoreType.DMA((3,)),
        ],
    )
    return pl.pallas_call(
        _expert_kernel,
        grid_spec=grid_spec,
        out_shape=jax.ShapeDtypeStruct((nb * bm, D), F32),
        compiler_params=_params(("arbitrary",)),
        name="experts",
    )(bexp, nval, off, order, h2, w1, w3, w2)


def _combine_kernel(alpha, pos_ref, h_ref, rt_ref, g_ref, b_ref, y_hbm, o_ref, ybuf0, ybuf1, ybuf2, gsem):
    i = pl.program_id(0)
    nt = pl.num_programs(0)
    K = EXPERT_TOP_K
    tm = h_ref.shape[0]
    ybuf = (ybuf0, ybuf1, ybuf2)
    nbuf = len(ybuf)
    slot = lax.rem(i, nbuf)

    def gather_row(tile, r, k, buf, priority=0):
        src = pos_ref[(tile * tm + r) * K + k]
        pltpu.make_async_copy(y_hbm.at[pl.ds(src, 1), :], ybuf[buf].at[k, pl.ds(r, 1), :],
                              gsem.at[buf]).start(priority)

    @pl.when(i == 0)
    def _():
        def body(r, c):
            for k in range(K):
                gather_row(0, r, k, 0)
                gather_row(1, r, k, 1)
            return c
        lax.fori_loop(0, tm, body, 0)

    ahead_tile = jnp.minimum(i + 2, nt - 1)
    rt = rt_ref[...]

    def compute(cur):
        _wait_rows(ybuf[cur], gsem.at[cur])
        for r in range(tm):
            for k in range(K):
                gather_row(ahead_tile, r, k, (cur + 2) % nbuf, k % 2)
        ffn = ybuf[cur][0] * rt[:, 2:3] + ybuf[cur][1] * rt[:, 3:4]
        o_ref[...] = _layer_norm(alpha * h_ref[...] + ffn, g_ref[...], b_ref[...])

        @pl.when(i == nt - 1)
        def _():
            _wait_rows(ybuf[(cur + 1) % nbuf], gsem.at[(cur + 1) % nbuf])
            _wait_rows(ybuf[(cur + 2) % nbuf], gsem.at[(cur + 2) % nbuf])

    for k in range(nbuf):
        pl.when(slot == k)(functools.partial(compute, k))


def _combine(h2, y_sorted, pos, route, g, b, alpha, tm=256):
    T, D = h2.shape
    vec = pl.BlockSpec((1, D), lambda i, ps: (0, 0))
    grid_spec = pltpu.PrefetchScalarGridSpec(
        num_scalar_prefetch=1,
        grid=(T // tm,),
        in_specs=[
            pl.BlockSpec((tm, D), lambda i, ps: (i, 0)),
            pl.BlockSpec((tm, LANES), lambda i, ps: (i, 0)),
            vec, vec,
            pl.BlockSpec(memory_space=pl.ANY),
        ],
        out_specs=pl.BlockSpec((tm, D), lambda i, ps: (i, 0)),
        scratch_shapes=[
            pltpu.VMEM((EXPERT_TOP_K, tm, D), F32),
            pltpu.VMEM((EXPERT_TOP_K, tm, D), F32),
            pltpu.VMEM((EXPERT_TOP_K, tm, D), F32),
            pltpu.SemaphoreType.DMA((3,)),
        ],
    )
    return pl.pallas_call(
        functools.partial(_combine_kernel, alpha),
        grid_spec=grid_spec,
        out_shape=jax.ShapeDtypeStruct((T, D), F32),
        compiler_params=_params(("arbitrary",)),
        name="combine_ln3",
    )(pos, h2, route, g, b, y_sorted)


def _routing_tables(route, lane_counts, n_tok, bm):
    K = EXPERT_TOP_K
    TK = n_tok * K
    flat_e = route[:, 0:K].astype(jnp.int32).reshape(-1)
    flat_rank = route[:, 2 * K:3 * K].astype(jnp.int32).reshape(-1)
    order = jnp.argsort(flat_e).astype(jnp.int32)
    experts = jnp.arange(N_EXPERTS, dtype=jnp.int32)
    counts = lane_counts[0, N_GROUPS:N_GROUPS + N_EXPERTS].astype(jnp.int32)
    start = jnp.cumsum(counts) - counts
    padded = (counts + bm - 1) // bm * bm
    padded_end = jnp.cumsum(padded)
    padded_start = padded_end - padded
    nb = TK // bm + N_EXPERTS
    block_start = jnp.arange(nb, dtype=jnp.int32) * bm
    bexp = jnp.minimum(jnp.sum((block_start[:, None] >= padded_end[None, :]).astype(jnp.int32), axis=1),
                       N_EXPERTS - 1)
    of_block = bexp[:, None] == experts[None, :]

    def pick(per_expert):
        return jnp.sum(jnp.where(of_block, per_expert[None, :], 0), axis=1)

    rank0 = block_start - pick(padded_start)
    nval = jnp.clip(pick(counts) - rank0, 0, bm).astype(jnp.int32)
    off = jnp.clip(pick(start) + rank0, 0, TK).astype(jnp.int32)
    order = jnp.concatenate([order, jnp.zeros((bm,), jnp.int32)])
    pos = flat_rank + jnp.sum(jnp.where(flat_e[:, None] == experts[None, :], padded_start[None, :], 0), axis=1)
    return bexp.astype(jnp.int32), nval, off, order, pos.astype(jnp.int32)


def kernel(x, mem, ln0_g, ln0_b, w_in, sconv_w, gdn_conv_w, gdn_a_log, gdn_dt_bias, gdn_norm_w,
           w_mix_out, ln1_g, ln1_b, xa_wq, xa_wk, xa_wv, xa_wo, ln2_g, ln2_b, w_group,
           w_expert_router, w1, w3, w2, ln3_g, ln3_b):
    B, S, D = x.shape
    T = B * S
    depth = w_in.shape[0]
    assert depth == 1, "single-layer stack: the out-projection kernel recomputes LayerNorm0(x)"
    alpha = float((2 * depth) ** 0.25)
    n_mem = mem.shape[1]
    bm = 256

    def vec(a):
        return a.reshape(1, -1).astype(F32)

    def pad_lanes(a):
        return jnp.pad(a, ((0, 0), (0, LANES - a.shape[1])))

    x2d = x.reshape(T, D)
    g0, b0 = vec(ln0_g), vec(ln0_b)
    l = 0
    w_main = w_in[l].astype(BF16)
    wab_hi, wab_lo = _split_bf16(pad_lanes(w_in[l][:, COL_AB:]))
    proj, ab = _ln_inproj(x2d, g0, b0, w_main, wab_hi, wab_lo)
    gate_p = jnp.pad(jnp.stack([gdn_a_log[l], gdn_dt_bias[l]]).astype(F32),
                     ((0, HALO - 2), (0, LANES - GDN_V_HEADS)))
    y = _mixer(proj.reshape(B, S, -1), ab.reshape(B, S, LANES), sconv_w[l].astype(F32),
               gdn_conv_w[l].astype(F32), gate_p, vec(gdn_norm_w[l]), B, S)
    h1 = _outproj(y.reshape(T, -1), x2d, g0, b0, w_mix_out[l].astype(BF16), vec(ln1_g[l]), vec(ln1_b[l]), alpha)
    kv = _matmul(mem.reshape(B * n_mem, D),
                 jnp.concatenate([xa_wk[l], xa_wv[l]], axis=1).astype(BF16))
    wr_hi, wr_lo = _split_bf16(pad_lanes(jnp.concatenate([w_group[l], w_expert_router[l]], axis=1)))
    h2, logits = _xattn(h1, xa_wq[l].astype(BF16), kv, xa_wo[l].astype(BF16), vec(ln2_g[l]),
                        vec(ln2_b[l]), wr_hi, wr_lo, alpha, S, n_mem)
    route, lane_counts = _route(logits)
    bexp, nval, off, order, pos = _routing_tables(route, lane_counts, T, bm)
    y_sorted = _experts(h2, w1[l], w3[l], w2[l], bexp, nval, off, order, bm)
    out = _combine(h2, y_sorted, pos, route, vec(ln3_g[l]), vec(ln3_b[l]), alpha)
    return out.reshape(B, S, D)
```

```python
import functools

import jax
import jax.numpy as jnp
from jax import lax
from jax.experimental import pallas as pl
from jax.experimental.pallas import tpu as pltpu

F32 = jnp.float32
BF16 = jnp.bfloat16

LN_EPS = 1e-5
RMS_EPS = 1e-6
L2_EPS = 1e-6

SCONV_WIDTH = 1024
SCONV_K = 3
GDN_HEAD_DIM = 128
GDN_V_HEADS = 8
GDN_K_HEADS = 4
GDN_KEY_DIM = GDN_K_HEADS * GDN_HEAD_DIM
GDN_VALUE_DIM = GDN_V_HEADS * GDN_HEAD_DIM
GDN_CONV_K = 4
GDN_CHUNK = 64
XA_HEADS = 4
N_GROUPS = 8
EXPERTS_PER_GROUP = 8
N_EXPERTS = N_GROUPS * EXPERTS_PER_GROUP
EXPERT_TOP_K = 2

LANES = 128
HALO = 8
VMEM_LIMIT = 56 * 1024 * 1024

COL_BG = 0
COL_CG = SCONV_WIDTH
COL_XA = 2 * SCONV_WIDTH
COL_Q = 3 * SCONV_WIDTH
COL_K = COL_Q + GDN_KEY_DIM
COL_V = COL_K + GDN_KEY_DIM
COL_Z = COL_V + GDN_VALUE_DIM
COL_AB = COL_Z + GDN_VALUE_DIM

NT_DIMS = (((1,), (1,)), ((), ()))
TN_DIMS = (((0,), (0,)), ((), ()))


def _dot(a, b):
    return jnp.dot(a, b, preferred_element_type=F32)


def _split_bf16(x):
    hi = x.astype(BF16)
    lo = (x - hi.astype(F32)).astype(BF16)
    return hi, lo


def _dot_split(x_hi, x_lo, w_hi, w_lo):
    n = w_hi.shape[1]
    both = _dot(x_hi, jnp.concatenate([w_hi, w_lo], axis=1))
    return both[:, :n] + both[:, n:] + _dot(x_lo, w_hi)


def _layer_norm(x, g, b):
    mu = jnp.mean(x, axis=-1, keepdims=True)
    xc = x - mu
    var = jnp.mean(xc * xc, axis=-1, keepdims=True)
    return xc * lax.rsqrt(var + LN_EPS) * g + b


def _sigmoid(x):
    return 1.0 / (1.0 + jnp.exp(-x))


def _silu(x):
    return x * _sigmoid(x)


def _params(sem):
    return pltpu.CompilerParams(dimension_semantics=sem, vmem_limit_bytes=VMEM_LIMIT)


def _ln_inproj_kernel(x_ref, g_ref, b_ref, w_ref, wab_hi_ref, wab_lo_ref, o_ref, ab_ref, hn_ref):
    @pl.when(pl.program_id(1) == 0)
    def _():
        h = _layer_norm(x_ref[...], g_ref[...], b_ref[...])
        hi, lo = _split_bf16(h)
        hn_ref[...] = hi
        ab_ref[...] = _dot_split(hi, lo, wab_hi_ref[...], wab_lo_ref[...])

    o_ref[...] = _dot(hn_ref[...], w_ref[...]).astype(o_ref.dtype)


def _ln_inproj(x2d, g, b, w_main, wab_hi, wab_lo, tm=1024, tn=1024):
    T, D = x2d.shape
    N = COL_AB
    return pl.pallas_call(
        _ln_inproj_kernel,
        grid=(T // tm, N // tn),
        in_specs=[
            pl.BlockSpec((tm, D), lambda i, j: (i, 0)),
            pl.BlockSpec((1, D), lambda i, j: (0, 0)),
            pl.BlockSpec((1, D), lambda i, j: (0, 0)),
            pl.BlockSpec((D, tn), lambda i, j: (0, j)),
            pl.BlockSpec((D, LANES), lambda i, j: (0, 0)),
            pl.BlockSpec((D, LANES), lambda i, j: (0, 0)),
        ],
        out_specs=[
            pl.BlockSpec((tm, tn), lambda i, j: (i, j)),
            pl.BlockSpec((tm, LANES), lambda i, j: (i, 0)),
        ],
        out_shape=[
            jax.ShapeDtypeStruct((T, N), BF16),
            jax.ShapeDtypeStruct((T, LANES), F32),
        ],
        scratch_shapes=[pltpu.VMEM((tm, D), BF16)],
        compiler_params=_params(("parallel", "arbitrary")),
        name="ln_inproj",
    )(x2d, g, b, w_main, wab_hi, wab_lo)


CONV_ROWS = 64


def _conv_taps(buf, b, lanes, w_ref, w_col0, ksize, r0):
    w_lanes = slice(w_col0 + lanes.start, w_col0 + lanes.stop)
    acc = None
    for j in range(ksize):
        start = HALO - ksize + 1 + j + r0
        term = w_ref[j:j + 1, w_lanes] * buf[b, start:start + CONV_ROWS, lanes]
        acc = term if acc is None else acc + term
    return acc


def _mixer_kernel(bg_ref, cg_ref, xa_ref, q_ref, k_ref, v_ref, z_ref, ab_ref,
                  sw_ref, cw_ref, gp_ref, nw_ref,
                  y_ref,
                  sbuf, qbuf, kbuf, vbuf, qn, kn, vc, g_s, b_s, state):
    nbatch, ts = q_ref.shape[0], q_ref.shape[1]
    C = GDN_CHUNK
    Dh = GDN_HEAD_DIM
    stage_bufs = (sbuf, qbuf, kbuf, vbuf)

    @pl.when(pl.program_id(0) == 0)
    def _():
        for buf in stage_bufs:
            buf[:, 0:HALO, :] = jnp.zeros((nbatch, HALO, buf.shape[2]), F32)
        state[...] = jnp.zeros(state.shape, F32)

    row = lax.broadcasted_iota(jnp.int32, (ts, ts), 0)
    col = lax.broadcasted_iota(jnp.int32, (ts, ts), 1)
    tri = jnp.where((row // C == col // C) & (col <= row), 1.0, 0.0).astype(BF16)

    for b in range(nbatch):
        ab = ab_ref[b]
        xg = ab + gp_ref[1:2, :]
        softplus = jnp.maximum(xg, 0.0) + jnp.log1p(jnp.exp(-jnp.abs(xg)))
        g = -jnp.exp(gp_ref[0:1, :]) * softplus
        b_s[b] = _sigmoid(pltpu.roll(ab, LANES - GDN_V_HEADS, axis=1))
        g1 = g.astype(BF16)
        r1 = g - g1.astype(F32)
        g2 = r1.astype(BF16)
        g3 = (r1 - g2.astype(F32)).astype(BF16)
        g_s[b] = _dot(tri, g1) + _dot(tri, g2) + _dot(tri, g3)

    def conv_block(r0):
        for b in range(nbatch):
            rows = slice(r0, r0 + CONV_ROWS)
            srows = slice(HALO + r0, HALO + r0 + CONV_ROWS)
            for cb in range(SCONV_WIDTH // LANES):
                ls = slice(cb * LANES, (cb + 1) * LANES)
                sbuf[b, srows, ls] = cg_ref[b, rows, ls].astype(F32) * xa_ref[b, rows, ls].astype(F32)
                conv = _conv_taps(sbuf, b, ls, sw_ref, 0, SCONV_K, r0)
                y_ref[b, rows, ls] = (bg_ref[b, rows, ls].astype(F32) * conv).astype(y_ref.dtype)
            for kh in range(GDN_K_HEADS):
                ls = slice(kh * Dh, (kh + 1) * Dh)
                qbuf[b, srows, ls] = q_ref[b, rows, ls].astype(F32)
                kbuf[b, srows, ls] = k_ref[b, rows, ls].astype(F32)
                qa = _silu(_conv_taps(qbuf, b, ls, cw_ref, 0, GDN_CONV_K, r0))
                ka = _silu(_conv_taps(kbuf, b, ls, cw_ref, GDN_KEY_DIM, GDN_CONV_K, r0))
                qn[b, rows, ls] = qa * (lax.rsqrt(jnp.sum(qa * qa, axis=-1, keepdims=True) + L2_EPS) * (Dh ** -0.5))
                kn[b, rows, ls] = ka * lax.rsqrt(jnp.sum(ka * ka, axis=-1, keepdims=True) + L2_EPS)
            for h in range(GDN_V_HEADS):
                ls = slice(h * Dh, (h + 1) * Dh)
                vbuf[b, srows, ls] = v_ref[b, rows, ls].astype(F32)
                vc[b, rows, ls] = _silu(_conv_taps(vbuf, b, ls, cw_ref, 2 * GDN_KEY_DIM, GDN_CONV_K, r0))
    ri = lax.broadcasted_iota(jnp.int32, (C, C), 0)
    ci = lax.broadcasted_iota(jnp.int32, (C, C), 1)
    causal = ci <= ri
    strict = ci < ri
    eye = jnp.where(ri == ci, 1.0, 0.0).astype(F32)
    nw = nw_ref[...]

    def chunk_body(c, carry):
        r0 = pl.multiple_of(c * C, C)
        rows = pl.ds(r0, C)
        batches = range(nbatch)
        chains = [(b, h) for b in batches for h in range(GDN_V_HEADS)]
        kchains = [(b, kh) for b in batches for kh in range(GDN_K_HEADS)]

        def kidx(b, h):
            return b * GDN_K_HEADS + h // 2

        gc, gct, bc, eg, e_tail, e_last = [], [], [], [], [], []
        for b in batches:
            gc.append(g_s[b, rows, :])
            gct.append(gc[b].T)
            bc.append(b_s[b, rows, :])
            eg.append(jnp.exp(gc[b]))
            g_last = g_s[b, pl.ds(r0 + C - 1, 1), :]
            e_tail.append(jnp.exp(g_last - gc[b]))
            e_last.append(jnp.exp(g_last))
        qc, kc, kcb, kk, qk = [], [], [], [], []
        for b, kh in kchains:
            ksl = slice(kh * Dh, (kh + 1) * Dh)
            qc.append(qn[b, rows, ksl])
            kc.append(kn[b, rows, ksl])
            kcb.append(kc[-1].astype(BF16))
        for n in range(len(kchains)):
            kk.append(lax.dot_general(kcb[n], kcb[n], NT_DIMS, preferred_element_type=F32))
            qk.append(lax.dot_general(qc[n].astype(BF16), kcb[n], NT_DIMS, preferred_element_type=F32))
        decay, bcol, egc, inv, pw = [], [], [], [], []
        for b, h in chains:
            decay.append(jnp.exp(jnp.where(causal, gc[b][:, h:h + 1] - gct[b][h:h + 1, :], -1e30)))
            bcol.append(bc[b][:, h:h + 1])
            egc.append(eg[b][:, h:h + 1])
        for n, (b, h) in enumerate(chains):
            low = jnp.where(strict, kk[kidx(b, h)] * bcol[n] * decay[n], 0.0)
            lb = low.astype(BF16)
            inv.append(eye - low)
            pw.append(_dot(lb, lb))
        for it in range(5):
            for n in range(len(chains)):
                pwb = pw[n].astype(BF16)
                both = jnp.concatenate([inv[n], pw[n]], axis=0) if it < 4 else inv[n]
                prod = _dot(both.astype(BF16), pwb)
                inv[n] = inv[n] + prod[:C]
                if it < 4:
                    pw[n] = prod[C:]
        u, w = [], []
        for n, (b, h) in enumerate(chains):
            vsl = slice(h * Dh, (h + 1) * Dh)
            rhs = jnp.concatenate([vc[b, rows, vsl] * bcol[n], kc[kidx(b, h)] * (bcol[n] * egc[n])], axis=1)
            sol = _dot(inv[n].astype(BF16), rhs.astype(BF16))
            u.append(sol[:, :Dh])
            w.append(sol[:, Dh:].astype(BF16))
        s_old, sb, vnb = [], [], []
        for n, (b, h) in enumerate(chains):
            s_old.append(state[b, h])
            sb.append(s_old[n].astype(BF16))
            vnb.append((u[n] - _dot(w[n], sb[n])).astype(BF16))
        for n, (b, h) in enumerate(chains):
            vsl = slice(h * Dh, (h + 1) * Dh)
            attn = (qk[kidx(b, h)] * decay[n]).astype(BF16)
            qd = (qc[kidx(b, h)] * egc[n]).astype(BF16)
            kt = (kc[kidx(b, h)] * e_tail[b][:, h:h + 1]).astype(BF16)
            o = _dot(qd, sb[n]) + _dot(attn, vnb[n])
            state[b, h] = s_old[n] * e_last[b][:, h:h + 1] + lax.dot_general(
                kt, vnb[n], TN_DIMS, preferred_element_type=F32)
            zc = z_ref[b, rows, vsl].astype(F32)
            ms = jnp.mean(o * o, axis=-1, keepdims=True)
            yb = o * lax.rsqrt(ms + RMS_EPS) * nw * _silu(zc)
            y_ref[b, rows, SCONV_WIDTH + h * Dh:SCONV_WIDTH + (h + 1) * Dh] = yb.astype(y_ref.dtype)

        return carry

    for r0 in range(0, ts, CONV_ROWS):
        conv_block(r0)
    for buf in stage_bufs:
        buf[:, 0:HALO, :] = buf[:, ts:ts + HALO, :]
    lax.fori_loop(0, ts // C, chunk_body, 0)


def _mixer(proj, ab, sconv_w, conv_w, gate_p, norm_w, batch, seq, ts=256):
    def rows(cb, width):
        return pl.BlockSpec((batch, ts, width), lambda s, cb=cb: (0, s, cb))

    def whole(a):
        return pl.BlockSpec(a.shape, lambda s: (0,) * a.ndim)

    width_out = SCONV_WIDTH + GDN_VALUE_DIM
    return pl.pallas_call(
        _mixer_kernel,
        grid=(seq // ts,),
        in_specs=[
            rows(COL_BG // SCONV_WIDTH, SCONV_WIDTH),
            rows(COL_CG // SCONV_WIDTH, SCONV_WIDTH),
            rows(COL_XA // SCONV_WIDTH, SCONV_WIDTH),
            rows(COL_Q // GDN_KEY_DIM, GDN_KEY_DIM),
            rows(COL_K // GDN_KEY_DIM, GDN_KEY_DIM),
            rows(COL_V // GDN_VALUE_DIM, GDN_VALUE_DIM),
            rows(COL_Z // GDN_VALUE_DIM, GDN_VALUE_DIM),
            rows(0, LANES),
            whole(sconv_w), whole(conv_w), whole(gate_p), whole(norm_w),
        ],
        out_specs=pl.BlockSpec((batch, ts, width_out), lambda s: (0, s, 0)),
        out_shape=jax.ShapeDtypeStruct((batch, seq, width_out), BF16),
        scratch_shapes=[
            pltpu.VMEM((batch, ts + HALO, SCONV_WIDTH), F32),
            pltpu.VMEM((batch, ts + HALO, GDN_KEY_DIM), F32),
            pltpu.VMEM((batch, ts + HALO, GDN_KEY_DIM), F32),
            pltpu.VMEM((batch, ts + HALO, GDN_VALUE_DIM), F32),
            pltpu.VMEM((batch, ts, GDN_KEY_DIM), F32),
            pltpu.VMEM((batch, ts, GDN_KEY_DIM), F32),
            pltpu.VMEM((batch, ts, GDN_VALUE_DIM), F32),
            pltpu.VMEM((batch, ts, LANES), F32),
            pltpu.VMEM((batch, ts, LANES), F32),
            pltpu.VMEM((batch, GDN_V_HEADS, GDN_HEAD_DIM, GDN_HEAD_DIM), F32),
        ],
        compiler_params=_params(("arbitrary",)),
        name="mixer",
    )(proj, proj, proj, proj, proj, proj, proj, ab, sconv_w, conv_w, gate_p, norm_w)


def _outproj_kernel(alpha, y_ref, x_ref, g0_ref, b0_ref, w_ref, g1_ref, b1_ref, o_ref):
    h0 = _layer_norm(x_ref[...], g0_ref[...], b0_ref[...])
    r = alpha * h0 + _dot(y_ref[...], w_ref[...])
    o_ref[...] = _layer_norm(r, g1_ref[...], b1_ref[...]).astype(o_ref.dtype)


def _outproj(y, x2d, g0, b0, w, g1, b1, alpha, tm=512):
    T, D = x2d.shape
    vec = pl.BlockSpec((1, D), lambda i: (0, 0))
    return pl.pallas_call(
        functools.partial(_outproj_kernel, alpha),
        grid=(T // tm,),
        in_specs=[
            pl.BlockSpec((tm, y.shape[1]), lambda i: (i, 0)),
            pl.BlockSpec((tm, D), lambda i: (i, 0)),
            vec, vec,
            pl.BlockSpec(w.shape, lambda i: (0, 0)),
            vec, vec,
        ],
        out_specs=pl.BlockSpec((tm, D), lambda i: (i, 0)),
        out_shape=jax.ShapeDtypeStruct((T, D), BF16),
        compiler_params=_params(("parallel",)),
        name="outproj_ln1",
    )(y, x2d, g0, b0, w, g1, b1)


def _matmul_kernel(x_ref, w_ref, o_ref):
    o_ref[...] = _dot(x_ref[...].astype(BF16), w_ref[...]).astype(o_ref.dtype)


def _matmul(x, w, tn=1024):
    M, K = x.shape
    N = w.shape[1]
    return pl.pallas_call(
        _matmul_kernel,
        grid=(N // tn,),
        in_specs=[pl.BlockSpec((M, K), lambda j: (0, 0)), pl.BlockSpec((K, tn), lambda j: (0, j))],
        out_specs=pl.BlockSpec((M, tn), lambda j: (0, j)),
        out_shape=jax.ShapeDtypeStruct((M, N), BF16),
        compiler_params=_params(("parallel",)),
        name="kvproj",
    )(x, w)


def _xattn_kernel(alpha, h_ref, wq_ref, k_ref, v_ref, wo_ref, g_ref, b_ref, wr_hi_ref, wr_lo_ref,
                  o_ref, lg_ref, acc_ref):
    D = h_ref.shape[1]
    hd = D // XA_HEADS
    h1 = h_ref[...]
    acc_ref[...] = alpha * h1.astype(F32)
    for hh in range(XA_HEADS):
        sl = slice(hh * hd, (hh + 1) * hd)
        qh = _dot(h1, wq_ref[:, sl]).astype(BF16)
        s = lax.dot_general(qh, k_ref[:, sl], NT_DIMS, preferred_element_type=F32) * (hd ** -0.5)
        p = jnp.exp(s - jnp.max(s, axis=-1, keepdims=True))
        p = p / jnp.sum(p, axis=-1, keepdims=True)
        oh = _dot(p.astype(BF16), v_ref[:, sl]).astype(BF16)
        acc_ref[...] += _dot(oh, wo_ref[sl, :])
    h2 = _layer_norm(acc_ref[...], g_ref[...], b_ref[...])
    o_ref[...] = h2
    hi, lo = _split_bf16(h2)
    lg_ref[...] = _dot_split(hi, lo, wr_hi_ref[...], wr_lo_ref[...])


def _xattn(h1, wq, kv, wo, g, b, wr_hi, wr_lo, alpha, seq, n_mem, tm=512):
    T, D = h1.shape
    spb = seq // tm
    vec = pl.BlockSpec((1, D), lambda i: (0, 0))
    full = pl.BlockSpec((D, D), lambda i: (0, 0))
    return pl.pallas_call(
        functools.partial(_xattn_kernel, alpha),
        grid=(T // tm,),
        in_specs=[
            pl.BlockSpec((tm, D), lambda i: (i, 0)),
            full,
            pl.BlockSpec((n_mem, D), lambda i: (i // spb, 0)),
            pl.BlockSpec((n_mem, D), lambda i: (i // spb, 1)),
            full,
            vec, vec,
            pl.BlockSpec((D, LANES), lambda i: (0, 0)),
            pl.BlockSpec((D, LANES), lambda i: (0, 0)),
        ],
        out_specs=[
            pl.BlockSpec((tm, D), lambda i: (i, 0)),
            pl.BlockSpec((tm, LANES), lambda i: (i, 0)),
        ],
        out_shape=[
            jax.ShapeDtypeStruct((T, D), F32),
            jax.ShapeDtypeStruct((T, LANES), F32),
        ],
        scratch_shapes=[pltpu.VMEM((tm, D), F32)],
        compiler_params=_params(("parallel",)),
        name="xattn_ln2",
    )(h1, wq, kv, kv, wo, g, b, wr_hi, wr_lo)


def _route_kernel(lg_ref, o_ref, cnt_ref, carry):
    @pl.when(pl.program_id(0) == 0)
    def _():
        carry[...] = jnp.zeros(carry.shape, F32)

    lg = lg_ref[...]
    lane = lax.broadcasted_iota(jnp.int32, lg.shape, 1)
    neg = -jnp.inf
    big = jnp.int32(LANES)
    gl = jnp.where(lane < N_GROUPS, lg, neg)
    gmax = jnp.max(gl, axis=-1, keepdims=True)
    g_idx = jnp.min(jnp.where(gl == gmax, lane, big), axis=-1, keepdims=True)
    g_prob = 1.0 / jnp.sum(jnp.exp(gl - gmax), axis=-1, keepdims=True)
    e_lane = lane - N_GROUPS
    in_group = (e_lane >= g_idx * EXPERTS_PER_GROUP) & (e_lane < (g_idx + 1) * EXPERTS_PER_GROUP)
    el = jnp.where(in_group, lg, neg)
    m1 = jnp.max(el, axis=-1, keepdims=True)
    i1 = jnp.min(jnp.where(el == m1, lane, big), axis=-1, keepdims=True)
    denom = jnp.sum(jnp.exp(el - m1), axis=-1, keepdims=True)
    el2 = jnp.where(lane == i1, neg, el)
    m2 = jnp.max(el2, axis=-1, keepdims=True)
    i2 = jnp.min(jnp.where(el2 == m2, lane, big), axis=-1, keepdims=True)
    p1 = 1.0 / denom
    p2 = jnp.exp(m2 - m1) / denom
    psum = p1 + p2
    gate1 = g_prob * p1 / psum
    gate2 = g_prob * p2 / psum
    tm = lg.shape[0]
    onehot = jnp.where((lane == i1) | (lane == i2), 1.0, 0.0)
    ri = lax.broadcasted_iota(jnp.int32, (tm, tm), 0)
    ci = lax.broadcasted_iota(jnp.int32, (tm, tm), 1)
    earlier = jnp.where(ci < ri, 1.0, 0.0).astype(BF16)
    before = _dot(earlier, onehot.astype(BF16)) + carry[...]
    rank1 = jnp.sum(jnp.where(lane == i1, before, 0.0), axis=-1, keepdims=True)
    rank2 = jnp.sum(jnp.where(lane == i2, before, 0.0), axis=-1, keepdims=True)
    carry[...] += jnp.sum(onehot, axis=0, keepdims=True)
    cnt_ref[...] = carry[...]
    cols = ((i1 - N_GROUPS).astype(F32), (i2 - N_GROUPS).astype(F32), gate1, gate2, rank1, rank2)
    out = jnp.zeros(lg.shape, F32)
    for n, val in enumerate(cols):
        out = jnp.where(lane == n, val, out)
    o_ref[...] = out


def _route(logits, tm=512):
    T = logits.shape[0]
    spec = pl.BlockSpec((tm, LANES), lambda i: (i, 0))
    return pl.pallas_call(
        _route_kernel,
        grid=(T // tm,),
        in_specs=[spec],
        out_specs=[spec, pl.BlockSpec((1, LANES), lambda i: (0, 0))],
        out_shape=[jax.ShapeDtypeStruct((T, LANES), F32), jax.ShapeDtypeStruct((1, LANES), F32)],
        scratch_shapes=[pltpu.VMEM((1, LANES), F32)],
        compiler_params=_params(("arbitrary",)),
        name="route",
    )(logits)


def _wait_rows(buf, sem):
    pltpu.make_async_copy(buf, buf, sem).wait()


def _expert_kernel(bexp_ref, nval_ref, off_ref, first_ref, wslot_ref, nexte_ref, order_ref,
                   h_hbm, w1_hbm, w3_hbm, w2_hbm,
                   y_ref, xbuf0, xbuf1, xbuf2, w1s, w3s, w2s, w1b, w3b, w2b, gsem, wsem):
    i = pl.program_id(0)
    last = pl.num_programs(0) - 1
    xbuf = (xbuf0, xbuf1, xbuf2)
    nbuf = len(xbuf)
    bm = xbuf0.shape[0]
    slot = lax.rem(i, nbuf)
    nv = nval_ref[i]
    nv_prev = jnp.where(i > 0, nval_ref[jnp.maximum(i - 1, 0)], 0)

    def gather_row(base, r, k, priority=0):
        tok = lax.shift_right_logical(order_ref[base + r], 1)
        pltpu.make_async_copy(h_hbm.at[pl.ds(tok, 1), :], xbuf[k].at[pl.ds(r, 1), :],
                              gsem.at[k]).start(priority)

    def for_slot(fn):
        for k in range(nbuf):
            pl.when(slot == k)(functools.partial(fn, k))

    def weight_copies(e, s):
        return (pltpu.make_async_copy(w1_hbm.at[e], w1s.at[s], wsem.at[s]),
                pltpu.make_async_copy(w3_hbm.at[e], w3s.at[s], wsem.at[s]),
                pltpu.make_async_copy(w2_hbm.at[e], w2s.at[s], wsem.at[s]))

    @pl.when(i == 0)
    def _():
        for c in weight_copies(bexp_ref[0], 0):
            c.start(priority=1)

        def body(r, c):
            gather_row(off_ref[0], r, 0)
            gather_row(off_ref[1], r, 1)
            return c
        lax.fori_loop(0, bm, body, 0)

    @pl.when(nv > 0)
    def _():
        @pl.when(first_ref[i] == 1)
        def _():
            ws = wslot_ref[i]
            nxt = nexte_ref[i]

            @pl.when(nxt >= 0)
            def _():
                for c in weight_copies(nxt, 1 - ws):
                    c.start(priority=1)

            def cast(s):
                for c in weight_copies(0, s):
                    c.wait()
                w1b[...] = w1s[s].astype(BF16)
                w3b[...] = w3s[s].astype(BF16)
                w2b[...] = w2s[s].astype(BF16)

            for s in range(2):
                pl.when(ws == s)(functools.partial(cast, s))

        ahead_base = off_ref[jnp.minimum(i + 2, last)]

        def compute(cur):
            _wait_rows(xbuf[cur], gsem.at[cur])
            for r in range(bm):
                gather_row(ahead_base, r, (cur + 2) % nbuf)
            xb = xbuf[cur][...].astype(BF16)
            hid = _silu(_dot(xb, w1b[...])) * _dot(xb, w3b[...])
            y_ref[...] = _dot(hid.astype(BF16), w2b[...])

        for_slot(compute)

    @pl.when(nv == 0)
    def _():
        y_ref[...] = jnp.zeros(y_ref.shape, y_ref.dtype)

        @pl.when(nv_prev > 0)
        def _():
            def drain(cur):
                _wait_rows(xbuf[cur], gsem.at[cur])
                _wait_rows(xbuf[(cur + 1) % nbuf], gsem.at[(cur + 1) % nbuf])
            for_slot(drain)


def _experts(h2, w1, w3, w2, tables, bm):
    bexp, nval, off, first, wslot, nexte, order = tables
    D = h2.shape[1]
    nb = bexp.shape[0]
    ff = w1.shape[2]
    hbm = pl.BlockSpec(memory_space=pl.ANY)
    grid_spec = pltpu.PrefetchScalarGridSpec(
        num_scalar_prefetch=len(tables),
        grid=(nb,),
        in_specs=[hbm, hbm, hbm, hbm],
        out_specs=pl.BlockSpec((bm, D), lambda i, *tables: (i, 0)),
        scratch_shapes=[
            pltpu.VMEM((bm, D), F32),
            pltpu.VMEM((bm, D), F32),
            pltpu.VMEM((bm, D), F32),
            pltpu.VMEM((2, D, ff), F32),
            pltpu.VMEM((2, D, ff), F32),
            pltpu.VMEM((2, ff, D), F32),
            pltpu.VMEM((D, ff), BF16),
            pltpu.VMEM((D, ff), BF16),
            pltpu.VMEM((ff, D), BF16),
            pltpu.SemaphoreType.DMA((3,)),
            pltpu.SemaphoreType.DMA((2,)),
        ],
    )
    return pl.pallas_call(
        _expert_kernel,
        grid_spec=grid_spec,
        out_shape=jax.ShapeDtypeStruct((nb * bm, D), F32),
        compiler_params=_params(("arbitrary",)),
        name="experts",
    )(*tables, h2, w1, w3, w2)


def _combine_kernel(alpha, pos_ref, h_ref, rt_ref, g_ref, b_ref, y_hbm, o_ref, ybuf0, ybuf1, ybuf2, gsem):
    i = pl.program_id(0)
    nt = pl.num_programs(0)
    K = EXPERT_TOP_K
    tm = h_ref.shape[0]
    ybuf = (ybuf0, ybuf1, ybuf2)
    nbuf = len(ybuf)
    slot = lax.rem(i, nbuf)

    def gather_row(tile, r, k, buf, priority=0):
        src = pos_ref[(tile * tm + r) * K + k]
        pltpu.make_async_copy(y_hbm.at[pl.ds(src, 1), :], ybuf[buf].at[k, pl.ds(r, 1), :],
                              gsem.at[buf]).start(priority)

    @pl.when(i == 0)
    def _():
        def body(r, c):
            for k in range(K):
                gather_row(0, r, k, 0)
                gather_row(1, r, k, 1)
            return c
        lax.fori_loop(0, tm, body, 0)

    ahead_tile = jnp.minimum(i + 2, nt - 1)
    rt = rt_ref[...]

    def compute(cur):
        _wait_rows(ybuf[cur], gsem.at[cur])
        for r in range(tm):
            for k in range(K):
                gather_row(ahead_tile, r, k, (cur + 2) % nbuf, k % 2)
        ffn = ybuf[cur][0] * rt[:, 2:3] + ybuf[cur][1] * rt[:, 3:4]
        o_ref[...] = _layer_norm(alpha * h_ref[...] + ffn, g_ref[...], b_ref[...])

        @pl.when(i == nt - 1)
        def _():
            _wait_rows(ybuf[(cur + 1) % nbuf], gsem.at[(cur + 1) % nbuf])
            _wait_rows(ybuf[(cur + 2) % nbuf], gsem.at[(cur + 2) % nbuf])

    for k in range(nbuf):
        pl.when(slot == k)(functools.partial(compute, k))


def _combine(h2, y_sorted, pos, route, g, b, alpha, tm=256):
    T, D = h2.shape
    vec = pl.BlockSpec((1, D), lambda i, ps: (0, 0))
    grid_spec = pltpu.PrefetchScalarGridSpec(
        num_scalar_prefetch=1,
        grid=(T // tm,),
        in_specs=[
            pl.BlockSpec((tm, D), lambda i, ps: (i, 0)),
            pl.BlockSpec((tm, LANES), lambda i, ps: (i, 0)),
            vec, vec,
            pl.BlockSpec(memory_space=pl.ANY),
        ],
        out_specs=pl.BlockSpec((tm, D), lambda i, ps: (i, 0)),
        scratch_shapes=[
            pltpu.VMEM((EXPERT_TOP_K, tm, D), F32),
            pltpu.VMEM((EXPERT_TOP_K, tm, D), F32),
            pltpu.VMEM((EXPERT_TOP_K, tm, D), F32),
            pltpu.SemaphoreType.DMA((3,)),
        ],
    )
    return pl.pallas_call(
        functools.partial(_combine_kernel, alpha),
        grid_spec=grid_spec,
        out_shape=jax.ShapeDtypeStruct((T, D), F32),
        compiler_params=_params(("arbitrary",)),
        name="combine_ln3",
    )(pos, h2, route, g, b, y_sorted)


def _routing_tables(route, lane_counts, n_tok, bm):
    K = EXPERT_TOP_K
    TK = n_tok * K
    flat_e = route[:, 0:K].astype(jnp.int32).reshape(-1)
    flat_rank = route[:, 2 * K:3 * K].astype(jnp.int32).reshape(-1)
    order = jnp.argsort(flat_e).astype(jnp.int32)
    experts = jnp.arange(N_EXPERTS, dtype=jnp.int32)
    counts = lane_counts[0, N_GROUPS:N_GROUPS + N_EXPERTS].astype(jnp.int32)
    start = jnp.cumsum(counts) - counts
    padded = (counts + bm - 1) // bm * bm
    padded_end = jnp.cumsum(padded)
    padded_start = padded_end - padded
    nb = TK // bm + N_EXPERTS
    block_start = jnp.arange(nb, dtype=jnp.int32) * bm
    bexp = jnp.minimum(jnp.sum((block_start[:, None] >= padded_end[None, :]).astype(jnp.int32), axis=1),
                       N_EXPERTS - 1)
    of_block = bexp[:, None] == experts[None, :]

    def pick(per_expert):
        return jnp.sum(jnp.where(of_block, per_expert[None, :], 0), axis=1)

    rank0 = block_start - pick(padded_start)
    nval = jnp.clip(pick(counts) - rank0, 0, bm).astype(jnp.int32)
    off = jnp.clip(pick(start) + rank0, 0, TK).astype(jnp.int32)
    order = jnp.concatenate([order, jnp.zeros((bm,), jnp.int32)])
    pos = flat_rank + jnp.sum(jnp.where(flat_e[:, None] == experts[None, :], padded_start[None, :], 0), axis=1)
    bexp = bexp.astype(jnp.int32)
    active = counts > 0
    ordinal = jnp.cumsum(active.astype(jnp.int32)) - 1
    later_active = (experts[None, :] > experts[:, None]) & active[None, :]
    next_active = jnp.min(jnp.where(later_active, experts[None, :], N_EXPERTS), axis=1)
    next_active = jnp.where(next_active == N_EXPERTS, -1, next_active)
    prev_bexp = jnp.concatenate([jnp.full((1,), -1, jnp.int32), bexp[:-1]])
    first = ((nval > 0) & (bexp != prev_bexp)).astype(jnp.int32)
    wslot = (pick(ordinal) % 2).astype(jnp.int32)
    nexte = pick(next_active).astype(jnp.int32)
    return (bexp, nval, off, first, wslot, nexte, order), pos.astype(jnp.int32)


def kernel(x, mem, ln0_g, ln0_b, w_in, sconv_w, gdn_conv_w, gdn_a_log, gdn_dt_bias, gdn_norm_w,
           w_mix_out, ln1_g, ln1_b, xa_wq, xa_wk, xa_wv, xa_wo, ln2_g, ln2_b, w_group,
           w_expert_router, w1, w3, w2, ln3_g, ln3_b):
    B, S, D = x.shape
    T = B * S
    depth = w_in.shape[0]
    assert depth == 1, "single-layer stack: the out-projection kernel recomputes LayerNorm0(x)"
    alpha = float((2 * depth) ** 0.25)
    n_mem = mem.shape[1]
    bm = 256

    def vec(a):
        return a.reshape(1, -1).astype(F32)

    def pad_lanes(a):
        return jnp.pad(a, ((0, 0), (0, LANES - a.shape[1])))

    x2d = x.reshape(T, D)
    g0, b0 = vec(ln0_g), vec(ln0_b)
    l = 0
    w_main = w_in[l].astype(BF16)
    wab_hi, wab_lo = _split_bf16(pad_lanes(w_in[l][:, COL_AB:]))
    proj, ab = _ln_inproj(x2d, g0, b0, w_main, wab_hi, wab_lo)
    gate_p = jnp.pad(jnp.stack([gdn_a_log[l], gdn_dt_bias[l]]).astype(F32),
                     ((0, HALO - 2), (0, LANES - GDN_V_HEADS)))
    y = _mixer(proj.reshape(B, S, -1), ab.reshape(B, S, LANES), sconv_w[l].astype(F32),
               gdn_conv_w[l].astype(F32), gate_p, vec(gdn_norm_w[l]), B, S)
    h1 = _outproj(y.reshape(T, -1), x2d, g0, b0, w_mix_out[l].astype(BF16), vec(ln1_g[l]), vec(ln1_b[l]), alpha)
    kv = _matmul(mem.reshape(B * n_mem, D),
                 jnp.concatenate([xa_wk[l], xa_wv[l]], axis=1).astype(BF16))
    wr_hi, wr_lo = _split_bf16(pad_lanes(jnp.concatenate([w_group[l], w_expert_router[l]], axis=1)))
    h2, logits = _xattn(h1, xa_wq[l].astype(BF16), kv, xa_wo[l].astype(BF16), vec(ln2_g[l]),
                        vec(ln2_b[l]), wr_hi, wr_lo, alpha, S, n_mem)
    route, lane_counts = _route(logits)
    tables, pos = _routing_tables(route, lane_counts, T, bm)
    y_sorted = _experts(h2, w1[l], w3[l], w2[l], tables, bm)
    out = _combine(h2, y_sorted, pos, route, vec(ln3_g[l]), vec(ln3_b[l]), alpha)
    return out.reshape(B, S, D)
```

```python
import functools

import jax
import jax.numpy as jnp
from jax import lax
from jax.experimental import pallas as pl
from jax.experimental.pallas import tpu as pltpu

F32 = jnp.float32
BF16 = jnp.bfloat16

LN_EPS = 1e-5
RMS_EPS = 1e-6
L2_EPS = 1e-6

SCONV_WIDTH = 1024
SCONV_K = 3
GDN_HEAD_DIM = 128
GDN_V_HEADS = 8
GDN_K_HEADS = 4
GDN_KEY_DIM = GDN_K_HEADS * GDN_HEAD_DIM
GDN_VALUE_DIM = GDN_V_HEADS * GDN_HEAD_DIM
GDN_CONV_K = 4
GDN_CHUNK = 64
XA_HEADS = 4
N_GROUPS = 8
EXPERTS_PER_GROUP = 8
N_EXPERTS = N_GROUPS * EXPERTS_PER_GROUP
EXPERT_TOP_K = 2

LANES = 128
HALO = 8
VMEM_LIMIT = 56 * 1024 * 1024

COL_BG = 0
COL_CG = SCONV_WIDTH
COL_XA = 2 * SCONV_WIDTH
COL_Q = 3 * SCONV_WIDTH
COL_K = COL_Q + GDN_KEY_DIM
COL_V = COL_K + GDN_KEY_DIM
COL_Z = COL_V + GDN_VALUE_DIM
COL_AB = COL_Z + GDN_VALUE_DIM

NT_DIMS = (((1,), (1,)), ((), ()))
TN_DIMS = (((0,), (0,)), ((), ()))


def _dot(a, b):
    return jnp.dot(a, b, preferred_element_type=F32)


def _split_bf16(x):
    hi = x.astype(BF16)
    lo = (x - hi.astype(F32)).astype(BF16)
    return hi, lo


def _dot_split(x_hi, x_lo, w_hi, w_lo):
    n = w_hi.shape[1]
    both = _dot(x_hi, jnp.concatenate([w_hi, w_lo], axis=1))
    return both[:, :n] + both[:, n:] + _dot(x_lo, w_hi)


def _layer_norm(x, g, b):
    mu = jnp.mean(x, axis=-1, keepdims=True)
    xc = x - mu
    var = jnp.mean(xc * xc, axis=-1, keepdims=True)
    return xc * lax.rsqrt(var + LN_EPS) * g + b


def _sigmoid(x):
    return 1.0 / (1.0 + jnp.exp(-x))


def _silu(x):
    return x * _sigmoid(x)


def _params(sem):
    return pltpu.CompilerParams(dimension_semantics=sem, vmem_limit_bytes=VMEM_LIMIT)


def _ln_inproj_kernel(x_ref, g_ref, b_ref, w_ref, wab_hi_ref, wab_lo_ref, o_ref, ab_ref, hn_ref):
    @pl.when(pl.program_id(1) == 0)
    def _():
        h = _layer_norm(x_ref[...], g_ref[...], b_ref[...])
        hi, lo = _split_bf16(h)
        hn_ref[...] = hi
        ab_ref[...] = _dot_split(hi, lo, wab_hi_ref[...], wab_lo_ref[...])

    o_ref[...] = _dot(hn_ref[...], w_ref[...]).astype(o_ref.dtype)


def _ln_inproj(x2d, g, b, w_main, wab_hi, wab_lo, tm=1024, tn=1024):
    T, D = x2d.shape
    N = COL_AB
    return pl.pallas_call(
        _ln_inproj_kernel,
        grid=(T // tm, N // tn),
        in_specs=[
            pl.BlockSpec((tm, D), lambda i, j: (i, 0)),
            pl.BlockSpec((1, D), lambda i, j: (0, 0)),
            pl.BlockSpec((1, D), lambda i, j: (0, 0)),
            pl.BlockSpec((D, tn), lambda i, j: (0, j)),
            pl.BlockSpec((D, LANES), lambda i, j: (0, 0)),
            pl.BlockSpec((D, LANES), lambda i, j: (0, 0)),
        ],
        out_specs=[
            pl.BlockSpec((tm, tn), lambda i, j: (i, j)),
            pl.BlockSpec((tm, LANES), lambda i, j: (i, 0)),
        ],
        out_shape=[
            jax.ShapeDtypeStruct((T, N), BF16),
            jax.ShapeDtypeStruct((T, LANES), F32),
        ],
        scratch_shapes=[pltpu.VMEM((tm, D), BF16)],
        compiler_params=_params(("parallel", "arbitrary")),
        name="ln_inproj",
    )(x2d, g, b, w_main, wab_hi, wab_lo)


CONV_ROWS = 64


def _conv_taps(buf, base, lanes, w_ref, w_col0, ksize, r0):
    w_lanes = slice(w_col0 + lanes.start, w_col0 + lanes.stop)
    acc = None
    for j in range(ksize):
        start = base + HALO - ksize + 1 + j + r0
        term = w_ref[j:j + 1, w_lanes] * buf[lanes.start // LANES, pl.ds(start, CONV_ROWS), :]
        acc = term if acc is None else acc + term
    return acc


def _mixer_kernel(bg_ref, cg_ref, xa_ref, q_ref, k_ref, v_ref, z_ref, ab_ref,
                  sw_ref, cw_ref, gp_ref, nw_ref,
                  y_ref,
                  sbuf, qbuf, kbuf, vbuf, qn, kn, vc, g_s, b_s, state):
    nbatch, ts = q_ref.shape[0], q_ref.shape[1]
    C = GDN_CHUNK
    Dh = GDN_HEAD_DIM
    stage_bufs = (sbuf, qbuf, kbuf, vbuf)
    pitch = ts + HALO

    @pl.when(pl.program_id(0) == 0)
    def _():
        for buf in stage_bufs:
            for b in range(nbatch):
                buf[:, b * pitch:b * pitch + HALO, :] = jnp.zeros((buf.shape[0], HALO, LANES), F32)
        state[...] = jnp.zeros(state.shape, F32)

    row = lax.broadcasted_iota(jnp.int32, (ts, ts), 0)
    col = lax.broadcasted_iota(jnp.int32, (ts, ts), 1)
    tri = jnp.where((row // C == col // C) & (col <= row), 1.0, 0.0).astype(BF16)

    for b in range(nbatch):
        ab = ab_ref[b]
        xg = ab + gp_ref[1:2, :]
        softplus = jnp.maximum(xg, 0.0) + jnp.log1p(jnp.exp(-jnp.abs(xg)))
        g = -jnp.exp(gp_ref[0:1, :]) * softplus
        b_s[b] = _sigmoid(pltpu.roll(ab, LANES - GDN_V_HEADS, axis=1))
        g1 = g.astype(BF16)
        r1 = g - g1.astype(F32)
        g2 = r1.astype(BF16)
        g3 = (r1 - g2.astype(F32)).astype(BF16)
        g_s[b] = _dot(tri, g1) + _dot(tri, g2) + _dot(tri, g3)

    def conv_units(r0):
        rows = pl.ds(r0, CONV_ROWS)

        def srows(b):
            return pl.ds(b * pitch + HALO + r0, CONV_ROWS)

        def sconv_unit(b, ls):
            sbuf[ls.start // LANES, srows(b), :] = cg_ref[b, rows, ls].astype(F32) * xa_ref[b, rows, ls].astype(F32)
            conv = _conv_taps(sbuf, b * pitch, ls, sw_ref, 0, SCONV_K, r0)
            y_ref[b, rows, ls] = (bg_ref[b, rows, ls].astype(F32) * conv).astype(y_ref.dtype)

        def qk_unit(b, ls):
            qbuf[ls.start // LANES, srows(b), :] = q_ref[b, rows, ls].astype(F32)
            kbuf[ls.start // LANES, srows(b), :] = k_ref[b, rows, ls].astype(F32)
            qa = _silu(_conv_taps(qbuf, b * pitch, ls, cw_ref, 0, GDN_CONV_K, r0))
            ka = _silu(_conv_taps(kbuf, b * pitch, ls, cw_ref, GDN_KEY_DIM, GDN_CONV_K, r0))
            qn[b, rows, ls] = qa * (lax.rsqrt(jnp.sum(qa * qa, axis=-1, keepdims=True) + L2_EPS) * (Dh ** -0.5))
            kn[b, rows, ls] = ka * lax.rsqrt(jnp.sum(ka * ka, axis=-1, keepdims=True) + L2_EPS)

        def v_unit(b, ls):
            vbuf[ls.start // LANES, srows(b), :] = v_ref[b, rows, ls].astype(F32)
            vc[b, rows, ls] = _silu(_conv_taps(vbuf, b * pitch, ls, cw_ref, 2 * GDN_KEY_DIM, GDN_CONV_K, r0))

        units = []
        for b in range(nbatch):
            for n in range(SCONV_WIDTH // LANES):
                units.append(functools.partial(sconv_unit, b, slice(n * LANES, (n + 1) * LANES)))
            for n in range(GDN_K_HEADS):
                units.append(functools.partial(qk_unit, b, slice(n * Dh, (n + 1) * Dh)))
            for n in range(GDN_V_HEADS):
                units.append(functools.partial(v_unit, b, slice(n * Dh, (n + 1) * Dh)))
        return units
    ri = lax.broadcasted_iota(jnp.int32, (C, C), 0)
    ci = lax.broadcasted_iota(jnp.int32, (C, C), 1)
    causal = ci <= ri
    strict = ci < ri
    eye = jnp.where(ri == ci, 1.0, 0.0).astype(F32)
    nw = nw_ref[...]

    def chunk_step(r0, side_work):
        rows = pl.ds(r0, C)
        side_work = list(side_work)
        per_stage = -(-len(side_work) // 9)

        def fill():
            for _ in range(min(per_stage, len(side_work))):
                side_work.pop(0)()

        batches = range(nbatch)
        chains = [(b, h) for b in batches for h in range(GDN_V_HEADS)]
        kchains = [(b, kh) for b in batches for kh in range(GDN_K_HEADS)]

        def kidx(b, h):
            return b * GDN_K_HEADS + h // 2

        gc, gct, bc, eg, e_tail, e_last = [], [], [], [], [], []
        for b in batches:
            gc.append(g_s[b, rows, :])
            gct.append(gc[b].T)
            bc.append(b_s[b, rows, :])
            eg.append(jnp.exp(gc[b]))
            g_last = g_s[b, pl.ds(r0 + C - 1, 1), :]
            e_tail.append(jnp.exp(g_last - gc[b]))
            e_last.append(jnp.exp(g_last))
        qc, kc, kcb, kk, qk = [], [], [], [], []
        for b, kh in kchains:
            ksl = slice(kh * Dh, (kh + 1) * Dh)
            qc.append(qn[b, rows, ksl])
            kc.append(kn[b, rows, ksl])
            kcb.append(kc[-1].astype(BF16))
        for n in range(len(kchains)):
            kk.append(lax.dot_general(kcb[n], kcb[n], NT_DIMS, preferred_element_type=F32))
            qk.append(lax.dot_general(qc[n].astype(BF16), kcb[n], NT_DIMS, preferred_element_type=F32))
        fill()
        decay, bcol, egc, inv, pw = [], [], [], [], []
        for b, h in chains:
            decay.append(jnp.exp(jnp.where(causal, gc[b][:, h:h + 1] - gct[b][h:h + 1, :], -1e30)))
            bcol.append(bc[b][:, h:h + 1])
            egc.append(eg[b][:, h:h + 1])
        for n, (b, h) in enumerate(chains):
            low = jnp.where(strict, kk[kidx(b, h)] * bcol[n] * decay[n], 0.0)
            lb = low.astype(BF16)
            inv.append(eye - low)
            pw.append(_dot(lb, lb))
        fill()
        for it in range(5):
            for n in range(len(chains)):
                pwb = pw[n].astype(BF16)
                both = jnp.concatenate([inv[n], pw[n]], axis=0) if it < 4 else inv[n]
                prod = _dot(both.astype(BF16), pwb)
                inv[n] = inv[n] + prod[:C]
                if it < 4:
                    pw[n] = prod[C:]
            fill()
        u, w = [], []
        for n, (b, h) in enumerate(chains):
            vsl = slice(h * Dh, (h + 1) * Dh)
            rhs = jnp.concatenate([vc[b, rows, vsl] * bcol[n], kc[kidx(b, h)] * (bcol[n] * egc[n])], axis=1)
            sol = _dot(inv[n].astype(BF16), rhs.astype(BF16))
            u.append(sol[:, :Dh])
            w.append(sol[:, Dh:].astype(BF16))
        fill()
        s_old, sb, vnb = [], [], []
        for n, (b, h) in enumerate(chains):
            s_old.append(state[b, h])
            sb.append(s_old[n].astype(BF16))
            vnb.append((u[n] - _dot(w[n], sb[n])).astype(BF16))
        fill()
        for n, (b, h) in enumerate(chains):
            vsl = slice(h * Dh, (h + 1) * Dh)
            attn = (qk[kidx(b, h)] * decay[n]).astype(BF16)
            qd = (qc[kidx(b, h)] * egc[n]).astype(BF16)
            kt = (kc[kidx(b, h)] * e_tail[b][:, h:h + 1]).astype(BF16)
            o = _dot(qd, sb[n]) + _dot(attn, vnb[n])
            state[b, h] = s_old[n] * e_last[b][:, h:h + 1] + lax.dot_general(
                kt, vnb[n], TN_DIMS, preferred_element_type=F32)
            zc = z_ref[b, rows, vsl].astype(F32)
            ms = jnp.mean(o * o, axis=-1, keepdims=True)
            yb = o * lax.rsqrt(ms + RMS_EPS) * nw * _silu(zc)
            y_ref[b, rows, SCONV_WIDTH + h * Dh:SCONV_WIDTH + (h + 1) * Dh] = yb.astype(y_ref.dtype)
        while side_work:
            fill()

    assert CONV_ROWS == C
    for unit in conv_units(0):
        unit()

    def pipelined(c, carry):
        r0 = pl.multiple_of(c * C, C)
        chunk_step(r0, conv_units(r0 + C))
        return carry

    lax.fori_loop(0, ts // C - 1, pipelined, 0)
    chunk_step(ts - C, [])
    for buf in stage_bufs:
        for b in range(nbatch):
            buf[:, b * pitch:b * pitch + HALO, :] = buf[:, b * pitch + ts:b * pitch + ts + HALO, :]


def _mixer(proj, ab, sconv_w, conv_w, gate_p, norm_w, batch, seq, ts=256):
    def rows(cb, width):
        return pl.BlockSpec((batch, ts, width), lambda s, cb=cb: (0, s, cb))

    def whole(a):
        return pl.BlockSpec(a.shape, lambda s: (0,) * a.ndim)

    width_out = SCONV_WIDTH + GDN_VALUE_DIM
    return pl.pallas_call(
        _mixer_kernel,
        grid=(seq // ts,),
        in_specs=[
            rows(COL_BG // SCONV_WIDTH, SCONV_WIDTH),
            rows(COL_CG // SCONV_WIDTH, SCONV_WIDTH),
            rows(COL_XA // SCONV_WIDTH, SCONV_WIDTH),
            rows(COL_Q // GDN_KEY_DIM, GDN_KEY_DIM),
            rows(COL_K // GDN_KEY_DIM, GDN_KEY_DIM),
            rows(COL_V // GDN_VALUE_DIM, GDN_VALUE_DIM),
            rows(COL_Z // GDN_VALUE_DIM, GDN_VALUE_DIM),
            rows(0, LANES),
            whole(sconv_w), whole(conv_w), whole(gate_p), whole(norm_w),
        ],
        out_specs=pl.BlockSpec((batch, ts, width_out), lambda s: (0, s, 0)),
        out_shape=jax.ShapeDtypeStruct((batch, seq, width_out), BF16),
        scratch_shapes=[
            pltpu.VMEM((SCONV_WIDTH // LANES, batch * (ts + HALO), LANES), F32),
            pltpu.VMEM((GDN_KEY_DIM // LANES, batch * (ts + HALO), LANES), F32),
            pltpu.VMEM((GDN_KEY_DIM // LANES, batch * (ts + HALO), LANES), F32),
            pltpu.VMEM((GDN_VALUE_DIM // LANES, batch * (ts + HALO), LANES), F32),
            pltpu.VMEM((batch, ts, GDN_KEY_DIM), F32),
            pltpu.VMEM((batch, ts, GDN_KEY_DIM), F32),
            pltpu.VMEM((batch, ts, GDN_VALUE_DIM), F32),
            pltpu.VMEM((batch, ts, LANES), F32),
            pltpu.VMEM((batch, ts, LANES), F32),
            pltpu.VMEM((batch, GDN_V_HEADS, GDN_HEAD_DIM, GDN_HEAD_DIM), F32),
        ],
        compiler_params=_params(("arbitrary",)),
        name="mixer",
    )(proj, proj, proj, proj, proj, proj, proj, ab, sconv_w, conv_w, gate_p, norm_w)


def _outproj_kernel(alpha, y_ref, x_ref, g0_ref, b0_ref, w_ref, g1_ref, b1_ref, o_ref):
    h0 = _layer_norm(x_ref[...], g0_ref[...], b0_ref[...])
    r = alpha * h0 + _dot(y_ref[...], w_ref[...])
    o_ref[...] = _layer_norm(r, g1_ref[...], b1_ref[...]).astype(o_ref.dtype)


def _outproj(y, x2d, g0, b0, w, g1, b1, alpha, tm=512):
    T, D = x2d.shape
    vec = pl.BlockSpec((1, D), lambda i: (0, 0))
    return pl.pallas_call(
        functools.partial(_outproj_kernel, alpha),
        grid=(T // tm,),
        in_specs=[
            pl.BlockSpec((tm, y.shape[1]), lambda i: (i, 0)),
            pl.BlockSpec((tm, D), lambda i: (i, 0)),
            vec, vec,
            pl.BlockSpec(w.shape, lambda i: (0, 0)),
            vec, vec,
        ],
        out_specs=pl.BlockSpec((tm, D), lambda i: (i, 0)),
        out_shape=jax.ShapeDtypeStruct((T, D), BF16),
        compiler_params=_params(("parallel",)),
        name="outproj_ln1",
    )(y, x2d, g0, b0, w, g1, b1)


def _matmul_kernel(x_ref, w_ref, o_ref):
    o_ref[...] = _dot(x_ref[...].astype(BF16), w_ref[...]).astype(o_ref.dtype)


def _matmul(x, w, tn=1024):
    M, K = x.shape
    N = w.shape[1]
    return pl.pallas_call(
        _matmul_kernel,
        grid=(N // tn,),
        in_specs=[pl.BlockSpec((M, K), lambda j: (0, 0)), pl.BlockSpec((K, tn), lambda j: (0, j))],
        out_specs=pl.BlockSpec((M, tn), lambda j: (0, j)),
        out_shape=jax.ShapeDtypeStruct((M, N), BF16),
        compiler_params=_params(("parallel",)),
        name="kvproj",
    )(x, w)


def _xattn_kernel(alpha, h_ref, wq_ref, k_ref, v_ref, wo_ref, g_ref, b_ref, wr_hi_ref, wr_lo_ref,
                  o_ref, lg_ref, acc_ref):
    D = h_ref.shape[1]
    hd = D // XA_HEADS
    h1 = h_ref[...]
    acc_ref[...] = alpha * h1.astype(F32)
    for hh in range(XA_HEADS):
        sl = slice(hh * hd, (hh + 1) * hd)
        qh = _dot(h1, wq_ref[:, sl]).astype(BF16)
        s = lax.dot_general(qh, k_ref[:, sl], NT_DIMS, preferred_element_type=F32) * (hd ** -0.5)
        p = jnp.exp(s - jnp.max(s, axis=-1, keepdims=True))
        p = p / jnp.sum(p, axis=-1, keepdims=True)
        oh = _dot(p.astype(BF16), v_ref[:, sl]).astype(BF16)
        acc_ref[...] += _dot(oh, wo_ref[sl, :])
    h2 = _layer_norm(acc_ref[...], g_ref[...], b_ref[...])
    o_ref[...] = h2
    hi, lo = _split_bf16(h2)
    lg_ref[...] = _dot_split(hi, lo, wr_hi_ref[...], wr_lo_ref[...])


def _xattn(h1, wq, kv, wo, g, b, wr_hi, wr_lo, alpha, seq, n_mem, tm=512):
    T, D = h1.shape
    spb = seq // tm
    vec = pl.BlockSpec((1, D), lambda i: (0, 0))
    full = pl.BlockSpec((D, D), lambda i: (0, 0))
    return pl.pallas_call(
        functools.partial(_xattn_kernel, alpha),
        grid=(T // tm,),
        in_specs=[
            pl.BlockSpec((tm, D), lambda i: (i, 0)),
            full,
            pl.BlockSpec((n_mem, D), lambda i: (i // spb, 0)),
            pl.BlockSpec((n_mem, D), lambda i: (i // spb, 1)),
            full,
            vec, vec,
            pl.BlockSpec((D, LANES), lambda i: (0, 0)),
            pl.BlockSpec((D, LANES), lambda i: (0, 0)),
        ],
        out_specs=[
            pl.BlockSpec((tm, D), lambda i: (i, 0)),
            pl.BlockSpec((tm, LANES), lambda i: (i, 0)),
        ],
        out_shape=[
            jax.ShapeDtypeStruct((T, D), F32),
            jax.ShapeDtypeStruct((T, LANES), F32),
        ],
        scratch_shapes=[pltpu.VMEM((tm, D), F32)],
        compiler_params=_params(("parallel",)),
        name="xattn_ln2",
    )(h1, wq, kv, kv, wo, g, b, wr_hi, wr_lo)


def _route_kernel(lg_ref, o_ref, cnt_ref, carry):
    @pl.when(pl.program_id(0) == 0)
    def _():
        carry[...] = jnp.zeros(carry.shape, F32)

    lg = lg_ref[...]
    lane = lax.broadcasted_iota(jnp.int32, lg.shape, 1)
    neg = -jnp.inf
    big = jnp.int32(LANES)
    gl = jnp.where(lane < N_GROUPS, lg, neg)
    gmax = jnp.max(gl, axis=-1, keepdims=True)
    g_idx = jnp.min(jnp.where(gl == gmax, lane, big), axis=-1, keepdims=True)
    g_prob = 1.0 / jnp.sum(jnp.exp(gl - gmax), axis=-1, keepdims=True)
    e_lane = lane - N_GROUPS
    in_group = (e_lane >= g_idx * EXPERTS_PER_GROUP) & (e_lane < (g_idx + 1) * EXPERTS_PER_GROUP)
    el = jnp.where(in_group, lg, neg)
    m1 = jnp.max(el, axis=-1, keepdims=True)
    i1 = jnp.min(jnp.where(el == m1, lane, big), axis=-1, keepdims=True)
    denom = jnp.sum(jnp.exp(el - m1), axis=-1, keepdims=True)
    el2 = jnp.where(lane == i1, neg, el)
    m2 = jnp.max(el2, axis=-1, keepdims=True)
    i2 = jnp.min(jnp.where(el2 == m2, lane, big), axis=-1, keepdims=True)
    p1 = 1.0 / denom
    p2 = jnp.exp(m2 - m1) / denom
    psum = p1 + p2
    gate1 = g_prob * p1 / psum
    gate2 = g_prob * p2 / psum
    tm = lg.shape[0]
    onehot = jnp.where((lane == i1) | (lane == i2), 1.0, 0.0)
    ri = lax.broadcasted_iota(jnp.int32, (tm, tm), 0)
    ci = lax.broadcasted_iota(jnp.int32, (tm, tm), 1)
    earlier = jnp.where(ci < ri, 1.0, 0.0).astype(BF16)
    before = _dot(earlier, onehot.astype(BF16)) + carry[...]
    rank1 = jnp.sum(jnp.where(lane == i1, before, 0.0), axis=-1, keepdims=True)
    rank2 = jnp.sum(jnp.where(lane == i2, before, 0.0), axis=-1, keepdims=True)
    carry[...] += jnp.sum(onehot, axis=0, keepdims=True)
    cnt_ref[...] = carry[...]
    cols = ((i1 - N_GROUPS).astype(F32), (i2 - N_GROUPS).astype(F32), gate1, gate2, rank1, rank2)
    out = jnp.zeros(lg.shape, F32)
    for n, val in enumerate(cols):
        out = jnp.where(lane == n, val, out)
    o_ref[...] = out


def _route(logits, tm=512):
    T = logits.shape[0]
    spec = pl.BlockSpec((tm, LANES), lambda i: (i, 0))
    return pl.pallas_call(
        _route_kernel,
        grid=(T // tm,),
        in_specs=[spec],
        out_specs=[spec, pl.BlockSpec((1, LANES), lambda i: (0, 0))],
        out_shape=[jax.ShapeDtypeStruct((T, LANES), F32), jax.ShapeDtypeStruct((1, LANES), F32)],
        scratch_shapes=[pltpu.VMEM((1, LANES), F32)],
        compiler_params=_params(("arbitrary",)),
        name="route",
    )(logits)


def _wait_rows(buf, sem):
    pltpu.make_async_copy(buf, buf, sem).wait()


def _expert_kernel(bexp_ref, nval_ref, off_ref, first_ref, wslot_ref, nexte_ref, order_ref,
                   h_hbm, w1_hbm, w3_hbm, w2_hbm,
                   y_ref, xbuf0, xbuf1, xbuf2, w1s, w3s, w2s, w1b, w3b, w2b, gsem, wsem):
    i = pl.program_id(0)
    last = pl.num_programs(0) - 1
    xbuf = (xbuf0, xbuf1, xbuf2)
    nbuf = len(xbuf)
    bm = xbuf0.shape[0]
    slot = lax.rem(i, nbuf)
    nv = nval_ref[i]
    nv_prev = jnp.where(i > 0, nval_ref[jnp.maximum(i - 1, 0)], 0)

    def gather_row(base, r, k, priority=0):
        tok = lax.shift_right_logical(order_ref[base + r], 1)
        pltpu.make_async_copy(h_hbm.at[pl.ds(tok, 1), :], xbuf[k].at[pl.ds(r, 1), :],
                              gsem.at[k]).start(priority)

    def for_slot(fn):
        for k in range(nbuf):
            pl.when(slot == k)(functools.partial(fn, k))

    def weight_copies(e, s):
        return (pltpu.make_async_copy(w1_hbm.at[e], w1s.at[s], wsem.at[s]),
                pltpu.make_async_copy(w3_hbm.at[e], w3s.at[s], wsem.at[s]),
                pltpu.make_async_copy(w2_hbm.at[e], w2s.at[s], wsem.at[s]))

    @pl.when(i == 0)
    def _():
        for c in weight_copies(bexp_ref[0], 0):
            c.start(priority=1)

        def body(r, c):
            gather_row(off_ref[0], r, 0)
            gather_row(off_ref[1], r, 1)
            return c
        lax.fori_loop(0, bm, body, 0)

    @pl.when(nv > 0)
    def _():
        @pl.when(first_ref[i] == 1)
        def _():
            ws = wslot_ref[i]
            nxt = nexte_ref[i]

            @pl.when(nxt >= 0)
            def _():
                for c in weight_copies(nxt, 1 - ws):
                    c.start(priority=1)

            def cast(s):
                for c in weight_copies(0, s):
                    c.wait()
                w1b[...] = w1s[s].astype(BF16)
                w3b[...] = w3s[s].astype(BF16)
                w2b[...] = w2s[s].astype(BF16)

            for s in range(2):
                pl.when(ws == s)(functools.partial(cast, s))

        ahead_base = off_ref[jnp.minimum(i + 2, last)]

        def compute(cur):
            _wait_rows(xbuf[cur], gsem.at[cur])
            for r in range(bm):
                gather_row(ahead_base, r, (cur + 2) % nbuf)
            xb = xbuf[cur][...].astype(BF16)
            hid = _silu(_dot(xb, w1b[...])) * _dot(xb, w3b[...])
            y_ref[...] = _dot(hid.astype(BF16), w2b[...])

        for_slot(compute)

    @pl.when(nv == 0)
    def _():
        y_ref[...] = jnp.zeros(y_ref.shape, y_ref.dtype)

        @pl.when(nv_prev > 0)
        def _():
            def drain(cur):
                _wait_rows(xbuf[cur], gsem.at[cur])
                _wait_rows(xbuf[(cur + 1) % nbuf], gsem.at[(cur + 1) % nbuf])
            for_slot(drain)


def _experts(h2, w1, w3, w2, tables, bm):
    bexp, nval, off, first, wslot, nexte, order = tables
    D = h2.shape[1]
    nb = bexp.shape[0]
    ff = w1.shape[2]
    hbm = pl.BlockSpec(memory_space=pl.ANY)
    grid_spec = pltpu.PrefetchScalarGridSpec(
        num_scalar_prefetch=len(tables),
        grid=(nb,),
        in_specs=[hbm, hbm, hbm, hbm],
        out_specs=pl.BlockSpec((bm, D), lambda i, *tables: (i, 0)),
        scratch_shapes=[
            pltpu.VMEM((bm, D), F32),
            pltpu.VMEM((bm, D), F32),
            pltpu.VMEM((bm, D), F32),
            pltpu.VMEM((2, D, ff), F32),
            pltpu.VMEM((2, D, ff), F32),
            pltpu.VMEM((2, ff, D), F32),
            pltpu.VMEM((D, ff), BF16),
            pltpu.VMEM((D, ff), BF16),
            pltpu.VMEM((ff, D), BF16),
            pltpu.SemaphoreType.DMA((3,)),
            pltpu.SemaphoreType.DMA((2,)),
        ],
    )
    return pl.pallas_call(
        _expert_kernel,
        grid_spec=grid_spec,
        out_shape=jax.ShapeDtypeStruct((nb * bm, D), F32),
        compiler_params=_params(("arbitrary",)),
        name="experts",
    )(*tables, h2, w1, w3, w2)


def _combine_kernel(alpha, pos_ref, h_ref, rt_ref, g_ref, b_ref, y_hbm, o_ref, ybuf0, ybuf1, ybuf2, gsem):
    i = pl.program_id(0)
    nt = pl.num_programs(0)
    K = EXPERT_TOP_K
    tm = h_ref.shape[0]
    ybuf = (ybuf0, ybuf1, ybuf2)
    nbuf = len(ybuf)
    slot = lax.rem(i, nbuf)

    def gather_row(tile, r, k, buf, priority=0):
        src = pos_ref[(tile * tm + r) * K + k]
        pltpu.make_async_copy(y_hbm.at[pl.ds(src, 1), :], ybuf[buf].at[k, pl.ds(r, 1), :],
                              gsem.at[buf]).start(priority)

    @pl.when(i == 0)
    def _():
        def body(r, c):
            for k in range(K):
                gather_row(0, r, k, 0)
                gather_row(1, r, k, 1)
            return c
        lax.fori_loop(0, tm, body, 0)

    ahead_tile = jnp.minimum(i + 2, nt - 1)
    rt = rt_ref[...]

    def compute(cur):
        _wait_rows(ybuf[cur], gsem.at[cur])
        for r in range(tm):
            for k in range(K):
                gather_row(ahead_tile, r, k, (cur + 2) % nbuf, k % 2)
        ffn = ybuf[cur][0] * rt[:, 2:3] + ybuf[cur][1] * rt[:, 3:4]
        o_ref[...] = _layer_norm(alpha * h_ref[...] + ffn, g_ref[...], b_ref[...])

        @pl.when(i == nt - 1)
        def _():
            _wait_rows(ybuf[(cur + 1) % nbuf], gsem.at[(cur + 1) % nbuf])
            _wait_rows(ybuf[(cur + 2) % nbuf], gsem.at[(cur + 2) % nbuf])

    for k in range(nbuf):
        pl.when(slot == k)(functools.partial(compute, k))


def _combine(h2, y_sorted, pos, route, g, b, alpha, tm=256):
    T, D = h2.shape
    vec = pl.BlockSpec((1, D), lambda i, ps: (0, 0))
    grid_spec = pltpu.PrefetchScalarGridSpec(
        num_scalar_prefetch=1,
        grid=(T // tm,),
        in_specs=[
            pl.BlockSpec((tm, D), lambda i, ps: (i, 0)),
            pl.BlockSpec((tm, LANES), lambda i, ps: (i, 0)),
            vec, vec,
            pl.BlockSpec(memory_space=pl.ANY),
        ],
        out_specs=pl.BlockSpec((tm, D), lambda i, ps: (i, 0)),
        scratch_shapes=[
            pltpu.VMEM((EXPERT_TOP_K, tm, D), F32),
            pltpu.VMEM((EXPERT_TOP_K, tm, D), F32),
            pltpu.VMEM((EXPERT_TOP_K, tm, D), F32),
            pltpu.SemaphoreType.DMA((3,)),
        ],
    )
    return pl.pallas_call(
        functools.partial(_combine_kernel, alpha),
        grid_spec=grid_spec,
        out_shape=jax.ShapeDtypeStruct((T, D), F32),
        compiler_params=_params(("arbitrary",)),
        name="combine_ln3",
    )(pos, h2, route, g, b, y_sorted)


def _routing_tables(route, lane_counts, n_tok, bm):
    K = EXPERT_TOP_K
    TK = n_tok * K
    flat_e = route[:, 0:K].astype(jnp.int32).reshape(-1)
    flat_rank = route[:, 2 * K:3 * K].astype(jnp.int32).reshape(-1)
    order = jnp.argsort(flat_e).astype(jnp.int32)
    experts = jnp.arange(N_EXPERTS, dtype=jnp.int32)
    counts = lane_counts[0, N_GROUPS:N_GROUPS + N_EXPERTS].astype(jnp.int32)
    start = jnp.cumsum(counts) - counts
    padded = (counts + bm - 1) // bm * bm
    padded_end = jnp.cumsum(padded)
    padded_start = padded_end - padded
    nb = TK // bm + N_EXPERTS
    block_start = jnp.arange(nb, dtype=jnp.int32) * bm
    bexp = jnp.minimum(jnp.sum((block_start[:, None] >= padded_end[None, :]).astype(jnp.int32), axis=1),
                       N_EXPERTS - 1)
    of_block = bexp[:, None] == experts[None, :]

    def pick(per_expert):
        return jnp.sum(jnp.where(of_block, per_expert[None, :], 0), axis=1)

    rank0 = block_start - pick(padded_start)
    nval = jnp.clip(pick(counts) - rank0, 0, bm).astype(jnp.int32)
    off = jnp.clip(pick(start) + rank0, 0, TK).astype(jnp.int32)
    order = jnp.concatenate([order, jnp.zeros((bm,), jnp.int32)])
    pos = flat_rank + jnp.sum(jnp.where(flat_e[:, None] == experts[None, :], padded_start[None, :], 0), axis=1)
    bexp = bexp.astype(jnp.int32)
    active = counts > 0
    ordinal = jnp.cumsum(active.astype(jnp.int32)) - 1
    later_active = (experts[None, :] > experts[:, None]) & active[None, :]
    next_active = jnp.min(jnp.where(later_active, experts[None, :], N_EXPERTS), axis=1)
    next_active = jnp.where(next_active == N_EXPERTS, -1, next_active)
    prev_bexp = jnp.concatenate([jnp.full((1,), -1, jnp.int32), bexp[:-1]])
    first = ((nval > 0) & (bexp != prev_bexp)).astype(jnp.int32)
    wslot = (pick(ordinal) % 2).astype(jnp.int32)
    nexte = pick(next_active).astype(jnp.int32)
    return (bexp, nval, off, first, wslot, nexte, order), pos.astype(jnp.int32)


def kernel(x, mem, ln0_g, ln0_b, w_in, sconv_w, gdn_conv_w, gdn_a_log, gdn_dt_bias, gdn_norm_w,
           w_mix_out, ln1_g, ln1_b, xa_wq, xa_wk, xa_wv, xa_wo, ln2_g, ln2_b, w_group,
           w_expert_router, w1, w3, w2, ln3_g, ln3_b):
    B, S, D = x.shape
    T = B * S
    depth = w_in.shape[0]
    assert depth == 1, "single-layer stack: the out-projection kernel recomputes LayerNorm0(x)"
    alpha = float((2 * depth) ** 0.25)
    n_mem = mem.shape[1]
    bm = 256

    def vec(a):
        return a.reshape(1, -1).astype(F32)

    def pad_lanes(a):
        return jnp.pad(a, ((0, 0), (0, LANES - a.shape[1])))

    x2d = x.reshape(T, D)
    g0, b0 = vec(ln0_g), vec(ln0_b)
    l = 0
    w_main = w_in[l].astype(BF16)
    wab_hi, wab_lo = _split_bf16(pad_lanes(w_in[l][:, COL_AB:]))
    proj, ab = _ln_inproj(x2d, g0, b0, w_main, wab_hi, wab_lo)
    gate_p = jnp.pad(jnp.stack([gdn_a_log[l], gdn_dt_bias[l]]).astype(F32),
                     ((0, HALO - 2), (0, LANES - GDN_V_HEADS)))
    y = _mixer(proj.reshape(B, S, -1), ab.reshape(B, S, LANES), sconv_w[l].astype(F32),
               gdn_conv_w[l].astype(F32), gate_p, vec(gdn_norm_w[l]), B, S)
    h1 = _outproj(y.reshape(T, -1), x2d, g0, b0, w_mix_out[l].astype(BF16), vec(ln1_g[l]), vec(ln1_b[l]), alpha)
    kv = _matmul(mem.reshape(B * n_mem, D),
                 jnp.concatenate([xa_wk[l], xa_wv[l]], axis=1).astype(BF16))
    wr_hi, wr_lo = _split_bf16(pad_lanes(jnp.concatenate([w_group[l], w_expert_router[l]], axis=1)))
    h2, logits = _xattn(h1, xa_wq[l].astype(BF16), kv, xa_wo[l].astype(BF16), vec(ln2_g[l]),
                        vec(ln2_b[l]), wr_hi, wr_lo, alpha, S, n_mem)
    route, lane_counts = _route(logits)
    tables, pos = _routing_tables(route, lane_counts, T, bm)
    y_sorted = _experts(h2, w1[l], w3[l], w2[l], tables, bm)
    out = _combine(h2, y_sorted, pos, route, vec(ln3_g[l]), vec(ln3_b[l]), alpha)
    return out.reshape(B, S, D)
```

```python
import functools

import jax
import jax.numpy as jnp
from jax import lax
from jax.experimental import pallas as pl
from jax.experimental.pallas import tpu as pltpu

F32 = jnp.float32
BF16 = jnp.bfloat16

LN_EPS = 1e-5
RMS_EPS = 1e-6
L2_EPS = 1e-6

SCONV_WIDTH = 1024
SCONV_K = 3
GDN_HEAD_DIM = 128
GDN_V_HEADS = 8
GDN_K_HEADS = 4
GDN_KEY_DIM = GDN_K_HEADS * GDN_HEAD_DIM
GDN_VALUE_DIM = GDN_V_HEADS * GDN_HEAD_DIM
GDN_CONV_K = 4
GDN_CHUNK = 64
XA_HEADS = 4
N_GROUPS = 8
EXPERTS_PER_GROUP = 8
N_EXPERTS = N_GROUPS * EXPERTS_PER_GROUP
EXPERT_TOP_K = 2

LANES = 128
HALO = 8
VMEM_LIMIT = 56 * 1024 * 1024

COL_BG = 0
COL_CG = SCONV_WIDTH
COL_XA = 2 * SCONV_WIDTH
COL_Q = 3 * SCONV_WIDTH
COL_K = COL_Q + GDN_KEY_DIM
COL_V = COL_K + GDN_KEY_DIM
COL_Z = COL_V + GDN_VALUE_DIM
COL_AB = COL_Z + GDN_VALUE_DIM

NT_DIMS = (((1,), (1,)), ((), ()))
TN_DIMS = (((0,), (0,)), ((), ()))


def _dot(a, b):
    return jnp.dot(a, b, preferred_element_type=F32)


def _split_bf16(x):
    hi = x.astype(BF16)
    lo = (x - hi.astype(F32)).astype(BF16)
    return hi, lo


def _dot_split(x_hi, x_lo, w_hi, w_lo):
    n = w_hi.shape[1]
    both = _dot(x_hi, jnp.concatenate([w_hi, w_lo], axis=1))
    return both[:, :n] + both[:, n:] + _dot(x_lo, w_hi)


def _layer_norm(x, g, b):
    mu = jnp.mean(x, axis=-1, keepdims=True)
    xc = x - mu
    var = jnp.mean(xc * xc, axis=-1, keepdims=True)
    return xc * lax.rsqrt(var + LN_EPS) * g + b


def _sigmoid(x):
    return 1.0 / (1.0 + jnp.exp(-x))


def _silu(x):
    return x * _sigmoid(x)


def _params(sem):
    return pltpu.CompilerParams(dimension_semantics=sem, vmem_limit_bytes=VMEM_LIMIT)


def _ln_inproj_kernel(x_ref, g_ref, b_ref, w_ref, wab_hi_ref, wab_lo_ref, o_ref, ab_ref, hn_ref):
    @pl.when(pl.program_id(1) == 0)
    def _():
        h = _layer_norm(x_ref[...], g_ref[...], b_ref[...])
        hi, lo = _split_bf16(h)
        hn_ref[...] = hi
        ab_ref[...] = _dot_split(hi, lo, wab_hi_ref[...], wab_lo_ref[...])

    o_ref[...] = _dot(hn_ref[...], w_ref[...]).astype(o_ref.dtype)


def _ln_inproj(x2d, g, b, w_main, wab_hi, wab_lo, tm=1024, tn=1024):
    T, D = x2d.shape
    N = COL_AB
    return pl.pallas_call(
        _ln_inproj_kernel,
        grid=(T // tm, N // tn),
        in_specs=[
            pl.BlockSpec((tm, D), lambda i, j: (i, 0)),
            pl.BlockSpec((1, D), lambda i, j: (0, 0)),
            pl.BlockSpec((1, D), lambda i, j: (0, 0)),
            pl.BlockSpec((D, tn), lambda i, j: (0, j)),
            pl.BlockSpec((D, LANES), lambda i, j: (0, 0)),
            pl.BlockSpec((D, LANES), lambda i, j: (0, 0)),
        ],
        out_specs=[
            pl.BlockSpec((tm, tn), lambda i, j: (i, j)),
            pl.BlockSpec((tm, LANES), lambda i, j: (i, 0)),
        ],
        out_shape=[
            jax.ShapeDtypeStruct((T, N), BF16),
            jax.ShapeDtypeStruct((T, LANES), F32),
        ],
        scratch_shapes=[pltpu.VMEM((tm, D), BF16)],
        compiler_params=_params(("parallel", "arbitrary")),
        name="ln_inproj",
    )(x2d, g, b, w_main, wab_hi, wab_lo)


CONV_ROWS = 64


def _conv_taps(buf, base, lanes, w_ref, w_col0, ksize, r0):
    w_lanes = slice(w_col0 + lanes.start, w_col0 + lanes.stop)
    acc = None
    for j in range(ksize):
        start = base + HALO - ksize + 1 + j + r0
        term = w_ref[j:j + 1, w_lanes] * buf[lanes.start // LANES, pl.ds(start, CONV_ROWS), :]
        acc = term if acc is None else acc + term
    return acc


def _mixer_kernel(bg_ref, cg_ref, xa_ref, q_ref, k_ref, v_ref, z_ref, ab_ref,
                  sw_ref, cw_ref, gp_ref, nw_ref,
                  y_ref,
                  sbuf, qbuf, kbuf, vbuf, qn, kn, vc, g_s, b_s, state):
    nbatch, ts = q_ref.shape[0], q_ref.shape[1]
    C = GDN_CHUNK
    Dh = GDN_HEAD_DIM
    stage_bufs = (sbuf, qbuf, kbuf, vbuf)
    pitch = ts + HALO

    @pl.when(pl.program_id(0) == 0)
    def _():
        for buf in stage_bufs:
            for b in range(nbatch):
                buf[:, b * pitch:b * pitch + HALO, :] = jnp.zeros((buf.shape[0], HALO, LANES), F32)
        state[...] = jnp.zeros(state.shape, F32)

    row = lax.broadcasted_iota(jnp.int32, (ts, ts), 0)
    col = lax.broadcasted_iota(jnp.int32, (ts, ts), 1)
    tri = jnp.where((row // C == col // C) & (col <= row), 1.0, 0.0).astype(BF16)

    for b in range(nbatch):
        ab = ab_ref[b]
        xg = ab + gp_ref[1:2, :]
        softplus = jnp.maximum(xg, 0.0) + jnp.log1p(jnp.exp(-jnp.abs(xg)))
        g = -jnp.exp(gp_ref[0:1, :]) * softplus
        b_s[b] = _sigmoid(pltpu.roll(ab, LANES - GDN_V_HEADS, axis=1))
        g1 = g.astype(BF16)
        r1 = g - g1.astype(F32)
        g2 = r1.astype(BF16)
        g3 = (r1 - g2.astype(F32)).astype(BF16)
        g_s[b] = _dot(tri, g1) + _dot(tri, g2) + _dot(tri, g3)

    def conv_units(r0):
        rows = pl.ds(r0, CONV_ROWS)

        def srows(b):
            return pl.ds(b * pitch + HALO + r0, CONV_ROWS)

        def sconv_unit(b, ls):
            sbuf[ls.start // LANES, srows(b), :] = cg_ref[b, rows, ls].astype(F32) * xa_ref[b, rows, ls].astype(F32)
            conv = _conv_taps(sbuf, b * pitch, ls, sw_ref, 0, SCONV_K, r0)
            y_ref[b, rows, ls] = (bg_ref[b, rows, ls].astype(F32) * conv).astype(y_ref.dtype)

        def qk_unit(b, ls):
            qbuf[ls.start // LANES, srows(b), :] = q_ref[b, rows, ls].astype(F32)
            kbuf[ls.start // LANES, srows(b), :] = k_ref[b, rows, ls].astype(F32)
            qa = _silu(_conv_taps(qbuf, b * pitch, ls, cw_ref, 0, GDN_CONV_K, r0))
            ka = _silu(_conv_taps(kbuf, b * pitch, ls, cw_ref, GDN_KEY_DIM, GDN_CONV_K, r0))
            qn[b, rows, ls] = qa * (lax.rsqrt(jnp.sum(qa * qa, axis=-1, keepdims=True) + L2_EPS) * (Dh ** -0.5))
            kn[b, rows, ls] = ka * lax.rsqrt(jnp.sum(ka * ka, axis=-1, keepdims=True) + L2_EPS)

        def v_unit(b, ls):
            vbuf[ls.start // LANES, srows(b), :] = v_ref[b, rows, ls].astype(F32)
            vc[b, rows, ls] = _silu(_conv_taps(vbuf, b * pitch, ls, cw_ref, 2 * GDN_KEY_DIM, GDN_CONV_K, r0))

        units = []
        for b in range(nbatch):
            for n in range(SCONV_WIDTH // LANES):
                units.append(functools.partial(sconv_unit, b, slice(n * LANES, (n + 1) * LANES)))
            for n in range(GDN_K_HEADS):
                units.append(functools.partial(qk_unit, b, slice(n * Dh, (n + 1) * Dh)))
            for n in range(GDN_V_HEADS):
                units.append(functools.partial(v_unit, b, slice(n * Dh, (n + 1) * Dh)))
        return units
    ri = lax.broadcasted_iota(jnp.int32, (C, C), 0)
    ci = lax.broadcasted_iota(jnp.int32, (C, C), 1)
    causal = ci <= ri
    strict = ci < ri
    eye = jnp.where(ri == ci, 1.0, 0.0).astype(F32)
    nw = nw_ref[...]

    def chunk_step(r0, side_work):
        rows = pl.ds(r0, C)
        side_work = list(side_work)
        per_stage = -(-len(side_work) // 9)

        def fill():
            for _ in range(min(per_stage, len(side_work))):
                side_work.pop(0)()

        batches = range(nbatch)
        chains = [(b, h) for b in batches for h in range(GDN_V_HEADS)]
        kchains = [(b, kh) for b in batches for kh in range(GDN_K_HEADS)]

        def kidx(b, h):
            return b * GDN_K_HEADS + h // 2

        gc, gct, bc, eg, e_tail, e_last = [], [], [], [], [], []
        for b in batches:
            gc.append(g_s[b, rows, :])
            gct.append(gc[b].T)
            bc.append(b_s[b, rows, :])
            eg.append(jnp.exp(gc[b]))
            g_last = g_s[b, pl.ds(r0 + C - 1, 1), :]
            e_tail.append(jnp.exp(g_last - gc[b]))
            e_last.append(jnp.exp(g_last))
        qc, kc, kcb, kk, qk = [], [], [], [], []
        for b, kh in kchains:
            ksl = slice(kh * Dh, (kh + 1) * Dh)
            qc.append(qn[b, rows, ksl])
            kc.append(kn[b, rows, ksl])
            kcb.append(kc[-1].astype(BF16))
        for n in range(len(kchains)):
            kk.append(lax.dot_general(kcb[n], kcb[n], NT_DIMS, preferred_element_type=F32))
            qk.append(lax.dot_general(qc[n].astype(BF16), kcb[n], NT_DIMS, preferred_element_type=F32))
        fill()
        decay, bcol, egc, inv, pw = [], [], [], [], []
        for b, h in chains:
            decay.append(jnp.exp(jnp.where(causal, gc[b][:, h:h + 1] - gct[b][h:h + 1, :], -1e30)))
            bcol.append(bc[b][:, h:h + 1])
            egc.append(eg[b][:, h:h + 1])
        for n, (b, h) in enumerate(chains):
            low = jnp.where(strict, kk[kidx(b, h)] * bcol[n] * decay[n], 0.0)
            lb = low.astype(BF16)
            inv.append(eye - low)
            pw.append(_dot(lb, lb))
        fill()
        for it in range(5):
            for n in range(len(chains)):
                pwb = pw[n].astype(BF16)
                both = jnp.concatenate([inv[n], pw[n]], axis=0) if it < 4 else inv[n]
                prod = _dot(both.astype(BF16), pwb)
                inv[n] = inv[n] + prod[:C]
                if it < 4:
                    pw[n] = prod[C:]
            fill()
        u, w = [], []
        for n, (b, h) in enumerate(chains):
            vsl = slice(h * Dh, (h + 1) * Dh)
            rhs = jnp.concatenate([vc[b, rows, vsl] * bcol[n], kc[kidx(b, h)] * (bcol[n] * egc[n])], axis=1)
            sol = _dot(inv[n].astype(BF16), rhs.astype(BF16))
            u.append(sol[:, :Dh])
            w.append(sol[:, Dh:].astype(BF16))
        fill()
        s_old, sb, vnb = [], [], []
        for n, (b, h) in enumerate(chains):
            s_old.append(state[b, h])
            sb.append(s_old[n].astype(BF16))
            vnb.append((u[n] - _dot(w[n], sb[n])).astype(BF16))
        fill()
        for n, (b, h) in enumerate(chains):
            vsl = slice(h * Dh, (h + 1) * Dh)
            attn = (qk[kidx(b, h)] * decay[n]).astype(BF16)
            qd = (qc[kidx(b, h)] * egc[n]).astype(BF16)
            kt = (kc[kidx(b, h)] * e_tail[b][:, h:h + 1]).astype(BF16)
            o = _dot(qd, sb[n]) + _dot(attn, vnb[n])
            state[b, h] = s_old[n] * e_last[b][:, h:h + 1] + lax.dot_general(
                kt, vnb[n], TN_DIMS, preferred_element_type=F32)
            zc = z_ref[b, rows, vsl].astype(F32)
            ms = jnp.mean(o * o, axis=-1, keepdims=True)
            yb = o * lax.rsqrt(ms + RMS_EPS) * nw * _silu(zc)
            y_ref[b, rows, SCONV_WIDTH + h * Dh:SCONV_WIDTH + (h + 1) * Dh] = yb.astype(y_ref.dtype)
        while side_work:
            fill()

    assert CONV_ROWS == C
    for unit in conv_units(0):
        unit()

    def pipelined(c, carry):
        r0 = pl.multiple_of(c * C, C)
        chunk_step(r0, conv_units(r0 + C))
        return carry

    lax.fori_loop(0, ts // C - 1, pipelined, 0)
    chunk_step(ts - C, [])
    for buf in stage_bufs:
        for b in range(nbatch):
            buf[:, b * pitch:b * pitch + HALO, :] = buf[:, b * pitch + ts:b * pitch + ts + HALO, :]


def _mixer(proj, ab, sconv_w, conv_w, gate_p, norm_w, batch, seq, ts=256):
    def rows(cb, width):
        return pl.BlockSpec((batch, ts, width), lambda s, cb=cb: (0, s, cb))

    def whole(a):
        return pl.BlockSpec(a.shape, lambda s: (0,) * a.ndim)

    width_out = SCONV_WIDTH + GDN_VALUE_DIM
    return pl.pallas_call(
        _mixer_kernel,
        grid=(seq // ts,),
        in_specs=[
            rows(COL_BG // SCONV_WIDTH, SCONV_WIDTH),
            rows(COL_CG // SCONV_WIDTH, SCONV_WIDTH),
            rows(COL_XA // SCONV_WIDTH, SCONV_WIDTH),
            rows(COL_Q // GDN_KEY_DIM, GDN_KEY_DIM),
            rows(COL_K // GDN_KEY_DIM, GDN_KEY_DIM),
            rows(COL_V // GDN_VALUE_DIM, GDN_VALUE_DIM),
            rows(COL_Z // GDN_VALUE_DIM, GDN_VALUE_DIM),
            rows(0, LANES),
            whole(sconv_w), whole(conv_w), whole(gate_p), whole(norm_w),
        ],
        out_specs=pl.BlockSpec((batch, ts, width_out), lambda s: (0, s, 0)),
        out_shape=jax.ShapeDtypeStruct((batch, seq, width_out), BF16),
        scratch_shapes=[
            pltpu.VMEM((SCONV_WIDTH // LANES, batch * (ts + HALO), LANES), F32),
            pltpu.VMEM((GDN_KEY_DIM // LANES, batch * (ts + HALO), LANES), F32),
            pltpu.VMEM((GDN_KEY_DIM // LANES, batch * (ts + HALO), LANES), F32),
            pltpu.VMEM((GDN_VALUE_DIM // LANES, batch * (ts + HALO), LANES), F32),
            pltpu.VMEM((batch, ts, GDN_KEY_DIM), F32),
            pltpu.VMEM((batch, ts, GDN_KEY_DIM), F32),
            pltpu.VMEM((batch, ts, GDN_VALUE_DIM), F32),
            pltpu.VMEM((batch, ts, LANES), F32),
            pltpu.VMEM((batch, ts, LANES), F32),
            pltpu.VMEM((batch, GDN_V_HEADS, GDN_HEAD_DIM, GDN_HEAD_DIM), F32),
        ],
        compiler_params=_params(("arbitrary",)),
        name="mixer",
    )(proj, proj, proj, proj, proj, proj, proj, ab, sconv_w, conv_w, gate_p, norm_w)


def _outproj_kernel(alpha, y_ref, x_ref, g0_ref, b0_ref, w_ref, g1_ref, b1_ref, o_ref):
    h0 = _layer_norm(x_ref[...], g0_ref[...], b0_ref[...])
    r = alpha * h0 + _dot(y_ref[...], w_ref[...])
    o_ref[...] = _layer_norm(r, g1_ref[...], b1_ref[...]).astype(o_ref.dtype)


def _outproj(y, x2d, g0, b0, w, g1, b1, alpha, tm=512):
    T, D = x2d.shape
    vec = pl.BlockSpec((1, D), lambda i: (0, 0))
    return pl.pallas_call(
        functools.partial(_outproj_kernel, alpha),
        grid=(T // tm,),
        in_specs=[
            pl.BlockSpec((tm, y.shape[1]), lambda i: (i, 0)),
            pl.BlockSpec((tm, D), lambda i: (i, 0)),
            vec, vec,
            pl.BlockSpec(w.shape, lambda i: (0, 0)),
            vec, vec,
        ],
        out_specs=pl.BlockSpec((tm, D), lambda i: (i, 0)),
        out_shape=jax.ShapeDtypeStruct((T, D), BF16),
        compiler_params=_params(("parallel",)),
        name="outproj_ln1",
    )(y, x2d, g0, b0, w, g1, b1)


def _matmul_kernel(x_ref, w_ref, o_ref):
    o_ref[...] = _dot(x_ref[...].astype(BF16), w_ref[...]).astype(o_ref.dtype)


def _matmul(x, w, tn=1024):
    M, K = x.shape
    N = w.shape[1]
    return pl.pallas_call(
        _matmul_kernel,
        grid=(N // tn,),
        in_specs=[pl.BlockSpec((M, K), lambda j: (0, 0)), pl.BlockSpec((K, tn), lambda j: (0, j))],
        out_specs=pl.BlockSpec((M, tn), lambda j: (0, j)),
        out_shape=jax.ShapeDtypeStruct((M, N), BF16),
        compiler_params=_params(("parallel",)),
        name="kvproj",
    )(x, w)


def _xattn_kernel(alpha, h_ref, wq_ref, k_ref, v_ref, wo_ref, g_ref, b_ref, wr_hi_ref, wr_lo_ref,
                  o_ref, lg_ref, acc0, acc1):
    D = h_ref.shape[1]
    tm = h_ref.shape[0]
    hd = D // XA_HEADS
    tail_rows = tm // XA_HEADS
    step_id = pl.program_id(0)

    @pl.when(step_id == 0)
    def _():
        acc1[...] = jnp.zeros(acc1.shape, F32)

    def step(acc_new, acc_done):
        def tail(n):
            rows = slice(n * tail_rows, (n + 1) * tail_rows)
            h2 = _layer_norm(acc_done[rows, :], g_ref[...], b_ref[...])
            o_ref[rows, :] = h2
            hi, lo = _split_bf16(h2)
            lg_ref[rows, :] = _dot_split(hi, lo, wr_hi_ref[...], wr_lo_ref[...])

        h1 = h_ref[...]
        acc_new[...] = alpha * h1.astype(F32)
        for hh in range(XA_HEADS):
            sl = slice(hh * hd, (hh + 1) * hd)
            qh = _dot(h1, wq_ref[:, sl]).astype(BF16)
            s = lax.dot_general(qh, k_ref[:, sl], NT_DIMS, preferred_element_type=F32) * (hd ** -0.5)
            p = jnp.exp(s - jnp.max(s, axis=-1, keepdims=True))
            p = p / jnp.sum(p, axis=-1, keepdims=True)
            oh = _dot(p.astype(BF16), v_ref[:, sl]).astype(BF16)
            acc_new[...] += _dot(oh, wo_ref[sl, :])
            tail(hh)

    parity = lax.rem(step_id, 2)
    pl.when(parity == 0)(functools.partial(step, acc0, acc1))
    pl.when(parity == 1)(functools.partial(step, acc1, acc0))


def _xattn(h1, wq, kv, wo, g, b, wr_hi, wr_lo, alpha, seq, n_mem, tm=512):
    T, D = h1.shape
    nt = T // tm
    spb = seq // tm
    vec = pl.BlockSpec((1, D), lambda s: (0, 0))
    full = pl.BlockSpec((D, D), lambda s: (0, 0))

    def tile_in(s):
        return jnp.minimum(s, nt - 1)

    def tile_out(s):
        return jnp.where(s == 0, nt, s - 1)

    return pl.pallas_call(
        functools.partial(_xattn_kernel, alpha),
        grid=(nt + 1,),
        in_specs=[
            pl.BlockSpec((tm, D), lambda s: (tile_in(s), 0)),
            full,
            pl.BlockSpec((n_mem, D), lambda s: (tile_in(s) // spb, 0)),
            pl.BlockSpec((n_mem, D), lambda s: (tile_in(s) // spb, 1)),
            full,
            vec, vec,
            pl.BlockSpec((D, LANES), lambda s: (0, 0)),
            pl.BlockSpec((D, LANES), lambda s: (0, 0)),
        ],
        out_specs=[
            pl.BlockSpec((tm, D), lambda s: (tile_out(s), 0)),
            pl.BlockSpec((tm, LANES), lambda s: (tile_out(s), 0)),
        ],
        out_shape=[
            jax.ShapeDtypeStruct((T + tm, D), F32),
            jax.ShapeDtypeStruct((T + tm, LANES), F32),
        ],
        scratch_shapes=[pltpu.VMEM((tm, D), F32), pltpu.VMEM((tm, D), F32)],
        compiler_params=_params(("arbitrary",)),
        name="xattn_ln2",
    )(h1, wq, kv, kv, wo, g, b, wr_hi, wr_lo)


def _route_kernel(lg_ref, o_ref, cnt_ref, idx_ref, carry):
    @pl.when(pl.program_id(0) == 0)
    def _():
        carry[...] = jnp.zeros(carry.shape, F32)

    lg = lg_ref[...]
    lane = lax.broadcasted_iota(jnp.int32, lg.shape, 1)
    neg = -jnp.inf
    big = jnp.int32(LANES)
    gl = jnp.where(lane < N_GROUPS, lg, neg)
    gmax = jnp.max(gl, axis=-1, keepdims=True)
    g_idx = jnp.min(jnp.where(gl == gmax, lane, big), axis=-1, keepdims=True)
    g_prob = 1.0 / jnp.sum(jnp.exp(gl - gmax), axis=-1, keepdims=True)
    e_lane = lane - N_GROUPS
    in_group = (e_lane >= g_idx * EXPERTS_PER_GROUP) & (e_lane < (g_idx + 1) * EXPERTS_PER_GROUP)
    el = jnp.where(in_group, lg, neg)
    m1 = jnp.max(el, axis=-1, keepdims=True)
    i1 = jnp.min(jnp.where(el == m1, lane, big), axis=-1, keepdims=True)
    denom = jnp.sum(jnp.exp(el - m1), axis=-1, keepdims=True)
    el2 = jnp.where(lane == i1, neg, el)
    m2 = jnp.max(el2, axis=-1, keepdims=True)
    i2 = jnp.min(jnp.where(el2 == m2, lane, big), axis=-1, keepdims=True)
    p1 = 1.0 / denom
    p2 = jnp.exp(m2 - m1) / denom
    psum = p1 + p2
    gate1 = g_prob * p1 / psum
    gate2 = g_prob * p2 / psum
    tm = lg.shape[0]
    onehot = jnp.where((lane == i1) | (lane == i2), 1.0, 0.0)
    ri = lax.broadcasted_iota(jnp.int32, (tm, tm), 0)
    ci = lax.broadcasted_iota(jnp.int32, (tm, tm), 1)
    earlier = jnp.where(ci < ri, 1.0, 0.0).astype(BF16)
    before = _dot(earlier, onehot.astype(BF16)) + carry[...]
    rank1 = jnp.sum(jnp.where(lane == i1, before, 0.0), axis=-1, keepdims=True)
    rank2 = jnp.sum(jnp.where(lane == i2, before, 0.0), axis=-1, keepdims=True)
    carry[...] += jnp.sum(onehot, axis=0, keepdims=True)
    cnt_ref[...] = carry[...]
    cols = ((i1 - N_GROUPS).astype(F32), (i2 - N_GROUPS).astype(F32), gate1, gate2, rank1, rank2)
    out = jnp.zeros(lg.shape, F32)
    for n, val in enumerate(cols):
        out = jnp.where(lane == n, val, out)
    o_ref[...] = out
    idx_ref[...] = out[:, 0:HALO].astype(jnp.int32)


def _route(logits, T, tm=512):
    spec = pl.BlockSpec((tm, LANES), lambda i: (i, 0))
    return pl.pallas_call(
        _route_kernel,
        grid=(T // tm,),
        in_specs=[spec],
        out_specs=[spec, pl.BlockSpec((1, LANES), lambda i: (0, 0)), pl.BlockSpec((tm, HALO), lambda i: (i, 0))],
        out_shape=[jax.ShapeDtypeStruct((T, LANES), F32), jax.ShapeDtypeStruct((1, LANES), F32),
                   jax.ShapeDtypeStruct((T, HALO), jnp.int32)],
        scratch_shapes=[pltpu.VMEM((1, LANES), F32)],
        compiler_params=_params(("arbitrary",)),
        name="route",
    )(logits)


def _wait_rows(buf, sem):
    pltpu.make_async_copy(buf, buf, sem).wait()


def _expert_kernel(bexp_ref, nval_ref, off_ref, first_ref, wslot_ref, nexte_ref, order_ref,
                   h_hbm, w1_hbm, w3_hbm, w2_hbm,
                   y_ref, xbuf0, xbuf1, xbuf2, w1s, w3s, w2s, w1b, w3b, w2b, gsem, wsem):
    i = pl.program_id(0)
    last = pl.num_programs(0) - 1
    xbuf = (xbuf0, xbuf1, xbuf2)
    nbuf = len(xbuf)
    bm = xbuf0.shape[0]
    slot = lax.rem(i, nbuf)
    nv = nval_ref[i]
    nv_prev = jnp.where(i > 0, nval_ref[jnp.maximum(i - 1, 0)], 0)

    def gather_row(base, r, k, priority=0):
        tok = lax.shift_right_logical(order_ref[base + r], 1)
        pltpu.make_async_copy(h_hbm.at[pl.ds(tok, 1), :], xbuf[k].at[pl.ds(r, 1), :],
                              gsem.at[k]).start(priority)

    def for_slot(fn):
        for k in range(nbuf):
            pl.when(slot == k)(functools.partial(fn, k))

    def weight_copies(e, s):
        return (pltpu.make_async_copy(w1_hbm.at[e], w1s.at[s], wsem.at[s]),
                pltpu.make_async_copy(w3_hbm.at[e], w3s.at[s], wsem.at[s]),
                pltpu.make_async_copy(w2_hbm.at[e], w2s.at[s], wsem.at[s]))

    @pl.when(i == 0)
    def _():
        for c in weight_copies(bexp_ref[0], 0):
            c.start(priority=1)

        def body(r, c):
            gather_row(off_ref[0], r, 0)
            gather_row(off_ref[1], r, 1)
            return c
        lax.fori_loop(0, bm, body, 0)

    @pl.when(nv > 0)
    def _():
        @pl.when(first_ref[i] == 1)
        def _():
            ws = wslot_ref[i]
            nxt = nexte_ref[i]

            @pl.when(nxt >= 0)
            def _():
                for c in weight_copies(nxt, 1 - ws):
                    c.start(priority=1)

            def cast(s):
                for c in weight_copies(0, s):
                    c.wait()
                w1b[...] = w1s[s].astype(BF16)
                w3b[...] = w3s[s].astype(BF16)
                w2b[...] = w2s[s].astype(BF16)

            for s in range(2):
                pl.when(ws == s)(functools.partial(cast, s))

        ahead_base = off_ref[jnp.minimum(i + 2, last)]

        def compute(cur):
            _wait_rows(xbuf[cur], gsem.at[cur])
            for r in range(bm):
                gather_row(ahead_base, r, (cur + 2) % nbuf)
            xb = xbuf[cur][...].astype(BF16)
            hid = _silu(_dot(xb, w1b[...])) * _dot(xb, w3b[...])
            y_ref[...] = _dot(hid.astype(BF16), w2b[...])

        for_slot(compute)

    @pl.when(nv == 0)
    def _():
        y_ref[...] = jnp.zeros(y_ref.shape, y_ref.dtype)

        @pl.when(nv_prev > 0)
        def _():
            def drain(cur):
                _wait_rows(xbuf[cur], gsem.at[cur])
                _wait_rows(xbuf[(cur + 1) % nbuf], gsem.at[(cur + 1) % nbuf])
            for_slot(drain)


def _experts(h2, w1, w3, w2, tables, bm):
    bexp, nval, off, first, wslot, nexte, order = tables
    D = h2.shape[1]
    nb = bexp.shape[0]
    ff = w1.shape[2]
    hbm = pl.BlockSpec(memory_space=pl.ANY)
    grid_spec = pltpu.PrefetchScalarGridSpec(
        num_scalar_prefetch=len(tables),
        grid=(nb,),
        in_specs=[hbm, hbm, hbm, hbm],
        out_specs=pl.BlockSpec((bm, D), lambda i, *tables: (i, 0)),
        scratch_shapes=[
            pltpu.VMEM((bm, D), F32),
            pltpu.VMEM((bm, D), F32),
            pltpu.VMEM((bm, D), F32),
            pltpu.VMEM((2, D, ff), F32),
            pltpu.VMEM((2, D, ff), F32),
            pltpu.VMEM((2, ff, D), F32),
            pltpu.VMEM((D, ff), BF16),
            pltpu.VMEM((D, ff), BF16),
            pltpu.VMEM((ff, D), BF16),
            pltpu.SemaphoreType.DMA((3,)),
            pltpu.SemaphoreType.DMA((2,)),
        ],
    )
    return pl.pallas_call(
        _expert_kernel,
        grid_spec=grid_spec,
        out_shape=jax.ShapeDtypeStruct((nb * bm, D), F32),
        compiler_params=_params(("arbitrary",)),
        name="experts",
    )(*tables, h2, w1, w3, w2)


def _combine_kernel(alpha, pos_ref, h_ref, rt_ref, g_ref, b_ref, y_hbm, o_ref, ybuf0, ybuf1, ybuf2, gsem):
    i = pl.program_id(0)
    nt = pl.num_programs(0)
    K = EXPERT_TOP_K
    tm = h_ref.shape[0]
    ybuf = (ybuf0, ybuf1, ybuf2)
    nbuf = len(ybuf)
    slot = lax.rem(i, nbuf)

    def gather_row(tile, r, k, buf, priority=0):
        src = pos_ref[(tile * tm + r) * K + k]
        pltpu.make_async_copy(y_hbm.at[pl.ds(src, 1), :], ybuf[buf].at[k, pl.ds(r, 1), :],
                              gsem.at[buf]).start(priority)

    @pl.when(i == 0)
    def _():
        def body(r, c):
            for k in range(K):
                gather_row(0, r, k, 0)
                gather_row(1, r, k, 1)
            return c
        lax.fori_loop(0, tm, body, 0)

    ahead_tile = jnp.minimum(i + 2, nt - 1)
    rt = rt_ref[...]

    def compute(cur):
        _wait_rows(ybuf[cur], gsem.at[cur])
        for r in range(tm):
            for k in range(K):
                gather_row(ahead_tile, r, k, (cur + 2) % nbuf, k % 2)
        ffn = ybuf[cur][0] * rt[:, 2:3] + ybuf[cur][1] * rt[:, 3:4]
        o_ref[...] = _layer_norm(alpha * h_ref[...] + ffn, g_ref[...], b_ref[...])

        @pl.when(i == nt - 1)
        def _():
            _wait_rows(ybuf[(cur + 1) % nbuf], gsem.at[(cur + 1) % nbuf])
            _wait_rows(ybuf[(cur + 2) % nbuf], gsem.at[(cur + 2) % nbuf])

    for k in range(nbuf):
        pl.when(slot == k)(functools.partial(compute, k))


def _combine(h2, y_sorted, pos, route, g, b, alpha, tm=256):
    T, D = route.shape[0], h2.shape[1]
    vec = pl.BlockSpec((1, D), lambda i, ps: (0, 0))
    grid_spec = pltpu.PrefetchScalarGridSpec(
        num_scalar_prefetch=1,
        grid=(T // tm,),
        in_specs=[
            pl.BlockSpec((tm, D), lambda i, ps: (i, 0)),
            pl.BlockSpec((tm, LANES), lambda i, ps: (i, 0)),
            vec, vec,
            pl.BlockSpec(memory_space=pl.ANY),
        ],
        out_specs=pl.BlockSpec((tm, D), lambda i, ps: (i, 0)),
        scratch_shapes=[
            pltpu.VMEM((EXPERT_TOP_K, tm, D), F32),
            pltpu.VMEM((EXPERT_TOP_K, tm, D), F32),
            pltpu.VMEM((EXPERT_TOP_K, tm, D), F32),
            pltpu.SemaphoreType.DMA((3,)),
        ],
    )
    return pl.pallas_call(
        functools.partial(_combine_kernel, alpha),
        grid_spec=grid_spec,
        out_shape=jax.ShapeDtypeStruct((T, D), F32),
        compiler_params=_params(("arbitrary",)),
        name="combine_ln3",
    )(pos, h2, route, g, b, y_sorted)


def _routing_tables(route_idx, lane_counts, n_tok, bm):
    K = EXPERT_TOP_K
    TK = n_tok * K
    flat_e = route_idx[:, 0:K].reshape(-1)
    flat_rank = route_idx[:, 2 * K:3 * K].reshape(-1)
    order = jnp.argsort(flat_e).astype(jnp.int32)
    experts = jnp.arange(N_EXPERTS, dtype=jnp.int32)
    counts = lane_counts[0, N_GROUPS:N_GROUPS + N_EXPERTS].astype(jnp.int32)
    start = jnp.cumsum(counts) - counts
    padded = (counts + bm - 1) // bm * bm
    padded_end = jnp.cumsum(padded)
    padded_start = padded_end - padded
    nb = TK // bm + N_EXPERTS
    block_start = jnp.arange(nb, dtype=jnp.int32) * bm
    bexp = jnp.minimum(jnp.sum((block_start[:, None] >= padded_end[None, :]).astype(jnp.int32), axis=1),
                       N_EXPERTS - 1)
    of_block = bexp[:, None] == experts[None, :]

    def pick(per_expert):
        return jnp.sum(jnp.where(of_block, per_expert[None, :], 0), axis=1)

    rank0 = block_start - pick(padded_start)
    nval = jnp.clip(pick(counts) - rank0, 0, bm).astype(jnp.int32)
    off = jnp.clip(pick(start) + rank0, 0, TK).astype(jnp.int32)
    order = jnp.concatenate([order, jnp.zeros((bm,), jnp.int32)])
    pos = flat_rank + jnp.sum(jnp.where(flat_e[:, None] == experts[None, :], padded_start[None, :], 0), axis=1)
    bexp = bexp.astype(jnp.int32)
    active = counts > 0
    ordinal = jnp.cumsum(active.astype(jnp.int32)) - 1
    later_active = (experts[None, :] > experts[:, None]) & active[None, :]
    next_active = jnp.min(jnp.where(later_active, experts[None, :], N_EXPERTS), axis=1)
    next_active = jnp.where(next_active == N_EXPERTS, -1, next_active)
    prev_bexp = jnp.concatenate([jnp.full((1,), -1, jnp.int32), bexp[:-1]])
    first = ((nval > 0) & (bexp != prev_bexp)).astype(jnp.int32)
    wslot = (pick(ordinal) % 2).astype(jnp.int32)
    nexte = pick(next_active).astype(jnp.int32)
    return (bexp, nval, off, first, wslot, nexte, order), pos.astype(jnp.int32)


def kernel(x, mem, ln0_g, ln0_b, w_in, sconv_w, gdn_conv_w, gdn_a_log, gdn_dt_bias, gdn_norm_w,
           w_mix_out, ln1_g, ln1_b, xa_wq, xa_wk, xa_wv, xa_wo, ln2_g, ln2_b, w_group,
           w_expert_router, w1, w3, w2, ln3_g, ln3_b):
    B, S, D = x.shape
    T = B * S
    depth = w_in.shape[0]
    assert depth == 1, "single-layer stack: the out-projection kernel recomputes LayerNorm0(x)"
    alpha = float((2 * depth) ** 0.25)
    n_mem = mem.shape[1]
    bm = 256

    def vec(a):
        return a.reshape(1, -1).astype(F32)

    def pad_lanes(a):
        return jnp.pad(a, ((0, 0), (0, LANES - a.shape[1])))

    x2d = x.reshape(T, D)
    g0, b0 = vec(ln0_g), vec(ln0_b)
    l = 0
    w_main = w_in[l].astype(BF16)
    wab_hi, wab_lo = _split_bf16(pad_lanes(w_in[l][:, COL_AB:]))
    proj, ab = _ln_inproj(x2d, g0, b0, w_main, wab_hi, wab_lo)
    gate_p = jnp.pad(jnp.stack([gdn_a_log[l], gdn_dt_bias[l]]).astype(F32),
                     ((0, HALO - 2), (0, LANES - GDN_V_HEADS)))
    y = _mixer(proj.reshape(B, S, -1), ab.reshape(B, S, LANES), sconv_w[l].astype(F32),
               gdn_conv_w[l].astype(F32), gate_p, vec(gdn_norm_w[l]), B, S)
    h1 = _outproj(y.reshape(T, -1), x2d, g0, b0, w_mix_out[l].astype(BF16), vec(ln1_g[l]), vec(ln1_b[l]), alpha)
    kv = _matmul(mem.reshape(B * n_mem, D),
                 jnp.concatenate([xa_wk[l], xa_wv[l]], axis=1).astype(BF16))
    wr_hi, wr_lo = _split_bf16(pad_lanes(jnp.concatenate([w_group[l], w_expert_router[l]], axis=1)))
    h2, logits = _xattn(h1, xa_wq[l].astype(BF16), kv, xa_wo[l].astype(BF16), vec(ln2_g[l]),
                        vec(ln2_b[l]), wr_hi, wr_lo, alpha, S, n_mem)
    route, lane_counts, route_idx = _route(logits, T)
    tables, pos = _routing_tables(route_idx, lane_counts, T, bm)
    y_sorted = _experts(h2, w1[l], w3[l], w2[l], tables, bm)
    out = _combine(h2, y_sorted, pos, route, vec(ln3_g[l]), vec(ln3_b[l]), alpha)
    return out.reshape(B, S, D)
```

```python
import functools

import jax
import jax.numpy as jnp
from jax import lax
from jax.experimental import pallas as pl
from jax.experimental.pallas import tpu as pltpu

F32 = jnp.float32
BF16 = jnp.bfloat16

LN_EPS = 1e-5
RMS_EPS = 1e-6
L2_EPS = 1e-6

SCONV_WIDTH = 1024
SCONV_K = 3
GDN_HEAD_DIM = 128
GDN_V_HEADS = 8
GDN_K_HEADS = 4
GDN_KEY_DIM = GDN_K_HEADS * GDN_HEAD_DIM
GDN_VALUE_DIM = GDN_V_HEADS * GDN_HEAD_DIM
GDN_CONV_K = 4
GDN_CHUNK = 64
XA_HEADS = 4
N_GROUPS = 8
EXPERTS_PER_GROUP = 8
N_EXPERTS = N_GROUPS * EXPERTS_PER_GROUP
EXPERT_TOP_K = 2

LANES = 128
HALO = 8
VMEM_LIMIT = 56 * 1024 * 1024

COL_BG = 0
COL_CG = SCONV_WIDTH
COL_XA = 2 * SCONV_WIDTH
COL_Q = 3 * SCONV_WIDTH
COL_K = COL_Q + GDN_KEY_DIM
COL_V = COL_K + GDN_KEY_DIM
COL_Z = COL_V + GDN_VALUE_DIM
COL_AB = COL_Z + GDN_VALUE_DIM

NT_DIMS = (((1,), (1,)), ((), ()))
TN_DIMS = (((0,), (0,)), ((), ()))


def _dot(a, b):
    return jnp.dot(a, b, preferred_element_type=F32)


def _split_bf16(x):
    hi = x.astype(BF16)
    lo = (x - hi.astype(F32)).astype(BF16)
    return hi, lo


def _dot_split(x_hi, x_lo, w_hi, w_lo):
    n = w_hi.shape[1]
    both = _dot(x_hi, jnp.concatenate([w_hi, w_lo], axis=1))
    return both[:, :n] + both[:, n:] + _dot(x_lo, w_hi)


def _layer_norm(x, g, b):
    mu = jnp.mean(x, axis=-1, keepdims=True)
    xc = x - mu
    var = jnp.mean(xc * xc, axis=-1, keepdims=True)
    return xc * lax.rsqrt(var + LN_EPS) * g + b


def _sigmoid(x):
    return 1.0 / (1.0 + jnp.exp(-x))


def _silu(x):
    return x * _sigmoid(x)


def _params(sem):
    return pltpu.CompilerParams(dimension_semantics=sem, vmem_limit_bytes=VMEM_LIMIT)


def _ln_inproj_kernel(x_ref, g_ref, b_ref, w_ref, wab_hi_ref, wab_lo_ref, o_ref, ab_ref, hn_ref):
    @pl.when(pl.program_id(1) == 0)
    def _():
        h = _layer_norm(x_ref[...], g_ref[...], b_ref[...])
        hi, lo = _split_bf16(h)
        hn_ref[...] = hi
        ab_ref[...] = _dot_split(hi, lo, wab_hi_ref[...], wab_lo_ref[...])

    o_ref[...] = _dot(hn_ref[...], w_ref[...]).astype(o_ref.dtype)


def _ln_inproj(x2d, g, b, w_main, wab_hi, wab_lo, tm=1024, tn=1536):
    T, D = x2d.shape
    N = COL_AB
    return pl.pallas_call(
        _ln_inproj_kernel,
        grid=(T // tm, N // tn),
        in_specs=[
            pl.BlockSpec((tm, D), lambda i, j: (i, 0)),
            pl.BlockSpec((1, D), lambda i, j: (0, 0)),
            pl.BlockSpec((1, D), lambda i, j: (0, 0)),
            pl.BlockSpec((D, tn), lambda i, j: (0, j)),
            pl.BlockSpec((D, LANES), lambda i, j: (0, 0)),
            pl.BlockSpec((D, LANES), lambda i, j: (0, 0)),
        ],
        out_specs=[
            pl.BlockSpec((tm, tn), lambda i, j: (i, j)),
            pl.BlockSpec((tm, LANES), lambda i, j: (i, 0)),
        ],
        out_shape=[
            jax.ShapeDtypeStruct((T, N), BF16),
            jax.ShapeDtypeStruct((T, LANES), F32),
        ],
        scratch_shapes=[pltpu.VMEM((tm, D), BF16)],
        compiler_params=_params(("parallel", "arbitrary")),
        name="ln_inproj",
    )(x2d, g, b, w_main, wab_hi, wab_lo)


CONV_ROWS = 64


def _conv_taps(buf, base, lanes, w_ref, w_col0, ksize, r0):
    w_lanes = slice(w_col0 + lanes.start, w_col0 + lanes.stop)
    acc = None
    for j in range(ksize):
        start = base + HALO - ksize + 1 + j + r0
        term = w_ref[j:j + 1, w_lanes] * buf[lanes.start // LANES, pl.ds(start, CONV_ROWS), :]
        acc = term if acc is None else acc + term
    return acc


def _mixer_kernel(bg_ref, cg_ref, xa_ref, q_ref, k_ref, v_ref, z_ref, ab_ref,
                  sw_ref, cw_ref, gp_ref, nw_ref,
                  y_ref,
                  sbuf, qbuf, kbuf, vbuf, qn, kn, vc, g_s, b_s, state):
    nbatch, ts = q_ref.shape[0], q_ref.shape[1]
    C = GDN_CHUNK
    Dh = GDN_HEAD_DIM
    stage_bufs = (sbuf, qbuf, kbuf, vbuf)
    pitch = ts + HALO

    @pl.when(pl.program_id(0) == 0)
    def _():
        for buf in stage_bufs:
            for b in range(nbatch):
                buf[:, b * pitch:b * pitch + HALO, :] = jnp.zeros((buf.shape[0], HALO, LANES), F32)
        state[...] = jnp.zeros(state.shape, F32)

    row = lax.broadcasted_iota(jnp.int32, (ts, ts), 0)
    col = lax.broadcasted_iota(jnp.int32, (ts, ts), 1)
    tri = jnp.where((row // C == col // C) & (col <= row), 1.0, 0.0).astype(BF16)

    for b in range(nbatch):
        ab = ab_ref[b]
        xg = ab + gp_ref[1:2, :]
        softplus = jnp.maximum(xg, 0.0) + jnp.log1p(jnp.exp(-jnp.abs(xg)))
        g = -jnp.exp(gp_ref[0:1, :]) * softplus
        b_s[b] = _sigmoid(pltpu.roll(ab, LANES - GDN_V_HEADS, axis=1))
        g1 = g.astype(BF16)
        r1 = g - g1.astype(F32)
        g2 = r1.astype(BF16)
        g3 = (r1 - g2.astype(F32)).astype(BF16)
        g_s[b] = _dot(tri, g1) + _dot(tri, g2) + _dot(tri, g3)

    def conv_units(r0):
        rows = pl.ds(r0, CONV_ROWS)

        def srows(b):
            return pl.ds(b * pitch + HALO + r0, CONV_ROWS)

        def sconv_unit(b, ls):
            sbuf[ls.start // LANES, srows(b), :] = cg_ref[b, rows, ls].astype(F32) * xa_ref[b, rows, ls].astype(F32)
            conv = _conv_taps(sbuf, b * pitch, ls, sw_ref, 0, SCONV_K, r0)
            y_ref[b, rows, ls] = (bg_ref[b, rows, ls].astype(F32) * conv).astype(y_ref.dtype)

        def qk_unit(b, ls):
            qbuf[ls.start // LANES, srows(b), :] = q_ref[b, rows, ls].astype(F32)
            kbuf[ls.start // LANES, srows(b), :] = k_ref[b, rows, ls].astype(F32)
            qa = _silu(_conv_taps(qbuf, b * pitch, ls, cw_ref, 0, GDN_CONV_K, r0))
            ka = _silu(_conv_taps(kbuf, b * pitch, ls, cw_ref, GDN_KEY_DIM, GDN_CONV_K, r0))
            qn[b, rows, ls] = qa * (lax.rsqrt(jnp.sum(qa * qa, axis=-1, keepdims=True) + L2_EPS) * (Dh ** -0.5))
            kn[b, rows, ls] = ka * lax.rsqrt(jnp.sum(ka * ka, axis=-1, keepdims=True) + L2_EPS)

        def v_unit(b, ls):
            vbuf[ls.start // LANES, srows(b), :] = v_ref[b, rows, ls].astype(F32)
            vc[b, rows, ls] = _silu(_conv_taps(vbuf, b * pitch, ls, cw_ref, 2 * GDN_KEY_DIM, GDN_CONV_K, r0))

        units = []
        for b in range(nbatch):
            for n in range(SCONV_WIDTH // LANES):
                units.append(functools.partial(sconv_unit, b, slice(n * LANES, (n + 1) * LANES)))
            for n in range(GDN_K_HEADS):
                units.append(functools.partial(qk_unit, b, slice(n * Dh, (n + 1) * Dh)))
            for n in range(GDN_V_HEADS):
                units.append(functools.partial(v_unit, b, slice(n * Dh, (n + 1) * Dh)))
        return units
    ri = lax.broadcasted_iota(jnp.int32, (C, C), 0)
    ci = lax.broadcasted_iota(jnp.int32, (C, C), 1)
    causal = ci <= ri
    strict = ci < ri
    eye = jnp.where(ri == ci, 1.0, 0.0).astype(F32)
    nw = nw_ref[...]

    def chunk_step(r0, side_work):
        rows = pl.ds(r0, C)
        side_work = list(side_work)
        per_stage = -(-len(side_work) // 9)

        def fill():
            for _ in range(min(per_stage, len(side_work))):
                side_work.pop(0)()

        batches = range(nbatch)
        chains = [(b, h) for b in batches for h in range(GDN_V_HEADS)]
        kchains = [(b, kh) for b in batches for kh in range(GDN_K_HEADS)]

        def kidx(b, h):
            return b * GDN_K_HEADS + h // 2

        gc, gct, bc, eg, e_tail, e_last = [], [], [], [], [], []
        for b in batches:
            gc.append(g_s[b, rows, :])
            gct.append(gc[b].T)
            bc.append(b_s[b, rows, :])
            eg.append(jnp.exp(gc[b]))
            g_last = g_s[b, pl.ds(r0 + C - 1, 1), :]
            e_tail.append(jnp.exp(g_last - gc[b]))
            e_last.append(jnp.exp(g_last))
        qc, kc, kcb, kk, qk = [], [], [], [], []
        for b, kh in kchains:
            ksl = slice(kh * Dh, (kh + 1) * Dh)
            qc.append(qn[b, rows, ksl])
            kc.append(kn[b, rows, ksl])
            kcb.append(kc[-1].astype(BF16))
        for n in range(len(kchains)):
            kk.append(lax.dot_general(kcb[n], kcb[n], NT_DIMS, preferred_element_type=F32))
            qk.append(lax.dot_general(qc[n].astype(BF16), kcb[n], NT_DIMS, preferred_element_type=F32))
        fill()
        decay, bcol, egc, inv, pw = [], [], [], [], []
        for b, h in chains:
            decay.append(jnp.exp(jnp.where(causal, gc[b][:, h:h + 1] - gct[b][h:h + 1, :], -1e30)))
            bcol.append(bc[b][:, h:h + 1])
            egc.append(eg[b][:, h:h + 1])
        for n, (b, h) in enumerate(chains):
            low = jnp.where(strict, kk[kidx(b, h)] * bcol[n] * decay[n], 0.0)
            lb = low.astype(BF16)
            inv.append(eye - low)
            pw.append(_dot(lb, lb))
        fill()
        for it in range(5):
            for n in range(len(chains)):
                pwb = pw[n].astype(BF16)
                both = jnp.concatenate([inv[n], pw[n]], axis=0) if it < 4 else inv[n]
                prod = _dot(both.astype(BF16), pwb)
                inv[n] = inv[n] + prod[:C]
                if it < 4:
                    pw[n] = prod[C:]
            fill()
        u, w = [], []
        for n, (b, h) in enumerate(chains):
            vsl = slice(h * Dh, (h + 1) * Dh)
            rhs = jnp.concatenate([vc[b, rows, vsl] * bcol[n], kc[kidx(b, h)] * (bcol[n] * egc[n])], axis=1)
            sol = _dot(inv[n].astype(BF16), rhs.astype(BF16))
            u.append(sol[:, :Dh])
            w.append(sol[:, Dh:].astype(BF16))
        fill()
        s_old, sb, vnb = [], [], []
        for n, (b, h) in enumerate(chains):
            s_old.append(state[b, h])
            sb.append(s_old[n].astype(BF16))
            vnb.append((u[n] - _dot(w[n], sb[n])).astype(BF16))
        fill()
        for n, (b, h) in enumerate(chains):
            vsl = slice(h * Dh, (h + 1) * Dh)
            attn = (qk[kidx(b, h)] * decay[n]).astype(BF16)
            qd = (qc[kidx(b, h)] * egc[n]).astype(BF16)
            kt = (kc[kidx(b, h)] * e_tail[b][:, h:h + 1]).astype(BF16)
            o = _dot(qd, sb[n]) + _dot(attn, vnb[n])
            state[b, h] = s_old[n] * e_last[b][:, h:h + 1] + lax.dot_general(
                kt, vnb[n], TN_DIMS, preferred_element_type=F32)
            zc = z_ref[b, rows, vsl].astype(F32)
            ms = jnp.mean(o * o, axis=-1, keepdims=True)
            yb = o * lax.rsqrt(ms + RMS_EPS) * nw * _silu(zc)
            y_ref[b, rows, SCONV_WIDTH + h * Dh:SCONV_WIDTH + (h + 1) * Dh] = yb.astype(y_ref.dtype)
        while side_work:
            fill()

    assert CONV_ROWS == C
    for unit in conv_units(0):
        unit()

    def pipelined(c, carry):
        r0 = pl.multiple_of(c * C, C)
        chunk_step(r0, conv_units(r0 + C))
        return carry

    lax.fori_loop(0, ts // C - 1, pipelined, 0)
    chunk_step(ts - C, [])
    for buf in stage_bufs:
        for b in range(nbatch):
            buf[:, b * pitch:b * pitch + HALO, :] = buf[:, b * pitch + ts:b * pitch + ts + HALO, :]


def _mixer(proj, ab, sconv_w, conv_w, gate_p, norm_w, batch, seq, ts=256):
    def rows(cb, width):
        return pl.BlockSpec((batch, ts, width), lambda s, cb=cb: (0, s, cb))

    def whole(a):
        return pl.BlockSpec(a.shape, lambda s: (0,) * a.ndim)

    width_out = SCONV_WIDTH + GDN_VALUE_DIM
    return pl.pallas_call(
        _mixer_kernel,
        grid=(seq // ts,),
        in_specs=[
            rows(COL_BG // SCONV_WIDTH, SCONV_WIDTH),
            rows(COL_CG // SCONV_WIDTH, SCONV_WIDTH),
            rows(COL_XA // SCONV_WIDTH, SCONV_WIDTH),
            rows(COL_Q // GDN_KEY_DIM, GDN_KEY_DIM),
            rows(COL_K // GDN_KEY_DIM, GDN_KEY_DIM),
            rows(COL_V // GDN_VALUE_DIM, GDN_VALUE_DIM),
            rows(COL_Z // GDN_VALUE_DIM, GDN_VALUE_DIM),
            rows(0, LANES),
            whole(sconv_w), whole(conv_w), whole(gate_p), whole(norm_w),
        ],
        out_specs=pl.BlockSpec((batch, ts, width_out), lambda s: (0, s, 0)),
        out_shape=jax.ShapeDtypeStruct((batch, seq, width_out), BF16),
        scratch_shapes=[
            pltpu.VMEM((SCONV_WIDTH // LANES, batch * (ts + HALO), LANES), F32),
            pltpu.VMEM((GDN_KEY_DIM // LANES, batch * (ts + HALO), LANES), F32),
            pltpu.VMEM((GDN_KEY_DIM // LANES, batch * (ts + HALO), LANES), F32),
            pltpu.VMEM((GDN_VALUE_DIM // LANES, batch * (ts + HALO), LANES), F32),
            pltpu.VMEM((batch, ts, GDN_KEY_DIM), F32),
            pltpu.VMEM((batch, ts, GDN_KEY_DIM), F32),
            pltpu.VMEM((batch, ts, GDN_VALUE_DIM), F32),
            pltpu.VMEM((batch, ts, LANES), F32),
            pltpu.VMEM((batch, ts, LANES), F32),
            pltpu.VMEM((batch, GDN_V_HEADS, GDN_HEAD_DIM, GDN_HEAD_DIM), F32),
        ],
        compiler_params=_params(("arbitrary",)),
        name="mixer",
    )(proj, proj, proj, proj, proj, proj, proj, ab, sconv_w, conv_w, gate_p, norm_w)


def _outproj_kernel(alpha, y_ref, x_ref, g0_ref, b0_ref, w_ref, g1_ref, b1_ref, o_ref):
    h0 = _layer_norm(x_ref[...], g0_ref[...], b0_ref[...])
    r = alpha * h0 + _dot(y_ref[...], w_ref[...])
    o_ref[...] = _layer_norm(r, g1_ref[...], b1_ref[...]).astype(o_ref.dtype)


def _outproj(y, x2d, g0, b0, w, g1, b1, alpha, tm=512):
    T, D = x2d.shape
    vec = pl.BlockSpec((1, D), lambda i: (0, 0))
    return pl.pallas_call(
        functools.partial(_outproj_kernel, alpha),
        grid=(T // tm,),
        in_specs=[
            pl.BlockSpec((tm, y.shape[1]), lambda i: (i, 0)),
            pl.BlockSpec((tm, D), lambda i: (i, 0)),
            vec, vec,
            pl.BlockSpec(w.shape, lambda i: (0, 0)),
            vec, vec,
        ],
        out_specs=pl.BlockSpec((tm, D), lambda i: (i, 0)),
        out_shape=jax.ShapeDtypeStruct((T, D), BF16),
        compiler_params=_params(("parallel",)),
        name="outproj_ln1",
    )(y, x2d, g0, b0, w, g1, b1)


def _matmul_kernel(x_ref, w_ref, o_ref):
    o_ref[...] = _dot(x_ref[...].astype(BF16), w_ref[...]).astype(o_ref.dtype)


def _matmul(x, w, tn=1024):
    M, K = x.shape
    N = w.shape[1]
    return pl.pallas_call(
        _matmul_kernel,
        grid=(N // tn,),
        in_specs=[pl.BlockSpec((M, K), lambda j: (0, 0)), pl.BlockSpec((K, tn), lambda j: (0, j))],
        out_specs=pl.BlockSpec((M, tn), lambda j: (0, j)),
        out_shape=jax.ShapeDtypeStruct((M, N), BF16),
        compiler_params=_params(("parallel",)),
        name="kvproj",
    )(x, w)


def _xattn_kernel(alpha, h_ref, wq_ref, k_ref, v_ref, wo_ref, g_ref, b_ref, wr_hi_ref, wr_lo_ref,
                  o_ref, lg_ref, acc0, acc1):
    D = h_ref.shape[1]
    tm = h_ref.shape[0]
    hd = D // XA_HEADS
    tail_rows = tm // XA_HEADS
    step_id = pl.program_id(0)

    @pl.when(step_id == 0)
    def _():
        acc1[...] = jnp.zeros(acc1.shape, F32)

    def step(acc_new, acc_done):
        def tail(n):
            rows = slice(n * tail_rows, (n + 1) * tail_rows)
            h2 = _layer_norm(acc_done[rows, :], g_ref[...], b_ref[...])
            o_ref[rows, :] = h2
            hi, lo = _split_bf16(h2)
            lg_ref[rows, :] = _dot_split(hi, lo, wr_hi_ref[...], wr_lo_ref[...])

        h1 = h_ref[...]
        acc_new[...] = alpha * h1.astype(F32)
        for hh in range(XA_HEADS):
            sl = slice(hh * hd, (hh + 1) * hd)
            qh = _dot(h1, wq_ref[:, sl]).astype(BF16)
            s = lax.dot_general(qh, k_ref[:, sl], NT_DIMS, preferred_element_type=F32) * (hd ** -0.5)
            p = jnp.exp(s - jnp.max(s, axis=-1, keepdims=True))
            p = p / jnp.sum(p, axis=-1, keepdims=True)
            oh = _dot(p.astype(BF16), v_ref[:, sl]).astype(BF16)
            acc_new[...] += _dot(oh, wo_ref[sl, :])
            tail(hh)

    parity = lax.rem(step_id, 2)
    pl.when(parity == 0)(functools.partial(step, acc0, acc1))
    pl.when(parity == 1)(functools.partial(step, acc1, acc0))


def _xattn(h1, wq, kv, wo, g, b, wr_hi, wr_lo, alpha, seq, n_mem, tm=512):
    T, D = h1.shape
    nt = T // tm
    spb = seq // tm
    vec = pl.BlockSpec((1, D), lambda s: (0, 0))
    full = pl.BlockSpec((D, D), lambda s: (0, 0))

    def tile_in(s):
        return jnp.minimum(s, nt - 1)

    def tile_out(s):
        return jnp.where(s == 0, nt, s - 1)

    return pl.pallas_call(
        functools.partial(_xattn_kernel, alpha),
        grid=(nt + 1,),
        in_specs=[
            pl.BlockSpec((tm, D), lambda s: (tile_in(s), 0)),
            full,
            pl.BlockSpec((n_mem, D), lambda s: (tile_in(s) // spb, 0)),
            pl.BlockSpec((n_mem, D), lambda s: (tile_in(s) // spb, 1)),
            full,
            vec, vec,
            pl.BlockSpec((D, LANES), lambda s: (0, 0)),
            pl.BlockSpec((D, LANES), lambda s: (0, 0)),
        ],
        out_specs=[
            pl.BlockSpec((tm, D), lambda s: (tile_out(s), 0)),
            pl.BlockSpec((tm, LANES), lambda s: (tile_out(s), 0)),
        ],
        out_shape=[
            jax.ShapeDtypeStruct((T + tm, D), F32),
            jax.ShapeDtypeStruct((T + tm, LANES), F32),
        ],
        scratch_shapes=[pltpu.VMEM((tm, D), F32), pltpu.VMEM((tm, D), F32)],
        compiler_params=_params(("arbitrary",)),
        name="xattn_ln2",
    )(h1, wq, kv, kv, wo, g, b, wr_hi, wr_lo)


def _route_kernel(lg_ref, o_ref, cnt_ref, idx_ref, carry):
    @pl.when(pl.program_id(0) == 0)
    def _():
        carry[...] = jnp.zeros(carry.shape, F32)

    lg = lg_ref[...]
    lane = lax.broadcasted_iota(jnp.int32, lg.shape, 1)
    neg = -jnp.inf
    big = jnp.int32(LANES)
    gl = jnp.where(lane < N_GROUPS, lg, neg)
    gmax = jnp.max(gl, axis=-1, keepdims=True)
    g_idx = jnp.min(jnp.where(gl == gmax, lane, big), axis=-1, keepdims=True)
    g_prob = 1.0 / jnp.sum(jnp.exp(gl - gmax), axis=-1, keepdims=True)
    e_lane = lane - N_GROUPS
    in_group = (e_lane >= g_idx * EXPERTS_PER_GROUP) & (e_lane < (g_idx + 1) * EXPERTS_PER_GROUP)
    el = jnp.where(in_group, lg, neg)
    m1 = jnp.max(el, axis=-1, keepdims=True)
    i1 = jnp.min(jnp.where(el == m1, lane, big), axis=-1, keepdims=True)
    denom = jnp.sum(jnp.exp(el - m1), axis=-1, keepdims=True)
    el2 = jnp.where(lane == i1, neg, el)
    m2 = jnp.max(el2, axis=-1, keepdims=True)
    i2 = jnp.min(jnp.where(el2 == m2, lane, big), axis=-1, keepdims=True)
    p1 = 1.0 / denom
    p2 = jnp.exp(m2 - m1) / denom
    psum = p1 + p2
    gate1 = g_prob * p1 / psum
    gate2 = g_prob * p2 / psum
    tm = lg.shape[0]
    onehot = jnp.where((lane == i1) | (lane == i2), 1.0, 0.0)
    ri = lax.broadcasted_iota(jnp.int32, (tm, tm), 0)
    ci = lax.broadcasted_iota(jnp.int32, (tm, tm), 1)
    earlier = jnp.where(ci < ri, 1.0, 0.0).astype(BF16)
    before = _dot(earlier, onehot.astype(BF16)) + carry[...]
    rank1 = jnp.sum(jnp.where(lane == i1, before, 0.0), axis=-1, keepdims=True)
    rank2 = jnp.sum(jnp.where(lane == i2, before, 0.0), axis=-1, keepdims=True)
    carry[...] += jnp.sum(onehot, axis=0, keepdims=True)
    cnt_ref[...] = carry[...]
    cols = ((i1 - N_GROUPS).astype(F32), (i2 - N_GROUPS).astype(F32), gate1, gate2, rank1, rank2)
    out = jnp.zeros(lg.shape, F32)
    for n, val in enumerate(cols):
        out = jnp.where(lane == n, val, out)
    o_ref[...] = out
    idx_ref[...] = out[:, 0:HALO].astype(jnp.int32)


def _route(logits, T, tm=512):
    spec = pl.BlockSpec((tm, LANES), lambda i: (i, 0))
    return pl.pallas_call(
        _route_kernel,
        grid=(T // tm,),
        in_specs=[spec],
        out_specs=[spec, pl.BlockSpec((1, LANES), lambda i: (0, 0)), pl.BlockSpec((tm, HALO), lambda i: (i, 0))],
        out_shape=[jax.ShapeDtypeStruct((T, LANES), F32), jax.ShapeDtypeStruct((1, LANES), F32),
                   jax.ShapeDtypeStruct((T, HALO), jnp.int32)],
        scratch_shapes=[pltpu.VMEM((1, LANES), F32)],
        compiler_params=_params(("arbitrary",)),
        name="route",
    )(logits)


def _wait_rows(buf, sem):
    pltpu.make_async_copy(buf, buf, sem).wait()


def _expert_kernel(bexp_ref, nval_ref, off_ref, first_ref, wslot_ref, nexte_ref, order_ref,
                   h_hbm, w1_hbm, w3_hbm, w2_hbm,
                   y_ref, xbuf0, xbuf1, xbuf2, w1s, w3s, w2s, w1b, w3b, w2b, gsem, wsem):
    i = pl.program_id(0)
    last = pl.num_programs(0) - 1
    xbuf = (xbuf0, xbuf1, xbuf2)
    nbuf = len(xbuf)
    bm = xbuf0.shape[0]
    slot = lax.rem(i, nbuf)
    nv = nval_ref[i]
    nv_prev = jnp.where(i > 0, nval_ref[jnp.maximum(i - 1, 0)], 0)

    def gather_row(base, r, k, priority=0):
        tok = lax.shift_right_logical(order_ref[base + r], 1)
        pltpu.make_async_copy(h_hbm.at[pl.ds(tok, 1), :], xbuf[k].at[pl.ds(r, 1), :],
                              gsem.at[k]).start(priority)

    def for_slot(fn):
        for k in range(nbuf):
            pl.when(slot == k)(functools.partial(fn, k))

    def weight_copies(e, s):
        return (pltpu.make_async_copy(w1_hbm.at[e], w1s.at[s], wsem.at[s]),
                pltpu.make_async_copy(w3_hbm.at[e], w3s.at[s], wsem.at[s]),
                pltpu.make_async_copy(w2_hbm.at[e], w2s.at[s], wsem.at[s]))

    @pl.when(i == 0)
    def _():
        for c in weight_copies(bexp_ref[0], 0):
            c.start(priority=1)

        def body(r, c):
            gather_row(off_ref[0], r, 0)
            gather_row(off_ref[1], r, 1)
            return c
        lax.fori_loop(0, bm, body, 0)

    @pl.when(nv > 0)
    def _():
        @pl.when(first_ref[i] == 1)
        def _():
            ws = wslot_ref[i]
            nxt = nexte_ref[i]

            @pl.when(nxt >= 0)
            def _():
                for c in weight_copies(nxt, 1 - ws):
                    c.start(priority=1)

            def cast(s):
                for c in weight_copies(0, s):
                    c.wait()
                w1b[...] = w1s[s].astype(BF16)
                w3b[...] = w3s[s].astype(BF16)
                w2b[...] = w2s[s].astype(BF16)

            for s in range(2):
                pl.when(ws == s)(functools.partial(cast, s))

        ahead_base = off_ref[jnp.minimum(i + 2, last)]

        def compute(cur):
            _wait_rows(xbuf[cur], gsem.at[cur])
            for r in range(bm):
                gather_row(ahead_base, r, (cur + 2) % nbuf)
            xb = xbuf[cur][...].astype(BF16)
            hid = _silu(_dot(xb, w1b[...])) * _dot(xb, w3b[...])
            y_ref[...] = _dot(hid.astype(BF16), w2b[...])

        for_slot(compute)

    @pl.when(nv == 0)
    def _():
        y_ref[...] = jnp.zeros(y_ref.shape, y_ref.dtype)

        @pl.when(nv_prev > 0)
        def _():
            def drain(cur):
                _wait_rows(xbuf[cur], gsem.at[cur])
                _wait_rows(xbuf[(cur + 1) % nbuf], gsem.at[(cur + 1) % nbuf])
            for_slot(drain)


def _experts(h2, w1, w3, w2, tables, bm):
    bexp, nval, off, first, wslot, nexte, order = tables
    D = h2.shape[1]
    nb = bexp.shape[0]
    ff = w1.shape[2]
    hbm = pl.BlockSpec(memory_space=pl.ANY)
    grid_spec = pltpu.PrefetchScalarGridSpec(
        num_scalar_prefetch=len(tables),
        grid=(nb,),
        in_specs=[hbm, hbm, hbm, hbm],
        out_specs=pl.BlockSpec((bm, D), lambda i, *tables: (i, 0)),
        scratch_shapes=[
            pltpu.VMEM((bm, D), F32),
            pltpu.VMEM((bm, D), F32),
            pltpu.VMEM((bm, D), F32),
            pltpu.VMEM((2, D, ff), F32),
            pltpu.VMEM((2, D, ff), F32),
            pltpu.VMEM((2, ff, D), F32),
            pltpu.VMEM((D, ff), BF16),
            pltpu.VMEM((D, ff), BF16),
            pltpu.VMEM((ff, D), BF16),
            pltpu.SemaphoreType.DMA((3,)),
            pltpu.SemaphoreType.DMA((2,)),
        ],
    )
    return pl.pallas_call(
        _expert_kernel,
        grid_spec=grid_spec,
        out_shape=jax.ShapeDtypeStruct((nb * bm, D), F32),
        compiler_params=_params(("arbitrary",)),
        name="experts",
    )(*tables, h2, w1, w3, w2)


def _combine_kernel(alpha, pos_ref, h_ref, rt_ref, g_ref, b_ref, y_hbm, o_ref, ybuf0, ybuf1, ybuf2, gsem):
    i = pl.program_id(0)
    nt = pl.num_programs(0)
    K = EXPERT_TOP_K
    tm = h_ref.shape[0]
    ybuf = (ybuf0, ybuf1, ybuf2)
    nbuf = len(ybuf)
    slot = lax.rem(i, nbuf)

    def gather_row(tile, r, k, buf, priority=0):
        src = pos_ref[(tile * tm + r) * K + k]
        pltpu.make_async_copy(y_hbm.at[pl.ds(src, 1), :], ybuf[buf].at[k, pl.ds(r, 1), :],
                              gsem.at[buf]).start(priority)

    @pl.when(i == 0)
    def _():
        def body(r, c):
            for k in range(K):
                gather_row(0, r, k, 0)
                gather_row(1, r, k, 1)
            return c
        lax.fori_loop(0, tm, body, 0)

    ahead_tile = jnp.minimum(i + 2, nt - 1)
    rt = rt_ref[...]

    def compute(cur):
        _wait_rows(ybuf[cur], gsem.at[cur])
        for r in range(tm):
            for k in range(K):
                gather_row(ahead_tile, r, k, (cur + 2) % nbuf, k % 2)
        ffn = ybuf[cur][0] * rt[:, 2:3] + ybuf[cur][1] * rt[:, 3:4]
        o_ref[...] = _layer_norm(alpha * h_ref[...] + ffn, g_ref[...], b_ref[...])

        @pl.when(i == nt - 1)
        def _():
            _wait_rows(ybuf[(cur + 1) % nbuf], gsem.at[(cur + 1) % nbuf])
            _wait_rows(ybuf[(cur + 2) % nbuf], gsem.at[(cur + 2) % nbuf])

    for k in range(nbuf):
        pl.when(slot == k)(functools.partial(compute, k))


def _combine(h2, y_sorted, pos, route, g, b, alpha, tm=256):
    T, D = route.shape[0], h2.shape[1]
    vec = pl.BlockSpec((1, D), lambda i, ps: (0, 0))
    grid_spec = pltpu.PrefetchScalarGridSpec(
        num_scalar_prefetch=1,
        grid=(T // tm,),
        in_specs=[
            pl.BlockSpec((tm, D), lambda i, ps: (i, 0)),
            pl.BlockSpec((tm, LANES), lambda i, ps: (i, 0)),
            vec, vec,
            pl.BlockSpec(memory_space=pl.ANY),
        ],
        out_specs=pl.BlockSpec((tm, D), lambda i, ps: (i, 0)),
        scratch_shapes=[
            pltpu.VMEM((EXPERT_TOP_K, tm, D), F32),
            pltpu.VMEM((EXPERT_TOP_K, tm, D), F32),
            pltpu.VMEM((EXPERT_TOP_K, tm, D), F32),
            pltpu.SemaphoreType.DMA((3,)),
        ],
    )
    return pl.pallas_call(
        functools.partial(_combine_kernel, alpha),
        grid_spec=grid_spec,
        out_shape=jax.ShapeDtypeStruct((T, D), F32),
        compiler_params=_params(("arbitrary",)),
        name="combine_ln3",
    )(pos, h2, route, g, b, y_sorted)


def _routing_tables(route_idx, lane_counts, n_tok, bm):
    K = EXPERT_TOP_K
    TK = n_tok * K
    flat_e = route_idx[:, 0:K].reshape(-1)
    flat_rank = route_idx[:, 2 * K:3 * K].reshape(-1)
    order = jnp.argsort(flat_e).astype(jnp.int32)
    experts = jnp.arange(N_EXPERTS, dtype=jnp.int32)
    counts = lane_counts[0, N_GROUPS:N_GROUPS + N_EXPERTS].astype(jnp.int32)
    start = jnp.cumsum(counts) - counts
    padded = (counts + bm - 1) // bm * bm
    padded_end = jnp.cumsum(padded)
    padded_start = padded_end - padded
    nb = TK // bm + N_EXPERTS
    block_start = jnp.arange(nb, dtype=jnp.int32) * bm
    bexp = jnp.minimum(jnp.sum((block_start[:, None] >= padded_end[None, :]).astype(jnp.int32), axis=1),
                       N_EXPERTS - 1)
    of_block = bexp[:, None] == experts[None, :]

    def pick(per_expert):
        return jnp.sum(jnp.where(of_block, per_expert[None, :], 0), axis=1)

    rank0 = block_start - pick(padded_start)
    nval = jnp.clip(pick(counts) - rank0, 0, bm).astype(jnp.int32)
    off = jnp.clip(pick(start) + rank0, 0, TK).astype(jnp.int32)
    order = jnp.concatenate([order, jnp.zeros((bm,), jnp.int32)])
    pos = flat_rank + jnp.sum(jnp.where(flat_e[:, None] == experts[None, :], padded_start[None, :], 0), axis=1)
    bexp = bexp.astype(jnp.int32)
    active = counts > 0
    ordinal = jnp.cumsum(active.astype(jnp.int32)) - 1
    later_active = (experts[None, :] > experts[:, None]) & active[None, :]
    next_active = jnp.min(jnp.where(later_active, experts[None, :], N_EXPERTS), axis=1)
    next_active = jnp.where(next_active == N_EXPERTS, -1, next_active)
    prev_bexp = jnp.concatenate([jnp.full((1,), -1, jnp.int32), bexp[:-1]])
    first = ((nval > 0) & (bexp != prev_bexp)).astype(jnp.int32)
    wslot = (pick(ordinal) % 2).astype(jnp.int32)
    nexte = pick(next_active).astype(jnp.int32)
    return (bexp, nval, off, first, wslot, nexte, order), pos.astype(jnp.int32)


def kernel(x, mem, ln0_g, ln0_b, w_in, sconv_w, gdn_conv_w, gdn_a_log, gdn_dt_bias, gdn_norm_w,
           w_mix_out, ln1_g, ln1_b, xa_wq, xa_wk, xa_wv, xa_wo, ln2_g, ln2_b, w_group,
           w_expert_router, w1, w3, w2, ln3_g, ln3_b):
    B, S, D = x.shape
    T = B * S
    depth = w_in.shape[0]
    assert depth == 1, "single-layer stack: the out-projection kernel recomputes LayerNorm0(x)"
    alpha = float((2 * depth) ** 0.25)
    n_mem = mem.shape[1]
    bm = 256

    def vec(a):
        return a.reshape(1, -1).astype(F32)

    def pad_lanes(a):
        return jnp.pad(a, ((0, 0), (0, LANES - a.shape[1])))

    x2d = x.reshape(T, D)
    g0, b0 = vec(ln0_g), vec(ln0_b)
    l = 0
    w_main = w_in[l].astype(BF16)
    wab_hi, wab_lo = _split_bf16(pad_lanes(w_in[l][:, COL_AB:]))
    proj, ab = _ln_inproj(x2d, g0, b0, w_main, wab_hi, wab_lo)
    gate_p = jnp.pad(jnp.stack([gdn_a_log[l], gdn_dt_bias[l]]).astype(F32),
                     ((0, HALO - 2), (0, LANES - GDN_V_HEADS)))
    y = _mixer(proj.reshape(B, S, -1), ab.reshape(B, S, LANES), sconv_w[l].astype(F32),
               gdn_conv_w[l].astype(F32), gate_p, vec(gdn_norm_w[l]), B, S)
    h1 = _outproj(y.reshape(T, -1), x2d, g0, b0, w_mix_out[l].astype(BF16), vec(ln1_g[l]), vec(ln1_b[l]), alpha)
    kv = _matmul(mem.reshape(B * n_mem, D),
                 jnp.concatenate([xa_wk[l], xa_wv[l]], axis=1).astype(BF16))
    wr_hi, wr_lo = _split_bf16(pad_lanes(jnp.concatenate([w_group[l], w_expert_router[l]], axis=1)))
    h2, logits = _xattn(h1, xa_wq[l].astype(BF16), kv, xa_wo[l].astype(BF16), vec(ln2_g[l]),
                        vec(ln2_b[l]), wr_hi, wr_lo, alpha, S, n_mem)
    route, lane_counts, route_idx = _route(logits, T)
    tables, pos = _routing_tables(route_idx, lane_counts, T, bm)
    y_sorted = _experts(h2, w1[l], w3[l], w2[l], tables, bm)
    out = _combine(h2, y_sorted, pos, route, vec(ln3_g[l]), vec(ln3_b[l]), alpha)
    return out.reshape(B, S, D)
```

```python
import functools

import jax
import jax.numpy as jnp
from jax import lax
from jax.experimental import pallas as pl
from jax.experimental.pallas import tpu as pltpu

F32 = jnp.float32
BF16 = jnp.bfloat16

LN_EPS = 1e-5
RMS_EPS = 1e-6
L2_EPS = 1e-6

SCONV_WIDTH = 1024
SCONV_K = 3
GDN_HEAD_DIM = 128
GDN_V_HEADS = 8
GDN_K_HEADS = 4
GDN_KEY_DIM = GDN_K_HEADS * GDN_HEAD_DIM
GDN_VALUE_DIM = GDN_V_HEADS * GDN_HEAD_DIM
GDN_CONV_K = 4
GDN_CHUNK = 64
XA_HEADS = 4
N_GROUPS = 8
EXPERTS_PER_GROUP = 8
N_EXPERTS = N_GROUPS * EXPERTS_PER_GROUP
EXPERT_TOP_K = 2

LANES = 128
HALO = 8
ROUTE_COLS = 8
VMEM_LIMIT = 56 * 1024 * 1024

COL_BG = 0
COL_CG = SCONV_WIDTH
COL_XA = 2 * SCONV_WIDTH
COL_Q = 3 * SCONV_WIDTH
COL_K = COL_Q + GDN_KEY_DIM
COL_V = COL_K + GDN_KEY_DIM
COL_Z = COL_V + GDN_VALUE_DIM
COL_AB = COL_Z + GDN_VALUE_DIM

NT_DIMS = (((1,), (1,)), ((), ()))
TN_DIMS = (((0,), (0,)), ((), ()))


def _dot(a, b):
    return jnp.dot(a, b, preferred_element_type=F32)


def _split_bf16(x):
    hi = x.astype(BF16)
    lo = (x - hi.astype(F32)).astype(BF16)
    return hi, lo


def _dot_split(x_hi, x_lo, w_hi, w_lo):
    n = w_hi.shape[1]
    both = _dot(x_hi, jnp.concatenate([w_hi, w_lo], axis=1))
    return both[:, :n] + both[:, n:] + _dot(x_lo, w_hi)


def _layer_norm(x, g, b):
    mu = jnp.mean(x, axis=-1, keepdims=True)
    xc = x - mu
    var = jnp.mean(xc * xc, axis=-1, keepdims=True)
    return xc * lax.rsqrt(var + LN_EPS) * g + b


def _sigmoid(x):
    return 1.0 / (1.0 + jnp.exp(-x))


def _silu(x):
    return x * _sigmoid(x)


def _params(sem):
    return pltpu.CompilerParams(dimension_semantics=sem, vmem_limit_bytes=VMEM_LIMIT)


def _ln_inproj_kernel(x_ref, g_ref, b_ref, w_ref, wab_hi_ref, wab_lo_ref, o_ref, ab_ref, hn_ref):
    @pl.when(pl.program_id(1) == 0)
    def _():
        h = _layer_norm(x_ref[...], g_ref[...], b_ref[...])
        hi, lo = _split_bf16(h)
        hn_ref[...] = hi
        ab_ref[...] = _dot_split(hi, lo, wab_hi_ref[...], wab_lo_ref[...])

    o_ref[...] = _dot(hn_ref[...], w_ref[...]).astype(o_ref.dtype)


def _ln_inproj(x2d, g, b, w_main, wab_hi, wab_lo, tm=1024, tn=1536):
    T, D = x2d.shape
    N = COL_AB
    return pl.pallas_call(
        _ln_inproj_kernel,
        grid=(T // tm, N // tn),
        in_specs=[
            pl.BlockSpec((tm, D), lambda i, j: (i, 0)),
            pl.BlockSpec((1, D), lambda i, j: (0, 0)),
            pl.BlockSpec((1, D), lambda i, j: (0, 0)),
            pl.BlockSpec((D, tn), lambda i, j: (0, j)),
            pl.BlockSpec((D, LANES), lambda i, j: (0, 0)),
            pl.BlockSpec((D, LANES), lambda i, j: (0, 0)),
        ],
        out_specs=[
            pl.BlockSpec((tm, tn), lambda i, j: (i, j)),
            pl.BlockSpec((tm, LANES), lambda i, j: (i, 0)),
        ],
        out_shape=[
            jax.ShapeDtypeStruct((T, N), BF16),
            jax.ShapeDtypeStruct((T, LANES), F32),
        ],
        scratch_shapes=[pltpu.VMEM((tm, D), BF16)],
        compiler_params=_params(("parallel", "arbitrary")),
        name="ln_inproj",
    )(x2d, g, b, w_main, wab_hi, wab_lo)


CONV_ROWS = 64


def _conv_taps(buf, base, lanes, w_ref, w_col0, ksize, r0):
    w_lanes = slice(w_col0 + lanes.start, w_col0 + lanes.stop)
    acc = None
    for j in range(ksize):
        start = base + HALO - ksize + 1 + j + r0
        term = w_ref[j:j + 1, w_lanes] * buf[lanes.start // LANES, pl.ds(start, CONV_ROWS), :]
        acc = term if acc is None else acc + term
    return acc


def _mixer_kernel(bg_ref, cg_ref, xa_ref, q_ref, k_ref, v_ref, z_ref, ab_ref,
                  sw_ref, cw_ref, gp_ref, nw_ref,
                  y_ref,
                  sbuf, qbuf, kbuf, vbuf, qn, kn, vc, g_s, b_s, state):
    nbatch, ts = q_ref.shape[0], q_ref.shape[1]
    C = GDN_CHUNK
    Dh = GDN_HEAD_DIM
    stage_bufs = (sbuf, qbuf, kbuf, vbuf)
    pitch = ts + HALO

    @pl.when(pl.program_id(0) == 0)
    def _():
        for buf in stage_bufs:
            for b in range(nbatch):
                buf[:, b * pitch:b * pitch + HALO, :] = jnp.zeros((buf.shape[0], HALO, LANES), F32)
        state[...] = jnp.zeros(state.shape, F32)

    row = lax.broadcasted_iota(jnp.int32, (ts, ts), 0)
    col = lax.broadcasted_iota(jnp.int32, (ts, ts), 1)
    tri = jnp.where((row // C == col // C) & (col <= row), 1.0, 0.0).astype(BF16)

    for b in range(nbatch):
        ab = ab_ref[b]
        xg = ab + gp_ref[1:2, :]
        softplus = jnp.maximum(xg, 0.0) + jnp.log1p(jnp.exp(-jnp.abs(xg)))
        g = -jnp.exp(gp_ref[0:1, :]) * softplus
        b_s[b] = _sigmoid(pltpu.roll(ab, LANES - GDN_V_HEADS, axis=1))
        g1 = g.astype(BF16)
        r1 = g - g1.astype(F32)
        g2 = r1.astype(BF16)
        g3 = (r1 - g2.astype(F32)).astype(BF16)
        g_s[b] = _dot(tri, g1) + _dot(tri, g2) + _dot(tri, g3)

    def conv_units(r0):
        rows = pl.ds(r0, CONV_ROWS)

        def srows(b):
            return pl.ds(b * pitch + HALO + r0, CONV_ROWS)

        def sconv_unit(b, ls):
            sbuf[ls.start // LANES, srows(b), :] = cg_ref[b, rows, ls].astype(F32) * xa_ref[b, rows, ls].astype(F32)
            conv = _conv_taps(sbuf, b * pitch, ls, sw_ref, 0, SCONV_K, r0)
            y_ref[b, rows, ls] = (bg_ref[b, rows, ls].astype(F32) * conv).astype(y_ref.dtype)

        def qk_unit(b, ls):
            qbuf[ls.start // LANES, srows(b), :] = q_ref[b, rows, ls].astype(F32)
            kbuf[ls.start // LANES, srows(b), :] = k_ref[b, rows, ls].astype(F32)
            qa = _silu(_conv_taps(qbuf, b * pitch, ls, cw_ref, 0, GDN_CONV_K, r0))
            ka = _silu(_conv_taps(kbuf, b * pitch, ls, cw_ref, GDN_KEY_DIM, GDN_CONV_K, r0))
            qn[b, rows, ls] = qa * (lax.rsqrt(jnp.sum(qa * qa, axis=-1, keepdims=True) + L2_EPS) * (Dh ** -0.5))
            kn[b, rows, ls] = ka * lax.rsqrt(jnp.sum(ka * ka, axis=-1, keepdims=True) + L2_EPS)

        def v_unit(b, ls):
            vbuf[ls.start // LANES, srows(b), :] = v_ref[b, rows, ls].astype(F32)
            vc[b, rows, ls] = _silu(_conv_taps(vbuf, b * pitch, ls, cw_ref, 2 * GDN_KEY_DIM, GDN_CONV_K, r0))

        units = []
        for b in range(nbatch):
            for n in range(SCONV_WIDTH // LANES):
                units.append(functools.partial(sconv_unit, b, slice(n * LANES, (n + 1) * LANES)))
            for n in range(GDN_K_HEADS):
                units.append(functools.partial(qk_unit, b, slice(n * Dh, (n + 1) * Dh)))
            for n in range(GDN_V_HEADS):
                units.append(functools.partial(v_unit, b, slice(n * Dh, (n + 1) * Dh)))
        return units
    ri = lax.broadcasted_iota(jnp.int32, (C, C), 0)
    ci = lax.broadcasted_iota(jnp.int32, (C, C), 1)
    causal = ci <= ri
    strict = ci < ri
    eye = jnp.where(ri == ci, 1.0, 0.0).astype(F32)
    nw = nw_ref[...]

    def chunk_step(r0, side_work):
        rows = pl.ds(r0, C)
        side_work = list(side_work)
        per_stage = -(-len(side_work) // 9)

        def fill():
            for _ in range(min(per_stage, len(side_work))):
                side_work.pop(0)()

        batches = range(nbatch)
        chains = [(b, h) for b in batches for h in range(GDN_V_HEADS)]
        kchains = [(b, kh) for b in batches for kh in range(GDN_K_HEADS)]

        def kidx(b, h):
            return b * GDN_K_HEADS + h // 2

        gc, gct, bc, eg, e_tail, e_last = [], [], [], [], [], []
        for b in batches:
            gc.append(g_s[b, rows, :])
            gct.append(gc[b].T)
            bc.append(b_s[b, rows, :])
            eg.append(jnp.exp(gc[b]))
            g_last = g_s[b, pl.ds(r0 + C - 1, 1), :]
            e_tail.append(jnp.exp(g_last - gc[b]))
            e_last.append(jnp.exp(g_last))
        qc, kc, kcb, kk, qk = [], [], [], [], []
        for b, kh in kchains:
            ksl = slice(kh * Dh, (kh + 1) * Dh)
            qc.append(qn[b, rows, ksl])
            kc.append(kn[b, rows, ksl])
            kcb.append(kc[-1].astype(BF16))
        for n in range(len(kchains)):
            kk.append(lax.dot_general(kcb[n], kcb[n], NT_DIMS, preferred_element_type=F32))
            qk.append(lax.dot_general(qc[n].astype(BF16), kcb[n], NT_DIMS, preferred_element_type=F32))
        fill()
        decay, bcol, egc, inv, pw = [], [], [], [], []
        for b, h in chains:
            decay.append(jnp.exp(jnp.where(causal, gc[b][:, h:h + 1] - gct[b][h:h + 1, :], -1e30)))
            bcol.append(bc[b][:, h:h + 1])
            egc.append(eg[b][:, h:h + 1])
        for n, (b, h) in enumerate(chains):
            low = jnp.where(strict, kk[kidx(b, h)] * bcol[n] * decay[n], 0.0)
            lb = low.astype(BF16)
            inv.append(eye - low)
            pw.append(_dot(lb, lb))
        fill()
        for it in range(5):
            for n in range(len(chains)):
                pwb = pw[n].astype(BF16)
                both = jnp.concatenate([inv[n], pw[n]], axis=0) if it < 4 else inv[n]
                prod = _dot(both.astype(BF16), pwb)
                inv[n] = inv[n] + prod[:C]
                if it < 4:
                    pw[n] = prod[C:]
            fill()
        u, w = [], []
        for n, (b, h) in enumerate(chains):
            vsl = slice(h * Dh, (h + 1) * Dh)
            rhs = jnp.concatenate([vc[b, rows, vsl] * bcol[n], kc[kidx(b, h)] * (bcol[n] * egc[n])], axis=1)
            sol = _dot(inv[n].astype(BF16), rhs.astype(BF16))
            u.append(sol[:, :Dh])
            w.append(sol[:, Dh:].astype(BF16))
        fill()
        s_old, sb, vnb = [], [], []
        for n, (b, h) in enumerate(chains):
            s_old.append(state[b, h])
            sb.append(s_old[n].astype(BF16))
            vnb.append((u[n] - _dot(w[n], sb[n])).astype(BF16))
        fill()
        for n, (b, h) in enumerate(chains):
            vsl = slice(h * Dh, (h + 1) * Dh)
            attn = (qk[kidx(b, h)] * decay[n]).astype(BF16)
            qd = (qc[kidx(b, h)] * egc[n]).astype(BF16)
            kt = (kc[kidx(b, h)] * e_tail[b][:, h:h + 1]).astype(BF16)
            o = _dot(qd, sb[n]) + _dot(attn, vnb[n])
            state[b, h] = s_old[n] * e_last[b][:, h:h + 1] + lax.dot_general(
                kt, vnb[n], TN_DIMS, preferred_element_type=F32)
            zc = z_ref[b, rows, vsl].astype(F32)
            ms = jnp.mean(o * o, axis=-1, keepdims=True)
            yb = o * lax.rsqrt(ms + RMS_EPS) * nw * _silu(zc)
            y_ref[b, rows, SCONV_WIDTH + h * Dh:SCONV_WIDTH + (h + 1) * Dh] = yb.astype(y_ref.dtype)
        while side_work:
            fill()

    assert CONV_ROWS == C
    for unit in conv_units(0):
        unit()

    def pipelined(c, carry):
        r0 = pl.multiple_of(c * C, C)
        chunk_step(r0, conv_units(r0 + C))
        return carry

    lax.fori_loop(0, ts // C - 1, pipelined, 0)
    chunk_step(ts - C, [])
    for buf in stage_bufs:
        for b in range(nbatch):
            buf[:, b * pitch:b * pitch + HALO, :] = buf[:, b * pitch + ts:b * pitch + ts + HALO, :]


def _mixer(proj, ab, sconv_w, conv_w, gate_p, norm_w, batch, seq, ts=256):
    def rows(cb, width):
        return pl.BlockSpec((batch, ts, width), lambda s, cb=cb: (0, s, cb))

    def whole(a):
        return pl.BlockSpec(a.shape, lambda s: (0,) * a.ndim)

    width_out = SCONV_WIDTH + GDN_VALUE_DIM
    return pl.pallas_call(
        _mixer_kernel,
        grid=(seq // ts,),
        in_specs=[
            rows(COL_BG // SCONV_WIDTH, SCONV_WIDTH),
            rows(COL_CG // SCONV_WIDTH, SCONV_WIDTH),
            rows(COL_XA // SCONV_WIDTH, SCONV_WIDTH),
            rows(COL_Q // GDN_KEY_DIM, GDN_KEY_DIM),
            rows(COL_K // GDN_KEY_DIM, GDN_KEY_DIM),
            rows(COL_V // GDN_VALUE_DIM, GDN_VALUE_DIM),
            rows(COL_Z // GDN_VALUE_DIM, GDN_VALUE_DIM),
            rows(0, LANES),
            whole(sconv_w), whole(conv_w), whole(gate_p), whole(norm_w),
        ],
        out_specs=pl.BlockSpec((batch, ts, width_out), lambda s: (0, s, 0)),
        out_shape=jax.ShapeDtypeStruct((batch, seq, width_out), BF16),
        scratch_shapes=[
            pltpu.VMEM((SCONV_WIDTH // LANES, batch * (ts + HALO), LANES), F32),
            pltpu.VMEM((GDN_KEY_DIM // LANES, batch * (ts + HALO), LANES), F32),
            pltpu.VMEM((GDN_KEY_DIM // LANES, batch * (ts + HALO), LANES), F32),
            pltpu.VMEM((GDN_VALUE_DIM // LANES, batch * (ts + HALO), LANES), F32),
            pltpu.VMEM((batch, ts, GDN_KEY_DIM), F32),
            pltpu.VMEM((batch, ts, GDN_KEY_DIM), F32),
            pltpu.VMEM((batch, ts, GDN_VALUE_DIM), F32),
            pltpu.VMEM((batch, ts, LANES), F32),
            pltpu.VMEM((batch, ts, LANES), F32),
            pltpu.VMEM((batch, GDN_V_HEADS, GDN_HEAD_DIM, GDN_HEAD_DIM), F32),
        ],
        compiler_params=_params(("arbitrary",)),
        name="mixer",
    )(proj, proj, proj, proj, proj, proj, proj, ab, sconv_w, conv_w, gate_p, norm_w)


def _outproj_kernel(alpha, y_ref, x_ref, g0_ref, b0_ref, w_ref, g1_ref, b1_ref, o_ref):
    h0 = _layer_norm(x_ref[...], g0_ref[...], b0_ref[...])
    r = alpha * h0 + _dot(y_ref[...], w_ref[...])
    o_ref[...] = _layer_norm(r, g1_ref[...], b1_ref[...]).astype(o_ref.dtype)


def _outproj(y, x2d, g0, b0, w, g1, b1, alpha, tm=512):
    T, D = x2d.shape
    vec = pl.BlockSpec((1, D), lambda i: (0, 0))
    return pl.pallas_call(
        functools.partial(_outproj_kernel, alpha),
        grid=(T // tm,),
        in_specs=[
            pl.BlockSpec((tm, y.shape[1]), lambda i: (i, 0)),
            pl.BlockSpec((tm, D), lambda i: (i, 0)),
            vec, vec,
            pl.BlockSpec(w.shape, lambda i: (0, 0)),
            vec, vec,
        ],
        out_specs=pl.BlockSpec((tm, D), lambda i: (i, 0)),
        out_shape=jax.ShapeDtypeStruct((T, D), BF16),
        compiler_params=_params(("parallel",)),
        name="outproj_ln1",
    )(y, x2d, g0, b0, w, g1, b1)


def _matmul_kernel(x_ref, w_ref, o_ref):
    o_ref[...] = _dot(x_ref[...].astype(BF16), w_ref[...]).astype(o_ref.dtype)


def _matmul(x, w, tn=1024):
    M, K = x.shape
    N = w.shape[1]
    return pl.pallas_call(
        _matmul_kernel,
        grid=(N // tn,),
        in_specs=[pl.BlockSpec((M, K), lambda j: (0, 0)), pl.BlockSpec((K, tn), lambda j: (0, j))],
        out_specs=pl.BlockSpec((M, tn), lambda j: (0, j)),
        out_shape=jax.ShapeDtypeStruct((M, N), BF16),
        compiler_params=_params(("parallel",)),
        name="kvproj",
    )(x, w)


def _xattn_kernel(alpha, h_ref, wq_ref, k_ref, v_ref, wo_ref, g_ref, b_ref, wr_hi_ref, wr_lo_ref,
                  o_ref, lg_ref, acc0, acc1):
    D = h_ref.shape[1]
    tm = h_ref.shape[0]
    hd = D // XA_HEADS
    tail_rows = tm // XA_HEADS
    step_id = pl.program_id(0)

    @pl.when(step_id == 0)
    def _():
        acc1[...] = jnp.zeros(acc1.shape, F32)

    def step(acc_new, acc_done):
        def tail(n):
            rows = slice(n * tail_rows, (n + 1) * tail_rows)
            h2 = _layer_norm(acc_done[rows, :], g_ref[...], b_ref[...])
            o_ref[rows, :] = h2
            hi, lo = _split_bf16(h2)
            lg_ref[rows, :] = _dot_split(hi, lo, wr_hi_ref[...], wr_lo_ref[...])

        h1 = h_ref[...]
        acc_new[...] = alpha * h1.astype(F32)
        for hh in range(XA_HEADS):
            sl = slice(hh * hd, (hh + 1) * hd)
            qh = _dot(h1, wq_ref[:, sl]).astype(BF16)
            s = lax.dot_general(qh, k_ref[:, sl], NT_DIMS, preferred_element_type=F32) * (hd ** -0.5)
            p = jnp.exp(s - jnp.max(s, axis=-1, keepdims=True))
            p = p / jnp.sum(p, axis=-1, keepdims=True)
            oh = _dot(p.astype(BF16), v_ref[:, sl]).astype(BF16)
            acc_new[...] += _dot(oh, wo_ref[sl, :])
            tail(hh)

    parity = lax.rem(step_id, 2)
    pl.when(parity == 0)(functools.partial(step, acc0, acc1))
    pl.when(parity == 1)(functools.partial(step, acc1, acc0))


def _xattn(h1, wq, kv, wo, g, b, wr_hi, wr_lo, alpha, seq, n_mem, tm=512):
    T, D = h1.shape
    nt = T // tm
    spb = seq // tm
    vec = pl.BlockSpec((1, D), lambda s: (0, 0))
    full = pl.BlockSpec((D, D), lambda s: (0, 0))

    def tile_in(s):
        return jnp.minimum(s, nt - 1)

    def tile_out(s):
        return jnp.where(s == 0, nt, s - 1)

    return pl.pallas_call(
        functools.partial(_xattn_kernel, alpha),
        grid=(nt + 1,),
        in_specs=[
            pl.BlockSpec((tm, D), lambda s: (tile_in(s), 0)),
            full,
            pl.BlockSpec((n_mem, D), lambda s: (tile_in(s) // spb, 0)),
            pl.BlockSpec((n_mem, D), lambda s: (tile_in(s) // spb, 1)),
            full,
            vec, vec,
            pl.BlockSpec((D, LANES), lambda s: (0, 0)),
            pl.BlockSpec((D, LANES), lambda s: (0, 0)),
        ],
        out_specs=[
            pl.BlockSpec((tm, D), lambda s: (tile_out(s), 0)),
            pl.BlockSpec((tm, LANES), lambda s: (tile_out(s), 0)),
        ],
        out_shape=[
            jax.ShapeDtypeStruct((T + tm, D), F32),
            jax.ShapeDtypeStruct((T + tm, LANES), F32),
        ],
        scratch_shapes=[pltpu.VMEM((tm, D), F32), pltpu.VMEM((tm, D), F32)],
        compiler_params=_params(("arbitrary",)),
        name="xattn_ln2",
    )(h1, wq, kv, kv, wo, g, b, wr_hi, wr_lo)


def _route_kernel(lg_ref, o_ref, cnt_ref, idx_ref, carry):
    @pl.when(pl.program_id(0) == 0)
    def _():
        carry[...] = jnp.zeros(carry.shape, F32)

    lg = lg_ref[...]
    lane = lax.broadcasted_iota(jnp.int32, lg.shape, 1)
    neg = -jnp.inf
    big = jnp.int32(LANES)
    gl = jnp.where(lane < N_GROUPS, lg, neg)
    gmax = jnp.max(gl, axis=-1, keepdims=True)
    g_idx = jnp.min(jnp.where(gl == gmax, lane, big), axis=-1, keepdims=True)
    g_prob = 1.0 / jnp.sum(jnp.exp(gl - gmax), axis=-1, keepdims=True)
    e_lane = lane - N_GROUPS
    in_group = (e_lane >= g_idx * EXPERTS_PER_GROUP) & (e_lane < (g_idx + 1) * EXPERTS_PER_GROUP)
    el = jnp.where(in_group, lg, neg)
    m1 = jnp.max(el, axis=-1, keepdims=True)
    i1 = jnp.min(jnp.where(el == m1, lane, big), axis=-1, keepdims=True)
    denom = jnp.sum(jnp.exp(el - m1), axis=-1, keepdims=True)
    el2 = jnp.where(lane == i1, neg, el)
    m2 = jnp.max(el2, axis=-1, keepdims=True)
    i2 = jnp.min(jnp.where(el2 == m2, lane, big), axis=-1, keepdims=True)
    p1 = 1.0 / denom
    p2 = jnp.exp(m2 - m1) / denom
    psum = p1 + p2
    gate1 = g_prob * p1 / psum
    gate2 = g_prob * p2 / psum
    tm = lg.shape[0]
    onehot = jnp.where((lane == i1) | (lane == i2), 1.0, 0.0)
    ri = lax.broadcasted_iota(jnp.int32, (tm, tm), 0)
    ci = lax.broadcasted_iota(jnp.int32, (tm, tm), 1)
    earlier = jnp.where(ci < ri, 1.0, 0.0).astype(BF16)
    before = _dot(earlier, onehot.astype(BF16)) + carry[...]
    rank1 = jnp.sum(jnp.where(lane == i1, before, 0.0), axis=-1, keepdims=True)
    rank2 = jnp.sum(jnp.where(lane == i2, before, 0.0), axis=-1, keepdims=True)
    carry[...] += jnp.sum(onehot, axis=0, keepdims=True)
    cnt_ref[...] = carry[...]
    cols = ((i1 - N_GROUPS).astype(F32), (i2 - N_GROUPS).astype(F32), gate1, gate2, rank1, rank2)
    out = jnp.zeros(lg.shape, F32)
    for n, val in enumerate(cols):
        out = jnp.where(lane == n, val, out)
    o_ref[...] = out
    idx_ref[...] = out[:, 0:ROUTE_COLS].astype(jnp.int32)


def _route(logits, T, tm=512):
    spec = pl.BlockSpec((tm, LANES), lambda i: (i, 0))
    return pl.pallas_call(
        _route_kernel,
        grid=(T // tm,),
        in_specs=[spec],
        out_specs=[spec, pl.BlockSpec((1, LANES), lambda i: (0, 0)),
                   pl.BlockSpec((tm, ROUTE_COLS), lambda i: (i, 0))],
        out_shape=[jax.ShapeDtypeStruct((T, LANES), F32), jax.ShapeDtypeStruct((1, LANES), F32),
                   jax.ShapeDtypeStruct((T, ROUTE_COLS), jnp.int32)],
        scratch_shapes=[pltpu.VMEM((1, LANES), F32)],
        compiler_params=_params(("arbitrary",)),
        name="route",
    )(logits)


def _wait_rows(buf, sem):
    pltpu.make_async_copy(buf, buf, sem).wait()


def _expert_kernel(bexp_ref, nval_ref, off_ref, first_ref, wslot_ref, nexte_ref, order_ref,
                   h_hbm, w1_hbm, w3_hbm, w2_hbm,
                   y_ref, xbuf0, xbuf1, xbuf2, w1s, w3s, w2s, w1b, w3b, w2b, gsem, wsem):
    i = pl.program_id(0)
    last = pl.num_programs(0) - 1
    xbuf = (xbuf0, xbuf1, xbuf2)
    nbuf = len(xbuf)
    bm = xbuf0.shape[0]
    slot = lax.rem(i, nbuf)
    nv = nval_ref[i]
    nv_prev = jnp.where(i > 0, nval_ref[jnp.maximum(i - 1, 0)], 0)

    def gather_row(base, r, k, priority=0):
        tok = lax.shift_right_logical(order_ref[base + r], 1)
        pltpu.make_async_copy(h_hbm.at[pl.ds(tok, 1), :], xbuf[k].at[pl.ds(r, 1), :],
                              gsem.at[k]).start(priority)

    def for_slot(fn):
        for k in range(nbuf):
            pl.when(slot == k)(functools.partial(fn, k))

    def weight_copies(e, s):
        return (pltpu.make_async_copy(w1_hbm.at[e], w1s.at[s], wsem.at[s]),
                pltpu.make_async_copy(w3_hbm.at[e], w3s.at[s], wsem.at[s]),
                pltpu.make_async_copy(w2_hbm.at[e], w2s.at[s], wsem.at[s]))

    @pl.when(i == 0)
    def _():
        for c in weight_copies(bexp_ref[0], 0):
            c.start(priority=1)

        def body(r, c):
            gather_row(off_ref[0], r, 0)
            gather_row(off_ref[1], r, 1)
            return c
        lax.fori_loop(0, bm, body, 0)

    @pl.when(nv > 0)
    def _():
        @pl.when(first_ref[i] == 1)
        def _():
            ws = wslot_ref[i]
            nxt = nexte_ref[i]

            @pl.when(nxt >= 0)
            def _():
                for c in weight_copies(nxt, 1 - ws):
                    c.start(priority=1)

            def cast(s):
                for c in weight_copies(0, s):
                    c.wait()
                w1b[...] = w1s[s].astype(BF16)
                w3b[...] = w3s[s].astype(BF16)
                w2b[...] = w2s[s].astype(BF16)

            for s in range(2):
                pl.when(ws == s)(functools.partial(cast, s))

        ahead_base = off_ref[jnp.minimum(i + 2, last)]

        def compute(cur):
            _wait_rows(xbuf[cur], gsem.at[cur])
            for r in range(bm):
                gather_row(ahead_base, r, (cur + 2) % nbuf)
            xb = xbuf[cur][...].astype(BF16)
            hid = _silu(_dot(xb, w1b[...])) * _dot(xb, w3b[...])
            y_ref[...] = _dot(hid.astype(BF16), w2b[...])

        for_slot(compute)

    @pl.when(nv == 0)
    def _():
        y_ref[...] = jnp.zeros(y_ref.shape, y_ref.dtype)

        @pl.when(nv_prev > 0)
        def _():
            def drain(cur):
                _wait_rows(xbuf[cur], gsem.at[cur])
                _wait_rows(xbuf[(cur + 1) % nbuf], gsem.at[(cur + 1) % nbuf])
            for_slot(drain)


def _experts(h2, w1, w3, w2, tables, bm):
    bexp, nval, off, first, wslot, nexte, order = tables
    D = h2.shape[1]
    nb = bexp.shape[0]
    ff = w1.shape[2]
    hbm = pl.BlockSpec(memory_space=pl.ANY)
    grid_spec = pltpu.PrefetchScalarGridSpec(
        num_scalar_prefetch=len(tables),
        grid=(nb,),
        in_specs=[hbm, hbm, hbm, hbm],
        out_specs=pl.BlockSpec((bm, D), lambda i, *tables: (i, 0)),
        scratch_shapes=[
            pltpu.VMEM((bm, D), F32),
            pltpu.VMEM((bm, D), F32),
            pltpu.VMEM((bm, D), F32),
            pltpu.VMEM((2, D, ff), F32),
            pltpu.VMEM((2, D, ff), F32),
            pltpu.VMEM((2, ff, D), F32),
            pltpu.VMEM((D, ff), BF16),
            pltpu.VMEM((D, ff), BF16),
            pltpu.VMEM((ff, D), BF16),
            pltpu.SemaphoreType.DMA((3,)),
            pltpu.SemaphoreType.DMA((2,)),
        ],
    )
    return pl.pallas_call(
        _expert_kernel,
        grid_spec=grid_spec,
        out_shape=jax.ShapeDtypeStruct((nb * bm, D), F32),
        compiler_params=_params(("arbitrary",)),
        name="experts",
    )(*tables, h2, w1, w3, w2)


def _combine_kernel(alpha, pos_ref, h_ref, rt_ref, g_ref, b_ref, y_hbm, o_ref, ybuf0, ybuf1, ybuf2, gsem):
    i = pl.program_id(0)
    nt = pl.num_programs(0)
    K = EXPERT_TOP_K
    tm = h_ref.shape[0]
    ybuf = (ybuf0, ybuf1, ybuf2)
    nbuf = len(ybuf)
    slot = lax.rem(i, nbuf)

    def gather_row(tile, r, k, buf, priority=0):
        src = pos_ref[(tile * tm + r) * K + k]
        pltpu.make_async_copy(y_hbm.at[pl.ds(src, 1), :], ybuf[buf].at[k, pl.ds(r, 1), :],
                              gsem.at[buf]).start(priority)

    @pl.when(i == 0)
    def _():
        def body(r, c):
            for k in range(K):
                gather_row(0, r, k, 0)
                gather_row(1, r, k, 1)
            return c
        lax.fori_loop(0, tm, body, 0)

    ahead_tile = jnp.minimum(i + 2, nt - 1)
    rt = rt_ref[...]

    def compute(cur):
        _wait_rows(ybuf[cur], gsem.at[cur])
        for r in range(tm):
            for k in range(K):
                gather_row(ahead_tile, r, k, (cur + 2) % nbuf, k % 2)
        ffn = ybuf[cur][0] * rt[:, 2:3] + ybuf[cur][1] * rt[:, 3:4]
        o_ref[...] = _layer_norm(alpha * h_ref[...] + ffn, g_ref[...], b_ref[...])

        @pl.when(i == nt - 1)
        def _():
            _wait_rows(ybuf[(cur + 1) % nbuf], gsem.at[(cur + 1) % nbuf])
            _wait_rows(ybuf[(cur + 2) % nbuf], gsem.at[(cur + 2) % nbuf])

    for k in range(nbuf):
        pl.when(slot == k)(functools.partial(compute, k))


def _combine(h2, y_sorted, pos, route, g, b, alpha, tm=512):
    T, D = route.shape[0], h2.shape[1]
    vec = pl.BlockSpec((1, D), lambda i, ps: (0, 0))
    grid_spec = pltpu.PrefetchScalarGridSpec(
        num_scalar_prefetch=1,
        grid=(T // tm,),
        in_specs=[
            pl.BlockSpec((tm, D), lambda i, ps: (i, 0)),
            pl.BlockSpec((tm, LANES), lambda i, ps: (i, 0)),
            vec, vec,
            pl.BlockSpec(memory_space=pl.ANY),
        ],
        out_specs=pl.BlockSpec((tm, D), lambda i, ps: (i, 0)),
        scratch_shapes=[
            pltpu.VMEM((EXPERT_TOP_K, tm, D), F32),
            pltpu.VMEM((EXPERT_TOP_K, tm, D), F32),
            pltpu.VMEM((EXPERT_TOP_K, tm, D), F32),
            pltpu.SemaphoreType.DMA((3,)),
        ],
    )
    return pl.pallas_call(
        functools.partial(_combine_kernel, alpha),
        grid_spec=grid_spec,
        out_shape=jax.ShapeDtypeStruct((T, D), F32),
        compiler_params=_params(("arbitrary",)),
        name="combine_ln3",
    )(pos, h2, route, g, b, y_sorted)


def _routing_tables(route_idx, lane_counts, n_tok, bm):
    K = EXPERT_TOP_K
    TK = n_tok * K
    flat_e = route_idx[:, 0:K].reshape(-1)
    flat_rank = route_idx[:, 2 * K:3 * K].reshape(-1)
    order = jnp.argsort(flat_e).astype(jnp.int32)
    experts = jnp.arange(N_EXPERTS, dtype=jnp.int32)
    counts = lane_counts[0, N_GROUPS:N_GROUPS + N_EXPERTS].astype(jnp.int32)
    start = jnp.cumsum(counts) - counts
    padded = (counts + bm - 1) // bm * bm
    padded_end = jnp.cumsum(padded)
    padded_start = padded_end - padded
    nb = TK // bm + N_EXPERTS
    block_start = jnp.arange(nb, dtype=jnp.int32) * bm
    bexp = jnp.minimum(jnp.sum((block_start[:, None] >= padded_end[None, :]).astype(jnp.int32), axis=1),
                       N_EXPERTS - 1)
    of_block = bexp[:, None] == experts[None, :]

    def pick(per_expert):
        return jnp.sum(jnp.where(of_block, per_expert[None, :], 0), axis=1)

    rank0 = block_start - pick(padded_start)
    nval = jnp.clip(pick(counts) - rank0, 0, bm).astype(jnp.int32)
    off = jnp.clip(pick(start) + rank0, 0, TK).astype(jnp.int32)
    order = jnp.concatenate([order, jnp.zeros((bm,), jnp.int32)])
    pos = flat_rank + jnp.sum(jnp.where(flat_e[:, None] == experts[None, :], padded_start[None, :], 0), axis=1)
    bexp = bexp.astype(jnp.int32)
    active = counts > 0
    ordinal = jnp.cumsum(active.astype(jnp.int32)) - 1
    later_active = (experts[None, :] > experts[:, None]) & active[None, :]
    next_active = jnp.min(jnp.where(later_active, experts[None, :], N_EXPERTS), axis=1)
    next_active = jnp.where(next_active == N_EXPERTS, -1, next_active)
    prev_bexp = jnp.concatenate([jnp.full((1,), -1, jnp.int32), bexp[:-1]])
    first = ((nval > 0) & (bexp != prev_bexp)).astype(jnp.int32)
    wslot = (pick(ordinal) % 2).astype(jnp.int32)
    nexte = pick(next_active).astype(jnp.int32)
    return (bexp, nval, off, first, wslot, nexte, order), pos.astype(jnp.int32)


def kernel(x, mem, ln0_g, ln0_b, w_in, sconv_w, gdn_conv_w, gdn_a_log, gdn_dt_bias, gdn_norm_w,
           w_mix_out, ln1_g, ln1_b, xa_wq, xa_wk, xa_wv, xa_wo, ln2_g, ln2_b, w_group,
           w_expert_router, w1, w3, w2, ln3_g, ln3_b):
    B, S, D = x.shape
    T = B * S
    depth = w_in.shape[0]
    assert depth == 1, "single-layer stack: the out-projection kernel recomputes LayerNorm0(x)"
    alpha = float((2 * depth) ** 0.25)
    n_mem = mem.shape[1]
    bm = 256

    def vec(a):
        return a.reshape(1, -1).astype(F32)

    def pad_lanes(a):
        return jnp.pad(a, ((0, 0), (0, LANES - a.shape[1])))

    x2d = x.reshape(T, D)
    g0, b0 = vec(ln0_g), vec(ln0_b)
    l = 0
    w_main = w_in[l].astype(BF16)
    wab_hi, wab_lo = _split_bf16(pad_lanes(w_in[l][:, COL_AB:]))
    proj, ab = _ln_inproj(x2d, g0, b0, w_main, wab_hi, wab_lo)
    gate_p = jnp.pad(jnp.stack([gdn_a_log[l], gdn_dt_bias[l]]).astype(F32),
                     ((0, HALO - 2), (0, LANES - GDN_V_HEADS)))
    y = _mixer(proj.reshape(B, S, -1), ab.reshape(B, S, LANES), sconv_w[l].astype(F32),
               gdn_conv_w[l].astype(F32), gate_p, vec(gdn_norm_w[l]), B, S)
    h1 = _outproj(y.reshape(T, -1), x2d, g0, b0, w_mix_out[l].astype(BF16), vec(ln1_g[l]), vec(ln1_b[l]), alpha)
    kv = _matmul(mem.reshape(B * n_mem, D),
                 jnp.concatenate([xa_wk[l], xa_wv[l]], axis=1).astype(BF16))
    wr_hi, wr_lo = _split_bf16(pad_lanes(jnp.concatenate([w_group[l], w_expert_router[l]], axis=1)))
    h2, logits = _xattn(h1, xa_wq[l].astype(BF16), kv, xa_wo[l].astype(BF16), vec(ln2_g[l]),
                        vec(ln2_b[l]), wr_hi, wr_lo, alpha, S, n_mem)
    route, lane_counts, route_idx = _route(logits, T)
    tables, pos = _routing_tables(route_idx, lane_counts, T, bm)
    y_sorted = _experts(h2, w1[l], w3[l], w2[l], tables, bm)
    out = _combine(h2, y_sorted, pos, route, vec(ln3_g[l]), vec(ln3_b[l]), alpha)
    return out.reshape(B, S, D)
```

```python
import functools

import jax
import jax.numpy as jnp
from jax import lax
from jax.experimental import pallas as pl
from jax.experimental.pallas import tpu as pltpu

F32 = jnp.float32
BF16 = jnp.bfloat16

LN_EPS = 1e-5
RMS_EPS = 1e-6
L2_EPS = 1e-6

SCONV_WIDTH = 1024
SCONV_K = 3
GDN_HEAD_DIM = 128
GDN_V_HEADS = 8
GDN_K_HEADS = 4
GDN_KEY_DIM = GDN_K_HEADS * GDN_HEAD_DIM
GDN_VALUE_DIM = GDN_V_HEADS * GDN_HEAD_DIM
GDN_CONV_K = 4
GDN_CHUNK = 64
XA_HEADS = 4
N_GROUPS = 8
EXPERTS_PER_GROUP = 8
N_EXPERTS = N_GROUPS * EXPERTS_PER_GROUP
EXPERT_TOP_K = 2

LANES = 128
HALO = 8
ROUTE_COLS = 8
VMEM_LIMIT = 56 * 1024 * 1024

COL_BG = 0
COL_CG = SCONV_WIDTH
COL_XA = 2 * SCONV_WIDTH
COL_Q = 3 * SCONV_WIDTH
COL_K = COL_Q + GDN_KEY_DIM
COL_V = COL_K + GDN_KEY_DIM
COL_Z = COL_V + GDN_VALUE_DIM
COL_AB = COL_Z + GDN_VALUE_DIM

NT_DIMS = (((1,), (1,)), ((), ()))
TN_DIMS = (((0,), (0,)), ((), ()))


def _dot(a, b):
    return jnp.dot(a, b, preferred_element_type=F32)


def _split_bf16(x):
    hi = x.astype(BF16)
    lo = (x - hi.astype(F32)).astype(BF16)
    return hi, lo


def _dot_split(x_hi, x_lo, w_hi, w_lo):
    n = w_hi.shape[1]
    both = _dot(x_hi, jnp.concatenate([w_hi, w_lo], axis=1))
    return both[:, :n] + both[:, n:] + _dot(x_lo, w_hi)


def _layer_norm(x, g, b):
    mu = jnp.mean(x, axis=-1, keepdims=True)
    xc = x - mu
    var = jnp.mean(xc * xc, axis=-1, keepdims=True)
    return xc * lax.rsqrt(var + LN_EPS) * g + b


def _sigmoid(x):
    return 1.0 / (1.0 + jnp.exp(-x))


def _silu(x):
    return x * _sigmoid(x)


def _params(sem):
    return pltpu.CompilerParams(dimension_semantics=sem, vmem_limit_bytes=VMEM_LIMIT)


def _ln_inproj_kernel(x_ref, g_ref, b_ref, w_ref, wab_hi_ref, wab_lo_ref, o_ref, ab_ref, hn_ref):
    @pl.when(pl.program_id(1) == 0)
    def _():
        h = _layer_norm(x_ref[...], g_ref[...], b_ref[...])
        hi, lo = _split_bf16(h)
        hn_ref[...] = hi
        ab_ref[...] = _dot_split(hi, lo, wab_hi_ref[...], wab_lo_ref[...])

    o_ref[...] = _dot(hn_ref[...], w_ref[...]).astype(o_ref.dtype)


def _ln_inproj(x2d, g, b, w_main, wab_hi, wab_lo, tm=1024, tn=1536):
    T, D = x2d.shape
    N = COL_AB
    return pl.pallas_call(
        _ln_inproj_kernel,
        grid=(T // tm, N // tn),
        in_specs=[
            pl.BlockSpec((tm, D), lambda i, j: (i, 0)),
            pl.BlockSpec((1, D), lambda i, j: (0, 0)),
            pl.BlockSpec((1, D), lambda i, j: (0, 0)),
            pl.BlockSpec((D, tn), lambda i, j: (0, j)),
            pl.BlockSpec((D, LANES), lambda i, j: (0, 0)),
            pl.BlockSpec((D, LANES), lambda i, j: (0, 0)),
        ],
        out_specs=[
            pl.BlockSpec((tm, tn), lambda i, j: (i, j)),
            pl.BlockSpec((tm, LANES), lambda i, j: (i, 0)),
        ],
        out_shape=[
            jax.ShapeDtypeStruct((T, N), BF16),
            jax.ShapeDtypeStruct((T, LANES), F32),
        ],
        scratch_shapes=[pltpu.VMEM((tm, D), BF16)],
        compiler_params=_params(("parallel", "arbitrary")),
        name="ln_inproj",
    )(x2d, g, b, w_main, wab_hi, wab_lo)


CONV_ROWS = 64


def _conv_taps(buf, base, lanes, w_ref, w_col0, ksize, r0):
    w_lanes = slice(w_col0 + lanes.start, w_col0 + lanes.stop)
    acc = None
    for j in range(ksize):
        start = base + HALO - ksize + 1 + j + r0
        term = w_ref[j:j + 1, w_lanes] * buf[lanes.start // LANES, pl.ds(start, CONV_ROWS), :]
        acc = term if acc is None else acc + term
    return acc


def _mixer_kernel(bg_ref, cg_ref, xa_ref, q_ref, k_ref, v_ref, z_ref, ab_ref,
                  sw_ref, cw_ref, gp_ref, nw_ref,
                  y_ref,
                  sbuf, qbuf, kbuf, vbuf, qn, kn, vc, g_s, b_s, state):
    nbatch, ts = q_ref.shape[0], q_ref.shape[1]
    C = GDN_CHUNK
    Dh = GDN_HEAD_DIM
    stage_bufs = (sbuf, qbuf, kbuf, vbuf)
    pitch = ts + HALO

    @pl.when(pl.program_id(0) == 0)
    def _():
        for buf in stage_bufs:
            for b in range(nbatch):
                buf[:, b * pitch:b * pitch + HALO, :] = jnp.zeros((buf.shape[0], HALO, LANES), F32)
        state[...] = jnp.zeros(state.shape, F32)

    row = lax.broadcasted_iota(jnp.int32, (ts, ts), 0)
    col = lax.broadcasted_iota(jnp.int32, (ts, ts), 1)
    tri = jnp.where((row // C == col // C) & (col <= row), 1.0, 0.0).astype(BF16)

    for b in range(nbatch):
        ab = ab_ref[b]
        xg = ab + gp_ref[1:2, :]
        softplus = jnp.maximum(xg, 0.0) + jnp.log1p(jnp.exp(-jnp.abs(xg)))
        g = -jnp.exp(gp_ref[0:1, :]) * softplus
        b_s[b] = _sigmoid(pltpu.roll(ab, LANES - GDN_V_HEADS, axis=1))
        g1 = g.astype(BF16)
        r1 = g - g1.astype(F32)
        g2 = r1.astype(BF16)
        g3 = (r1 - g2.astype(F32)).astype(BF16)
        g_s[b] = _dot(tri, g1) + _dot(tri, g2) + _dot(tri, g3)

    def conv_units(r0):
        rows = pl.ds(r0, CONV_ROWS)

        def srows(b):
            return pl.ds(b * pitch + HALO + r0, CONV_ROWS)

        def sconv_unit(b, ls):
            sbuf[ls.start // LANES, srows(b), :] = cg_ref[b, rows, ls].astype(F32) * xa_ref[b, rows, ls].astype(F32)
            conv = _conv_taps(sbuf, b * pitch, ls, sw_ref, 0, SCONV_K, r0)
            y_ref[b, rows, ls] = (bg_ref[b, rows, ls].astype(F32) * conv).astype(y_ref.dtype)

        def qk_unit(b, ls):
            qbuf[ls.start // LANES, srows(b), :] = q_ref[b, rows, ls].astype(F32)
            kbuf[ls.start // LANES, srows(b), :] = k_ref[b, rows, ls].astype(F32)
            qa = _silu(_conv_taps(qbuf, b * pitch, ls, cw_ref, 0, GDN_CONV_K, r0))
            ka = _silu(_conv_taps(kbuf, b * pitch, ls, cw_ref, GDN_KEY_DIM, GDN_CONV_K, r0))
            qn[b, rows, ls] = qa * (lax.rsqrt(jnp.sum(qa * qa, axis=-1, keepdims=True) + L2_EPS) * (Dh ** -0.5))
            kn[b, rows, ls] = ka * lax.rsqrt(jnp.sum(ka * ka, axis=-1, keepdims=True) + L2_EPS)

        def v_unit(b, ls):
            vbuf[ls.start // LANES, srows(b), :] = v_ref[b, rows, ls].astype(F32)
            vc[b, rows, ls] = _silu(_conv_taps(vbuf, b * pitch, ls, cw_ref, 2 * GDN_KEY_DIM, GDN_CONV_K, r0))

        units = []
        for b in range(nbatch):
            for n in range(SCONV_WIDTH // LANES):
                units.append(functools.partial(sconv_unit, b, slice(n * LANES, (n + 1) * LANES)))
            for n in range(GDN_K_HEADS):
                units.append(functools.partial(qk_unit, b, slice(n * Dh, (n + 1) * Dh)))
            for n in range(GDN_V_HEADS):
                units.append(functools.partial(v_unit, b, slice(n * Dh, (n + 1) * Dh)))
        return units
    ri = lax.broadcasted_iota(jnp.int32, (C, C), 0)
    ci = lax.broadcasted_iota(jnp.int32, (C, C), 1)
    causal = ci <= ri
    strict = ci < ri
    eye = jnp.where(ri == ci, 1.0, 0.0).astype(F32)
    nw = nw_ref[...]

    def chunk_step(r0, side_work):
        rows = pl.ds(r0, C)
        side_work = list(side_work)
        per_stage = -(-len(side_work) // 6)

        def fill():
            for _ in range(min(per_stage, len(side_work))):
                side_work.pop(0)()

        batches = range(nbatch)
        chains = [(b, h) for b in batches for h in range(GDN_V_HEADS)]
        kchains = [(b, kh) for b in batches for kh in range(GDN_K_HEADS)]

        def kidx(b, h):
            return b * GDN_K_HEADS + h // 2

        gc, gct, bc, eg, e_tail, e_last = [], [], [], [], [], []
        for b in batches:
            gc.append(g_s[b, rows, :])
            gct.append(gc[b].T)
            bc.append(b_s[b, rows, :])
            eg.append(jnp.exp(gc[b]))
            g_last = g_s[b, pl.ds(r0 + C - 1, 1), :]
            e_tail.append(jnp.exp(g_last - gc[b]))
            e_last.append(jnp.exp(g_last))
        qc, kc, kcb, kk, qk = [], [], [], [], []
        for b, kh in kchains:
            ksl = slice(kh * Dh, (kh + 1) * Dh)
            qc.append(qn[b, rows, ksl])
            kc.append(kn[b, rows, ksl])
            kcb.append(kc[-1].astype(BF16))
        for n in range(len(kchains)):
            kk.append(lax.dot_general(kcb[n], kcb[n], NT_DIMS, preferred_element_type=F32))
            qk.append(lax.dot_general(qc[n].astype(BF16), kcb[n], NT_DIMS, preferred_element_type=F32))
        fill()
        decay, bcol, egc, inv, pw = [], [], [], [], []
        for b, h in chains:
            decay.append(jnp.exp(jnp.where(causal, gc[b][:, h:h + 1] - gct[b][h:h + 1, :], -1e30)))
            bcol.append(bc[b][:, h:h + 1])
            egc.append(eg[b][:, h:h + 1])
        for n, (b, h) in enumerate(chains):
            low = jnp.where(strict, kk[kidx(b, h)] * bcol[n] * decay[n], 0.0)
            lb = low.astype(BF16)
            inv.append(eye - low)
            pw.append(_dot(lb, lb))
        fill()
        for it in range(5):
            for n in range(len(chains)):
                pwb = pw[n].astype(BF16)
                both = jnp.concatenate([inv[n], pw[n]], axis=0) if it < 4 else inv[n]
                prod = _dot(both.astype(BF16), pwb)
                inv[n] = inv[n] + prod[:C]
                if it < 4:
                    pw[n] = prod[C:]
            fill()
        u, w = [], []
        for n, (b, h) in enumerate(chains):
            vsl = slice(h * Dh, (h + 1) * Dh)
            rhs = jnp.concatenate([vc[b, rows, vsl] * bcol[n], kc[kidx(b, h)] * (bcol[n] * egc[n])], axis=1)
            sol = _dot(inv[n].astype(BF16), rhs.astype(BF16))
            u.append(sol[:, :Dh])
            w.append(sol[:, Dh:].astype(BF16))
        fill()
        s_old, sb, vnb = [], [], []
        for n, (b, h) in enumerate(chains):
            s_old.append(state[b, h])
            sb.append(s_old[n].astype(BF16))
            vnb.append((u[n] - _dot(w[n], sb[n])).astype(BF16))
        fill()
        for n, (b, h) in enumerate(chains):
            vsl = slice(h * Dh, (h + 1) * Dh)
            attn = (qk[kidx(b, h)] * decay[n]).astype(BF16)
            qd = (qc[kidx(b, h)] * egc[n]).astype(BF16)
            kt = (kc[kidx(b, h)] * e_tail[b][:, h:h + 1]).astype(BF16)
            o = _dot(qd, sb[n]) + _dot(attn, vnb[n])
            state[b, h] = s_old[n] * e_last[b][:, h:h + 1] + lax.dot_general(
                kt, vnb[n], TN_DIMS, preferred_element_type=F32)
            zc = z_ref[b, rows, vsl].astype(F32)
            ms = jnp.mean(o * o, axis=-1, keepdims=True)
            yb = o * lax.rsqrt(ms + RMS_EPS) * nw * _silu(zc)
            y_ref[b, rows, SCONV_WIDTH + h * Dh:SCONV_WIDTH + (h + 1) * Dh] = yb.astype(y_ref.dtype)
        while side_work:
            fill()

    assert CONV_ROWS == C
    for unit in conv_units(0):
        unit()

    def pipelined(c, carry):
        r0 = pl.multiple_of(c * C, C)
        chunk_step(r0, conv_units(r0 + C))
        return carry

    lax.fori_loop(0, ts // C - 1, pipelined, 0)
    chunk_step(ts - C, [])
    for buf in stage_bufs:
        for b in range(nbatch):
            buf[:, b * pitch:b * pitch + HALO, :] = buf[:, b * pitch + ts:b * pitch + ts + HALO, :]


def _mixer(proj, ab, sconv_w, conv_w, gate_p, norm_w, batch, seq, ts=256):
    def rows(cb, width):
        return pl.BlockSpec((batch, ts, width), lambda s, cb=cb: (0, s, cb))

    def whole(a):
        return pl.BlockSpec(a.shape, lambda s: (0,) * a.ndim)

    width_out = SCONV_WIDTH + GDN_VALUE_DIM
    return pl.pallas_call(
        _mixer_kernel,
        grid=(seq // ts,),
        in_specs=[
            rows(COL_BG // SCONV_WIDTH, SCONV_WIDTH),
            rows(COL_CG // SCONV_WIDTH, SCONV_WIDTH),
            rows(COL_XA // SCONV_WIDTH, SCONV_WIDTH),
            rows(COL_Q // GDN_KEY_DIM, GDN_KEY_DIM),
            rows(COL_K // GDN_KEY_DIM, GDN_KEY_DIM),
            rows(COL_V // GDN_VALUE_DIM, GDN_VALUE_DIM),
            rows(COL_Z // GDN_VALUE_DIM, GDN_VALUE_DIM),
            rows(0, LANES),
            whole(sconv_w), whole(conv_w), whole(gate_p), whole(norm_w),
        ],
        out_specs=pl.BlockSpec((batch, ts, width_out), lambda s: (0, s, 0)),
        out_shape=jax.ShapeDtypeStruct((batch, seq, width_out), BF16),
        scratch_shapes=[
            pltpu.VMEM((SCONV_WIDTH // LANES, batch * (ts + HALO), LANES), F32),
            pltpu.VMEM((GDN_KEY_DIM // LANES, batch * (ts + HALO), LANES), F32),
            pltpu.VMEM((GDN_KEY_DIM // LANES, batch * (ts + HALO), LANES), F32),
            pltpu.VMEM((GDN_VALUE_DIM // LANES, batch * (ts + HALO), LANES), F32),
            pltpu.VMEM((batch, ts, GDN_KEY_DIM), F32),
            pltpu.VMEM((batch, ts, GDN_KEY_DIM), F32),
            pltpu.VMEM((batch, ts, GDN_VALUE_DIM), F32),
            pltpu.VMEM((batch, ts, LANES), F32),
            pltpu.VMEM((batch, ts, LANES), F32),
            pltpu.VMEM((batch, GDN_V_HEADS, GDN_HEAD_DIM, GDN_HEAD_DIM), F32),
        ],
        compiler_params=_params(("arbitrary",)),
        name="mixer",
    )(proj, proj, proj, proj, proj, proj, proj, ab, sconv_w, conv_w, gate_p, norm_w)


def _outproj_kernel(alpha, y_ref, x_ref, g0_ref, b0_ref, w_ref, g1_ref, b1_ref, o_ref):
    h0 = _layer_norm(x_ref[...], g0_ref[...], b0_ref[...])
    r = alpha * h0 + _dot(y_ref[...], w_ref[...])
    o_ref[...] = _layer_norm(r, g1_ref[...], b1_ref[...]).astype(o_ref.dtype)


def _outproj(y, x2d, g0, b0, w, g1, b1, alpha, tm=512):
    T, D = x2d.shape
    vec = pl.BlockSpec((1, D), lambda i: (0, 0))
    return pl.pallas_call(
        functools.partial(_outproj_kernel, alpha),
        grid=(T // tm,),
        in_specs=[
            pl.BlockSpec((tm, y.shape[1]), lambda i: (i, 0)),
            pl.BlockSpec((tm, D), lambda i: (i, 0)),
            vec, vec,
            pl.BlockSpec(w.shape, lambda i: (0, 0)),
            vec, vec,
        ],
        out_specs=pl.BlockSpec((tm, D), lambda i: (i, 0)),
        out_shape=jax.ShapeDtypeStruct((T, D), BF16),
        compiler_params=_params(("parallel",)),
        name="outproj_ln1",
    )(y, x2d, g0, b0, w, g1, b1)


def _matmul_kernel(x_ref, w_ref, o_ref):
    o_ref[...] = _dot(x_ref[...].astype(BF16), w_ref[...]).astype(o_ref.dtype)


def _matmul(x, w, tn=1024):
    M, K = x.shape
    N = w.shape[1]
    return pl.pallas_call(
        _matmul_kernel,
        grid=(N // tn,),
        in_specs=[pl.BlockSpec((M, K), lambda j: (0, 0)), pl.BlockSpec((K, tn), lambda j: (0, j))],
        out_specs=pl.BlockSpec((M, tn), lambda j: (0, j)),
        out_shape=jax.ShapeDtypeStruct((M, N), BF16),
        compiler_params=_params(("parallel",)),
        name="kvproj",
    )(x, w)


def _xattn_kernel(alpha, h_ref, wq_ref, k_ref, v_ref, wo_ref, g_ref, b_ref, wr_hi_ref, wr_lo_ref,
                  o_ref, lg_ref, acc0, acc1):
    D = h_ref.shape[1]
    tm = h_ref.shape[0]
    hd = D // XA_HEADS
    tail_rows = tm // XA_HEADS
    step_id = pl.program_id(0)

    @pl.when(step_id == 0)
    def _():
        acc1[...] = jnp.zeros(acc1.shape, F32)

    def step(acc_new, acc_done):
        def tail(n):
            rows = slice(n * tail_rows, (n + 1) * tail_rows)
            h2 = _layer_norm(acc_done[rows, :], g_ref[...], b_ref[...])
            o_ref[rows, :] = h2
            hi, lo = _split_bf16(h2)
            lg_ref[rows, :] = _dot_split(hi, lo, wr_hi_ref[...], wr_lo_ref[...])

        h1 = h_ref[...]
        acc_new[...] = alpha * h1.astype(F32)
        for hh in range(XA_HEADS):
            sl = slice(hh * hd, (hh + 1) * hd)
            qh = _dot(h1, wq_ref[:, sl]).astype(BF16)
            s = lax.dot_general(qh, k_ref[:, sl], NT_DIMS, preferred_element_type=F32) * (hd ** -0.5)
            p = jnp.exp(s - jnp.max(s, axis=-1, keepdims=True))
            p = p / jnp.sum(p, axis=-1, keepdims=True)
            oh = _dot(p.astype(BF16), v_ref[:, sl]).astype(BF16)
            acc_new[...] += _dot(oh, wo_ref[sl, :])
            tail(hh)

    parity = lax.rem(step_id, 2)
    pl.when(parity == 0)(functools.partial(step, acc0, acc1))
    pl.when(parity == 1)(functools.partial(step, acc1, acc0))


def _xattn(h1, wq, kv, wo, g, b, wr_hi, wr_lo, alpha, seq, n_mem, tm=512):
    T, D = h1.shape
    nt = T // tm
    spb = seq // tm
    vec = pl.BlockSpec((1, D), lambda s: (0, 0))
    full = pl.BlockSpec((D, D), lambda s: (0, 0))

    def tile_in(s):
        return jnp.minimum(s, nt - 1)

    def tile_out(s):
        return jnp.where(s == 0, nt, s - 1)

    return pl.pallas_call(
        functools.partial(_xattn_kernel, alpha),
        grid=(nt + 1,),
        in_specs=[
            pl.BlockSpec((tm, D), lambda s: (tile_in(s), 0)),
            full,
            pl.BlockSpec((n_mem, D), lambda s: (tile_in(s) // spb, 0)),
            pl.BlockSpec((n_mem, D), lambda s: (tile_in(s) // spb, 1)),
            full,
            vec, vec,
            pl.BlockSpec((D, LANES), lambda s: (0, 0)),
            pl.BlockSpec((D, LANES), lambda s: (0, 0)),
        ],
        out_specs=[
            pl.BlockSpec((tm, D), lambda s: (tile_out(s), 0)),
            pl.BlockSpec((tm, LANES), lambda s: (tile_out(s), 0)),
        ],
        out_shape=[
            jax.ShapeDtypeStruct((T + tm, D), F32),
            jax.ShapeDtypeStruct((T + tm, LANES), F32),
        ],
        scratch_shapes=[pltpu.VMEM((tm, D), F32), pltpu.VMEM((tm, D), F32)],
        compiler_params=_params(("arbitrary",)),
        name="xattn_ln2",
    )(h1, wq, kv, kv, wo, g, b, wr_hi, wr_lo)


def _route_kernel(lg_ref, o_ref, cnt_ref, idx_ref, carry):
    @pl.when(pl.program_id(0) == 0)
    def _():
        carry[...] = jnp.zeros(carry.shape, F32)

    lg = lg_ref[...]
    lane = lax.broadcasted_iota(jnp.int32, lg.shape, 1)
    neg = -jnp.inf
    big = jnp.int32(LANES)
    gl = jnp.where(lane < N_GROUPS, lg, neg)
    gmax = jnp.max(gl, axis=-1, keepdims=True)
    g_idx = jnp.min(jnp.where(gl == gmax, lane, big), axis=-1, keepdims=True)
    g_prob = 1.0 / jnp.sum(jnp.exp(gl - gmax), axis=-1, keepdims=True)
    e_lane = lane - N_GROUPS
    in_group = (e_lane >= g_idx * EXPERTS_PER_GROUP) & (e_lane < (g_idx + 1) * EXPERTS_PER_GROUP)
    el = jnp.where(in_group, lg, neg)
    m1 = jnp.max(el, axis=-1, keepdims=True)
    i1 = jnp.min(jnp.where(el == m1, lane, big), axis=-1, keepdims=True)
    denom = jnp.sum(jnp.exp(el - m1), axis=-1, keepdims=True)
    el2 = jnp.where(lane == i1, neg, el)
    m2 = jnp.max(el2, axis=-1, keepdims=True)
    i2 = jnp.min(jnp.where(el2 == m2, lane, big), axis=-1, keepdims=True)
    p1 = 1.0 / denom
    p2 = jnp.exp(m2 - m1) / denom
    psum = p1 + p2
    gate1 = g_prob * p1 / psum
    gate2 = g_prob * p2 / psum
    tm = lg.shape[0]
    onehot = jnp.where((lane == i1) | (lane == i2), 1.0, 0.0)
    ri = lax.broadcasted_iota(jnp.int32, (tm, tm), 0)
    ci = lax.broadcasted_iota(jnp.int32, (tm, tm), 1)
    earlier = jnp.where(ci < ri, 1.0, 0.0).astype(BF16)
    before = _dot(earlier, onehot.astype(BF16)) + carry[...]
    rank1 = jnp.sum(jnp.where(lane == i1, before, 0.0), axis=-1, keepdims=True)
    rank2 = jnp.sum(jnp.where(lane == i2, before, 0.0), axis=-1, keepdims=True)
    carry[...] += jnp.sum(onehot, axis=0, keepdims=True)
    cnt_ref[...] = carry[...]
    cols = ((i1 - N_GROUPS).astype(F32), (i2 - N_GROUPS).astype(F32), gate1, gate2, rank1, rank2)
    out = jnp.zeros(lg.shape, F32)
    for n, val in enumerate(cols):
        out = jnp.where(lane == n, val, out)
    o_ref[...] = out
    idx_ref[...] = out[:, 0:ROUTE_COLS].astype(jnp.int32)


def _route(logits, T, tm=512):
    spec = pl.BlockSpec((tm, LANES), lambda i: (i, 0))
    return pl.pallas_call(
        _route_kernel,
        grid=(T // tm,),
        in_specs=[spec],
        out_specs=[spec, pl.BlockSpec((1, LANES), lambda i: (0, 0)),
                   pl.BlockSpec((tm, ROUTE_COLS), lambda i: (i, 0))],
        out_shape=[jax.ShapeDtypeStruct((T, LANES), F32), jax.ShapeDtypeStruct((1, LANES), F32),
                   jax.ShapeDtypeStruct((T, ROUTE_COLS), jnp.int32)],
        scratch_shapes=[pltpu.VMEM((1, LANES), F32)],
        compiler_params=_params(("arbitrary",)),
        name="route",
    )(logits)


def _wait_rows(buf, sem):
    pltpu.make_async_copy(buf, buf, sem).wait()


def _expert_kernel(bexp_ref, nval_ref, off_ref, first_ref, wslot_ref, nexte_ref, order_ref,
                   h_hbm, w1_hbm, w3_hbm, w2_hbm,
                   y_ref, xbuf0, xbuf1, xbuf2, w1s, w3s, w2s, w1b, w3b, w2b, gsem, wsem):
    i = pl.program_id(0)
    last = pl.num_programs(0) - 1
    xbuf = (xbuf0, xbuf1, xbuf2)
    nbuf = len(xbuf)
    bm = xbuf0.shape[0]
    slot = lax.rem(i, nbuf)
    nv = nval_ref[i]
    nv_prev = jnp.where(i > 0, nval_ref[jnp.maximum(i - 1, 0)], 0)

    def gather_row(base, r, k, priority=0):
        tok = lax.shift_right_logical(order_ref[base + r], 1)
        pltpu.make_async_copy(h_hbm.at[pl.ds(tok, 1), :], xbuf[k].at[pl.ds(r, 1), :],
                              gsem.at[k]).start(priority)

    def for_slot(fn):
        for k in range(nbuf):
            pl.when(slot == k)(functools.partial(fn, k))

    def weight_copies(e, s):
        return (pltpu.make_async_copy(w1_hbm.at[e], w1s.at[s], wsem.at[s]),
                pltpu.make_async_copy(w3_hbm.at[e], w3s.at[s], wsem.at[s]),
                pltpu.make_async_copy(w2_hbm.at[e], w2s.at[s], wsem.at[s]))

    @pl.when(i == 0)
    def _():
        for c in weight_copies(bexp_ref[0], 0):
            c.start(priority=1)

        def body(r, c):
            gather_row(off_ref[0], r, 0)
            gather_row(off_ref[1], r, 1)
            return c
        lax.fori_loop(0, bm, body, 0)

    @pl.when(nv > 0)
    def _():
        @pl.when(first_ref[i] == 1)
        def _():
            ws = wslot_ref[i]
            nxt = nexte_ref[i]

            @pl.when(nxt >= 0)
            def _():
                for c in weight_copies(nxt, 1 - ws):
                    c.start(priority=1)

            def cast(s):
                for c in weight_copies(0, s):
                    c.wait()
                w1b[...] = w1s[s].astype(BF16)
                w3b[...] = w3s[s].astype(BF16)
                w2b[...] = w2s[s].astype(BF16)

            for s in range(2):
                pl.when(ws == s)(functools.partial(cast, s))

        ahead_base = off_ref[jnp.minimum(i + 2, last)]

        def compute(cur):
            _wait_rows(xbuf[cur], gsem.at[cur])
            for r in range(bm):
                gather_row(ahead_base, r, (cur + 2) % nbuf)
            xb = xbuf[cur][...].astype(BF16)
            hid = _silu(_dot(xb, w1b[...])) * _dot(xb, w3b[...])
            y_ref[...] = _dot(hid.astype(BF16), w2b[...])

        for_slot(compute)

    @pl.when(nv == 0)
    def _():
        y_ref[...] = jnp.zeros(y_ref.shape, y_ref.dtype)

        @pl.when(nv_prev > 0)
        def _():
            def drain(cur):
                _wait_rows(xbuf[cur], gsem.at[cur])
                _wait_rows(xbuf[(cur + 1) % nbuf], gsem.at[(cur + 1) % nbuf])
            for_slot(drain)


def _experts(h2, w1, w3, w2, tables, bm):
    bexp, nval, off, first, wslot, nexte, order = tables
    D = h2.shape[1]
    nb = bexp.shape[0]
    ff = w1.shape[2]
    hbm = pl.BlockSpec(memory_space=pl.ANY)
    grid_spec = pltpu.PrefetchScalarGridSpec(
        num_scalar_prefetch=len(tables),
        grid=(nb,),
        in_specs=[hbm, hbm, hbm, hbm],
        out_specs=pl.BlockSpec((bm, D), lambda i, *tables: (i, 0)),
        scratch_shapes=[
            pltpu.VMEM((bm, D), F32),
            pltpu.VMEM((bm, D), F32),
            pltpu.VMEM((bm, D), F32),
            pltpu.VMEM((2, D, ff), F32),
            pltpu.VMEM((2, D, ff), F32),
            pltpu.VMEM((2, ff, D), F32),
            pltpu.VMEM((D, ff), BF16),
            pltpu.VMEM((D, ff), BF16),
            pltpu.VMEM((ff, D), BF16),
            pltpu.SemaphoreType.DMA((3,)),
            pltpu.SemaphoreType.DMA((2,)),
        ],
    )
    return pl.pallas_call(
        _expert_kernel,
        grid_spec=grid_spec,
        out_shape=jax.ShapeDtypeStruct((nb * bm, D), F32),
        compiler_params=_params(("arbitrary",)),
        name="experts",
    )(*tables, h2, w1, w3, w2)


def _combine_kernel(alpha, pos_ref, h_ref, rt_ref, g_ref, b_ref, y_hbm, o_ref, ybuf0, ybuf1, ybuf2, gsem):
    i = pl.program_id(0)
    nt = pl.num_programs(0)
    K = EXPERT_TOP_K
    tm = h_ref.shape[0]
    ybuf = (ybuf0, ybuf1, ybuf2)
    nbuf = len(ybuf)
    slot = lax.rem(i, nbuf)

    def gather_row(tile, r, k, buf, priority=0):
        src = pos_ref[(tile * tm + r) * K + k]
        pltpu.make_async_copy(y_hbm.at[pl.ds(src, 1), :], ybuf[buf].at[k, pl.ds(r, 1), :],
                              gsem.at[buf]).start(priority)

    @pl.when(i == 0)
    def _():
        def body(r, c):
            for k in range(K):
                gather_row(0, r, k, 0)
                gather_row(1, r, k, 1)
            return c
        lax.fori_loop(0, tm, body, 0)

    ahead_tile = jnp.minimum(i + 2, nt - 1)
    rt = rt_ref[...]

    def compute(cur):
        _wait_rows(ybuf[cur], gsem.at[cur])
        for r in range(tm):
            for k in range(K):
                gather_row(ahead_tile, r, k, (cur + 2) % nbuf, k % 2)
        ffn = ybuf[cur][0] * rt[:, 2:3] + ybuf[cur][1] * rt[:, 3:4]
        o_ref[...] = _layer_norm(alpha * h_ref[...] + ffn, g_ref[...], b_ref[...])

        @pl.when(i == nt - 1)
        def _():
            _wait_rows(ybuf[(cur + 1) % nbuf], gsem.at[(cur + 1) % nbuf])
            _wait_rows(ybuf[(cur + 2) % nbuf], gsem.at[(cur + 2) % nbuf])

    for k in range(nbuf):
        pl.when(slot == k)(functools.partial(compute, k))


def _combine(h2, y_sorted, pos, route, g, b, alpha, tm=512):
    T, D = route.shape[0], h2.shape[1]
    vec = pl.BlockSpec((1, D), lambda i, ps: (0, 0))
    grid_spec = pltpu.PrefetchScalarGridSpec(
        num_scalar_prefetch=1,
        grid=(T // tm,),
        in_specs=[
            pl.BlockSpec((tm, D), lambda i, ps: (i, 0)),
            pl.BlockSpec((tm, LANES), lambda i, ps: (i, 0)),
            vec, vec,
            pl.BlockSpec(memory_space=pl.ANY),
        ],
        out_specs=pl.BlockSpec((tm, D), lambda i, ps: (i, 0)),
        scratch_shapes=[
            pltpu.VMEM((EXPERT_TOP_K, tm, D), F32),
            pltpu.VMEM((EXPERT_TOP_K, tm, D), F32),
            pltpu.VMEM((EXPERT_TOP_K, tm, D), F32),
            pltpu.SemaphoreType.DMA((3,)),
        ],
    )
    return pl.pallas_call(
        functools.partial(_combine_kernel, alpha),
        grid_spec=grid_spec,
        out_shape=jax.ShapeDtypeStruct((T, D), F32),
        compiler_params=_params(("arbitrary",)),
        name="combine_ln3",
    )(pos, h2, route, g, b, y_sorted)


def _routing_tables(route_idx, lane_counts, n_tok, bm):
    K = EXPERT_TOP_K
    TK = n_tok * K
    flat_e = route_idx[:, 0:K].reshape(-1)
    flat_rank = route_idx[:, 2 * K:3 * K].reshape(-1)
    order = jnp.argsort(flat_e).astype(jnp.int32)
    experts = jnp.arange(N_EXPERTS, dtype=jnp.int32)
    counts = lane_counts[0, N_GROUPS:N_GROUPS + N_EXPERTS].astype(jnp.int32)
    start = jnp.cumsum(counts) - counts
    padded = (counts + bm - 1) // bm * bm
    padded_end = jnp.cumsum(padded)
    padded_start = padded_end - padded
    nb = TK // bm + N_EXPERTS
    block_start = jnp.arange(nb, dtype=jnp.int32) * bm
    bexp = jnp.minimum(jnp.sum((block_start[:, None] >= padded_end[None, :]).astype(jnp.int32), axis=1),
                       N_EXPERTS - 1)
    of_block = bexp[:, None] == experts[None, :]

    def pick(per_expert):
        return jnp.sum(jnp.where(of_block, per_expert[None, :], 0), axis=1)

    rank0 = block_start - pick(padded_start)
    nval = jnp.clip(pick(counts) - rank0, 0, bm).astype(jnp.int32)
    off = jnp.clip(pick(start) + rank0, 0, TK).astype(jnp.int32)
    order = jnp.concatenate([order, jnp.zeros((bm,), jnp.int32)])
    pos = flat_rank + jnp.sum(jnp.where(flat_e[:, None] == experts[None, :], padded_start[None, :], 0), axis=1)
    bexp = bexp.astype(jnp.int32)
    active = counts > 0
    ordinal = jnp.cumsum(active.astype(jnp.int32)) - 1
    later_active = (experts[None, :] > experts[:, None]) & active[None, :]
    next_active = jnp.min(jnp.where(later_active, experts[None, :], N_EXPERTS), axis=1)
    next_active = jnp.where(next_active == N_EXPERTS, -1, next_active)
    prev_bexp = jnp.concatenate([jnp.full((1,), -1, jnp.int32), bexp[:-1]])
    first = ((nval > 0) & (bexp != prev_bexp)).astype(jnp.int32)
    wslot = (pick(ordinal) % 2).astype(jnp.int32)
    nexte = pick(next_active).astype(jnp.int32)
    return (bexp, nval, off, first, wslot, nexte, order), pos.astype(jnp.int32)


def kernel(x, mem, ln0_g, ln0_b, w_in, sconv_w, gdn_conv_w, gdn_a_log, gdn_dt_bias, gdn_norm_w,
           w_mix_out, ln1_g, ln1_b, xa_wq, xa_wk, xa_wv, xa_wo, ln2_g, ln2_b, w_group,
           w_expert_router, w1, w3, w2, ln3_g, ln3_b):
    B, S, D = x.shape
    T = B * S
    depth = w_in.shape[0]
    assert depth == 1, "single-layer stack: the out-projection kernel recomputes LayerNorm0(x)"
    alpha = float((2 * depth) ** 0.25)
    n_mem = mem.shape[1]
    bm = 256

    def vec(a):
        return a.reshape(1, -1).astype(F32)

    def pad_lanes(a):
        return jnp.pad(a, ((0, 0), (0, LANES - a.shape[1])))

    x2d = x.reshape(T, D)
    g0, b0 = vec(ln0_g), vec(ln0_b)
    l = 0
    w_main = w_in[l].astype(BF16)
    wab_hi, wab_lo = _split_bf16(pad_lanes(w_in[l][:, COL_AB:]))
    proj, ab = _ln_inproj(x2d, g0, b0, w_main, wab_hi, wab_lo)
    gate_p = jnp.pad(jnp.stack([gdn_a_log[l], gdn_dt_bias[l]]).astype(F32),
                     ((0, HALO - 2), (0, LANES - GDN_V_HEADS)))
    y = _mixer(proj.reshape(B, S, -1), ab.reshape(B, S, LANES), sconv_w[l].astype(F32),
               gdn_conv_w[l].astype(F32), gate_p, vec(gdn_norm_w[l]), B, S)
    h1 = _outproj(y.reshape(T, -1), x2d, g0, b0, w_mix_out[l].astype(BF16), vec(ln1_g[l]), vec(ln1_b[l]), alpha)
    kv = _matmul(mem.reshape(B * n_mem, D),
                 jnp.concatenate([xa_wk[l], xa_wv[l]], axis=1).astype(BF16))
    wr_hi, wr_lo = _split_bf16(pad_lanes(jnp.concatenate([w_group[l], w_expert_router[l]], axis=1)))
    h2, logits = _xattn(h1, xa_wq[l].astype(BF16), kv, xa_wo[l].astype(BF16), vec(ln2_g[l]),
                        vec(ln2_b[l]), wr_hi, wr_lo, alpha, S, n_mem)
    route, lane_counts, route_idx = _route(logits, T)
    tables, pos = _routing_tables(route_idx, lane_counts, T, bm)
    y_sorted = _experts(h2, w1[l], w3[l], w2[l], tables, bm)
    out = _combine(h2, y_sorted, pos, route, vec(ln3_g[l]), vec(ln3_b[l]), alpha)
    return out.reshape(B, S, D)
```
